```python
import jax
import jax.numpy as jnp
from jax import lax
import numpy as np

D_MODEL = 1024
BATCH = 16
SEQ = 256
DEPTH = 4
DEC_BATCH = 8
DEC_SEQ = 1024
PAST_LEN = 256

GRID_W = 64
EPS = 1e-6
ATT_HEADS = 8
ATT_KV_HEADS = 2
HEAD_DIM = 64
GQA_GROUP = ATT_HEADS // ATT_KV_HEADS
ATT_Q = ATT_HEADS * HEAD_DIM
ATT_KV = ATT_KV_HEADS * HEAD_DIM
ROPE_THETA = 10000.0
Q_BLOCK = 128
RWKV_HEADS = 8
RWKV_HD = 64
RWKV_DIM = RWKV_HEADS * RWKV_HD
DECAY_RANK = 64
ICLR_RANK = 64
GATE_RANK = 128
RWKV_COLS = 3 * RWKV_DIM + 2 * DECAY_RANK + 2 * ICLR_RANK + GATE_RANK
GN_EPS = 64e-5
EVEN_IN = ATT_Q + 2 * ATT_KV + RWKV_COLS
EVEN_MIX = ATT_Q + RWKV_DIM
D_RNN = D_MODEL
LRU_BLOCKS = 16
LRU_BS = D_RNN // LRU_BLOCKS
CONV_W = 4
LRU_C = 8.0
N_EXPERTS = 32
TOP_K = 4
D_EXPERT = D_MODEL
SWIGLU_LIMIT = 7.0
SWIGLU_ALPHA = 1.702
MOE_BLOCK = 128

kernel_name = 'hybrid_flow_prefix_trunk_step'


def rms_norm(x, g):
    xf = x.astype(jnp.float32)
    y = xf * lax.rsqrt(jnp.mean(xf * xf, axis=-1, keepdims=True) + EPS)
    return (y * g.astype(jnp.float32)).astype(x.dtype)


def modulate(h, shift, scale):
    return h * (1.0 + scale) + shift


def axial_rope(n_tokens):
    rows = n_tokens // GRID_W
    t = jnp.arange(rows * GRID_W)
    pos = jnp.stack([t // GRID_W, t % GRID_W], axis=-1).astype(jnp.float32)
    n_freq = HEAD_DIM // 4
    inv = ROPE_THETA ** (-jnp.arange(n_freq, dtype=jnp.float32) / n_freq)
    ang = (pos[:, :, None] * inv).reshape(rows * GRID_W, 2 * n_freq)
    return jnp.cos(ang), jnp.sin(ang)


def apply_rope(x, cos, sin):
    xf = x.astype(jnp.float32)
    x1, x2 = xf[..., 0::2], xf[..., 1::2]
    c, s = cos[None, :, None, :], sin[None, :, None, :]
    return jnp.stack([x1 * c - x2 * s, x1 * s + x2 * c], axis=-1).reshape(x.shape).astype(x.dtype)


def blocked_attention(q, k, v):
    B, T = q.shape[0], q.shape[1]
    n_blk = T // Q_BLOCK
    qf = (q.astype(jnp.float32) * HEAD_DIM ** -0.5).reshape(B, n_blk, Q_BLOCK, ATT_KV_HEADS, GQA_GROUP, HEAD_DIM)
    qf = jnp.moveaxis(qf, 1, 0)
    kf, vf = k.astype(jnp.float32), v.astype(jnp.float32)

    def one_block(qb):
        s = jnp.einsum('bqkgd,bskd->bkgqs', qb, kf)
        p = jax.nn.softmax(s, axis=-1)
        return jnp.einsum('bkgqs,bskd->bqkgd', p, vf)

    o = lax.map(one_block, qf)
    return jnp.moveaxis(o, 0, 1).reshape(B, T, ATT_Q).astype(q.dtype)


def centred_shift_mix(u, mu):
    prev = jnp.pad(u[:, :-1], ((0, 0), (1, 0), (0, 0)))
    nxt = jnp.pad(u[:, 1:], ((0, 0), (0, 1), (0, 0)))
    return u + mu * (0.5 * (prev + nxt) - u)


def rwkv7_scan(r, w, k, v, kk, a, s0, reverse):
    xs = tuple(jnp.swapaxes(t, 0, 1) for t in (r, w, k, v, kk, a))

    def step(S, inp):
        r_t, w_t, k_t, v_t, kk_t, a_t = inp
        sa = jnp.einsum('bhvk,bhk->bhv', S, -kk_t)
        S = S * w_t[:, :, None, :] + sa[..., None] * (kk_t * a_t)[:, :, None, :] + v_t[..., None] * k_t[:, :, None, :]
        return S, jnp.einsum('bhvk,bhk->bhv', S, r_t)

    S, out = lax.scan(step, s0, xs, reverse=reverse)
    return S, jnp.swapaxes(out, 0, 1)


def rwkv7_mix(u, s0, w0, w_up, a0, a_up, g_up, k_k, k_a, r_k, ln_g, ln_b):
    B, T, _ = u.shape
    r, k, v = u[..., :RWKV_DIM], u[..., RWKV_DIM:2 * RWKV_DIM], u[..., 2 * RWKV_DIM:3 * RWKV_DIM]
    o = 3 * RWKV_DIM
    wd = u[..., o:o + 2 * DECAY_RANK].reshape(B, T, 2, DECAY_RANK)
    o += 2 * DECAY_RANK
    ad = u[..., o:o + 2 * ICLR_RANK].reshape(B, T, 2, ICLR_RANK)
    o += 2 * ICLR_RANK
    gd = u[..., o:o + GATE_RANK]

    def heads(t):
        return t.reshape(B, T, RWKV_HEADS, RWKV_HD).astype(jnp.float32)

    kk = heads(k * k_k)
    kk = kk / jnp.maximum(jnp.sqrt(jnp.sum(kk * kk, axis=-1, keepdims=True)), 1e-12)
    rh, vh = heads(r), heads(v)
    outs, bonuses, finals = [], [], []
    for d in range(2):
        w_log = (w0[d] + jnp.tanh(wd[:, :, d]) @ w_up[d]).astype(jnp.float32)
        decay = jnp.exp(-jnp.exp(-jax.nn.softplus(-w_log) - 0.5))
        a = jax.nn.sigmoid(a0[d] + ad[:, :, d] @ a_up[d])
        kd = heads(k * (1.0 + (a - 1.0) * k_a))
        s_fin, od = rwkv7_scan(rh, heads(decay), kd, vh, kk, heads(a), s0[:, d].astype(jnp.float32), d == 1)
        outs.append(od)
        bonuses.append(jnp.sum(rh * kd * r_k.astype(jnp.float32), axis=-1, keepdims=True) * vh)
        finals.append(s_fin)
    osum = outs[0] + outs[1]
    mean = jnp.mean(osum, axis=-1, keepdims=True)
    var = jnp.mean(jnp.square(osum - mean), axis=-1, keepdims=True)
    on = ((osum - mean) * lax.rsqrt(var + GN_EPS)).reshape(B, T, RWKV_DIM) * ln_g + ln_b
    y = (on + (bonuses[0] + bonuses[1]).reshape(B, T, RWKV_DIM)) * (jax.nn.sigmoid(gd) @ g_up)
    return y.astype(u.dtype), jnp.stack(finals, axis=1).astype(u.dtype)


def even_mixer(h, cache, rope, prm):
    (w_in, w_out, q_g, k_g, mu, w0, w_up, a0, a_up, g_up, k_k, k_a, r_k, ln_g, ln_b) = prm
    B, T, _ = h.shape
    proj = h @ w_in
    q = rms_norm(proj[..., :ATT_Q].reshape(B, T, ATT_HEADS, HEAD_DIM), q_g)
    k = rms_norm(proj[..., ATT_Q:ATT_Q + ATT_KV].reshape(B, T, ATT_KV_HEADS, HEAD_DIM), k_g)
    v = proj[..., ATT_Q + ATT_KV:ATT_Q + 2 * ATT_KV].reshape(B, T, ATT_KV_HEADS, HEAD_DIM)
    u = centred_shift_mix(proj[..., ATT_Q + 2 * ATT_KV:], mu)
    if cache is None:
        att = blocked_attention(q, k, v)
        s0 = jnp.zeros((B, 2, RWKV_HEADS, RWKV_HD, RWKV_HD), jnp.float32)
    else:
        ctx_k, ctx_v, s0 = cache
        cos, sin = rope
        keys = jnp.concatenate([ctx_k.astype(k.dtype), apply_rope(k, cos, sin)], axis=1)
        vals = jnp.concatenate([ctx_v.astype(v.dtype), v], axis=1)
        att = blocked_attention(apply_rope(q, cos, sin), keys, vals)
    rw, s_fin = rwkv7_mix(u, s0, w0, w_up, a0, a_up, g_up, k_k, k_a, r_k, ln_g, ln_b)
    y = jnp.concatenate([att, rw], axis=-1) @ w_out
    return y, (k, v, s_fin)


def centred_dwconv(u, w, b):
    T = u.shape[1]
    left = CONV_W // 2
    up = jnp.pad(u, ((0, 0), (left, CONV_W - 1 - left), (0, 0)))
    out = b
    for j in range(CONV_W):
        out = out + up[:, j:j + T] * w[j]
    return out


def linear_scan(a, b, h0, reverse):
    idx = -1 if reverse else 0
    b = b.at[:, idx].add(a[:, idx] * h0)

    def combine(lhs, rhs):
        return lhs[0] * rhs[0], rhs[0] * lhs[1] + rhs[1]

    _, h = lax.associative_scan(combine, (a, b), reverse=reverse, axis=1)
    return h


def odd_mixer(h, h0, prm):
    (w_in, w_out, conv_w, conv_b, wa, ba, wx, bx, lam) = prm
    B, T, _ = h.shape
    proj = h @ w_in
    gate = jax.nn.gelu(proj[..., :D_RNN])
    u = centred_dwconv(proj[..., D_RNN:], conv_w, conv_b)
    if h0 is None:
        h0 = jnp.zeros((B, 2, D_RNN), jnp.float32)
    ub = u.reshape(B, T, LRU_BLOCKS, LRU_BS)
    uf = u.astype(jnp.float32)
    outs, finals = [], []
    for d in range(2):
        ga = jax.nn.sigmoid((jnp.einsum('btnc,ncd->btnd', ub, wa[d]).reshape(B, T, D_RNN) + ba[d]).astype(jnp.float32))
        gx = jax.nn.sigmoid((jnp.einsum('btnc,ncd->btnd', ub, wx[d]).reshape(B, T, D_RNN) + bx[d]).astype(jnp.float32))
        log_a = -LRU_C * ga * jax.nn.softplus(-lam[d].astype(jnp.float32))
        bt = jnp.sqrt(-jnp.expm1(2.0 * log_a)) * gx * uf
        hs = linear_scan(jnp.exp(log_a), bt, h0[:, d].astype(jnp.float32), d == 1)
        outs.append(hs)
        finals.append(hs[:, 0] if d == 1 else hs[:, -1])
    y = (gate * (outs[0] + outs[1]).astype(h.dtype)) @ w_out
    return y, jnp.stack(finals, axis=1).astype(h.dtype)


def moe_ffn(x, rw, rb, w1, b1, w2, b2):
    shp = x.shape
    xt = x.reshape(-1, shp[-1])
    n_asg = xt.shape[0] * TOP_K
    logits = xt.astype(jnp.float32) @ rw.astype(jnp.float32) + rb.astype(jnp.float32)
    top_val, top_idx = lax.top_k(logits, TOP_K)
    gate = jax.nn.softmax(top_val, axis=-1).reshape(-1)
    expert = top_idx.reshape(-1)
    order = jnp.argsort(expert)
    exp_sorted = expert[order]
    tok_sorted = order // TOP_K
    count = jnp.bincount(expert, length=N_EXPERTS)
    padded = (count + MOE_BLOCK - 1) // MOE_BLOCK * MOE_BLOCK
    start = jnp.cumsum(count) - count
    pend = jnp.cumsum(padded)
    dest = (pend - padded)[exp_sorted] + jnp.arange(n_asg) - start[exp_sorted]
    n_blk = -(-n_asg // MOE_BLOCK) + N_EXPERTS
    rows = jnp.zeros((n_blk * MOE_BLOCK, shp[-1]), x.dtype).at[dest].set(xt[tok_sorted])
    blk_expert = jnp.minimum(jnp.searchsorted(pend, jnp.arange(n_blk) * MOE_BLOCK, side='right'), N_EXPERTS - 1)

    def expert_block(args):
        xb, e = args
        hb = xb @ w1[e] + b1[e]
        glu = jnp.minimum(hb[:, :D_EXPERT], SWIGLU_LIMIT)
        lin = jnp.clip(hb[:, D_EXPERT:], -SWIGLU_LIMIT, SWIGLU_LIMIT)
        return (glu * jax.nn.sigmoid(SWIGLU_ALPHA * glu) * (lin + 1.0)) @ w2[e] + b2[e]

    out = lax.map(expert_block, (rows.reshape(n_blk, MOE_BLOCK, shp[-1]), blk_expert))
    y = out.reshape(-1, shp[-1])[dest] * gate[order][:, None].astype(x.dtype)
    return jnp.zeros_like(xt).at[tok_sorted].add(y).reshape(shp)


def setup_inputs(seed: int = 0) -> dict:
    key = jax.random.key(seed)
    keys = iter(jax.random.split(key, 64))
    f32 = jnp.float32
    d = D_MODEL
    n_even = (DEPTH + 1) // 2
    n_odd = DEPTH // 2

    def nrm(shape, scale):
        return jax.random.normal(next(keys), shape, f32) * scale

    u_lam = jax.random.uniform(next(keys), (n_odd, 2, D_RNN), f32, minval=0.9, maxval=0.999)
    s_lam = u_lam ** (1.0 / LRU_C)
    inp = {}
    inp['x_prompt'] = nrm((BATCH, SEQ, d), 1.0)
    inp['x_sample'] = nrm((DEC_BATCH, DEC_SEQ, d), 1.0)
    inp['cache_attn_k'] = nrm((DEC_BATCH, n_even, PAST_LEN, ATT_KV_HEADS, HEAD_DIM), 1.0)
    inp['cache_attn_v'] = nrm((DEC_BATCH, n_even, PAST_LEN, ATT_KV_HEADS, HEAD_DIM), 1.0)
    inp['state_rwkv'] = nrm((DEC_BATCH, n_even, 2, RWKV_HEADS, RWKV_HD, RWKV_HD), 0.5)
    inp['state_lru'] = nrm((DEC_BATCH, n_odd, 2, D_RNN), 0.5)
    inp['c'] = nrm((DEC_BATCH, d), 1.0)
    inp['c_ctx'] = nrm((d,), 1.0)
    inp['mod_w'] = nrm((DEPTH, d, 6 * d), 0.5 * d ** -0.5)
    inp['mod_b'] = nrm((DEPTH, 6 * d), 0.02)
    inp['norm1_g'] = 1.0 + nrm((DEPTH, d), 0.02)
    inp['norm2_g'] = 1.0 + nrm((DEPTH, d), 0.02)
    inp['ev_w_in'] = nrm((n_even, d, EVEN_IN), d ** -0.5)
    inp['ev_w_out'] = nrm((n_even, EVEN_MIX, d), EVEN_MIX ** -0.5)
    inp['q_norm_g'] = 1.0 + nrm((n_even, HEAD_DIM), 0.02)
    inp['k_norm_g'] = 1.0 + nrm((n_even, HEAD_DIM), 0.02)
    inp['rwkv_mu'] = 0.5 + nrm((n_even, RWKV_COLS), 0.1)
    inp['rwkv_w0'] = 0.5 + nrm((n_even, 2, RWKV_DIM), 0.5)
    inp['rwkv_w_up'] = nrm((n_even, 2, DECAY_RANK, RWKV_DIM), 0.5 * DECAY_RANK ** -0.5)
    inp['rwkv_a0'] = nrm((n_even, 2, RWKV_DIM), 0.3)
    inp['rwkv_a_up'] = nrm((n_even, 2, ICLR_RANK, RWKV_DIM), 0.5 * ICLR_RANK ** -0.5)
    inp['rwkv_g_up'] = nrm((n_even, GATE_RANK, RWKV_DIM), GATE_RANK ** -0.5)
    inp['rwkv_k_k'] = 0.85 + nrm((n_even, RWKV_DIM), 0.05)
    inp['rwkv_k_a'] = 1.0 + nrm((n_even, RWKV_DIM), 0.05)
    inp['rwkv_r_k'] = nrm((n_even, RWKV_HEADS, RWKV_HD), 0.1)
    inp['rwkv_ln_g'] = 1.0 + nrm((n_even, RWKV_DIM), 0.02)
    inp['rwkv_ln_b'] = nrm((n_even, RWKV_DIM), 0.02)
    inp['od_w_in'] = nrm((n_odd, d, 2 * D_RNN), d ** -0.5)
    inp['od_w_out'] = nrm((n_odd, D_RNN, d), D_RNN ** -0.5)
    inp['conv_w'] = nrm((n_odd, CONV_W, D_RNN), CONV_W ** -0.5)
    inp['conv_b'] = nrm((n_odd, D_RNN), 0.02)
    inp['lru_wa'] = nrm((n_odd, 2, LRU_BLOCKS, LRU_BS, LRU_BS), LRU_BS ** -0.5)
    inp['lru_ba'] = nrm((n_odd, 2, D_RNN), 0.02)
    inp['lru_wx'] = nrm((n_odd, 2, LRU_BLOCKS, LRU_BS, LRU_BS), LRU_BS ** -0.5)
    inp['lru_bx'] = nrm((n_odd, 2, D_RNN), 0.02)
    inp['lru_lambda'] = jnp.log(s_lam) - jnp.log1p(-s_lam)
    inp['router_w'] = nrm((DEPTH, d, N_EXPERTS), d ** -0.5)
    inp['router_b'] = nrm((DEPTH, N_EXPERTS), 0.01)
    inp['exp_w1'] = nrm((DEPTH, N_EXPERTS, d, 2 * D_EXPERT), d ** -0.5)
    inp['exp_b1'] = nrm((DEPTH, N_EXPERTS, 2 * D_EXPERT), 0.02)
    inp['exp_w2'] = nrm((DEPTH, N_EXPERTS, D_EXPERT, d), D_EXPERT ** -0.5)
    inp['exp_b2'] = nrm((DEPTH, N_EXPERTS, d), 0.02)
    return inp


def reference(x_prompt, x_sample, cache_attn_k, cache_attn_v, state_rwkv, state_lru, c, c_ctx,
              mod_w, mod_b, norm1_g, norm2_g, ev_w_in, ev_w_out, q_norm_g, k_norm_g,
              rwkv_mu, rwkv_w0, rwkv_w_up, rwkv_a0, rwkv_a_up, rwkv_g_up, rwkv_k_k, rwkv_k_a,
              rwkv_r_k, rwkv_ln_g, rwkv_ln_b, od_w_in, od_w_out, conv_w, conv_b,
              lru_wa, lru_ba, lru_wx, lru_bx, lru_lambda,
              router_w, router_b, exp_w1, exp_b1, exp_w2, exp_b2):
    cos, sin = axial_rope(x_sample.shape[1])
    xp, xs = x_prompt, x_sample
    new_k, new_v, new_rw, new_lru = [], [], [], []
    for l in range(DEPTH):
        j = l // 2
        sp1, cp1, gp1, sp2, cp2, gp2 = jnp.split(jax.nn.silu(c_ctx) @ mod_w[l] + mod_b[l], 6, axis=-1)
        ms = jnp.split(jax.nn.silu(c) @ mod_w[l] + mod_b[l], 6, axis=-1)
        ss1, cs1, gs1, ss2, cs2, gs2 = [m[:, None, :] for m in ms]
        hp = modulate(rms_norm(xp, norm1_g[l]), sp1, cp1)
        hs = modulate(rms_norm(xs, norm1_g[l]), ss1, cs1)
        if l % 2 == 0:
            prm = (ev_w_in[j], ev_w_out[j], q_norm_g[j], k_norm_g[j], rwkv_mu[j], rwkv_w0[j], rwkv_w_up[j],
                   rwkv_a0[j], rwkv_a_up[j], rwkv_g_up[j], rwkv_k_k[j], rwkv_k_a[j], rwkv_r_k[j],
                   rwkv_ln_g[j], rwkv_ln_b[j])
            yp, (kc, vc, rc) = even_mixer(hp, None, None, prm)
            ys, _ = even_mixer(hs, (cache_attn_k[:, j], cache_attn_v[:, j], state_rwkv[:, j]), (cos, sin), prm)
            new_k.append(kc)
            new_v.append(vc)
            new_rw.append(rc)
        else:
            prm = (od_w_in[j], od_w_out[j], conv_w[j], conv_b[j], lru_wa[j], lru_ba[j], lru_wx[j],
                   lru_bx[j], lru_lambda[j])
            yp, lc = odd_mixer(hp, None, prm)
            ys, _ = odd_mixer(hs, state_lru[:, j], prm)
            new_lru.append(lc)
        xp = xp + gp1 * yp
        xs = xs + gs1 * ys
        moe_prm = (router_w[l], router_b[l], exp_w1[l], exp_b1[l], exp_w2[l], exp_b2[l])
        xp = xp + gp2 * moe_ffn(modulate(rms_norm(xp, norm2_g[l]), sp2, cp2), *moe_prm)
        xs = xs + gs2 * moe_ffn(modulate(rms_norm(xs, norm2_g[l]), ss2, cs2), *moe_prm)
    new_attn_k = jnp.stack(new_k, axis=1)
    new_attn_v = jnp.stack(new_v, axis=1)
    new_state_rwkv = jnp.stack(new_rw, axis=1)
    new_state_lru = jnp.stack(new_lru, axis=1)
    return (xp, xs, new_attn_k, new_attn_v, new_state_rwkv, new_state_lru)
```

```python
import functools
from typing import NamedTuple

import numpy as np
import jax
import jax.numpy as jnp
from jax import lax
from jax.experimental import pallas as pl
from jax.experimental.pallas import tpu as pltpu

F32 = jnp.float32
BF16 = jnp.bfloat16
I32 = jnp.int32
HIGHEST = lax.Precision.HIGHEST

D_MODEL = 1024
EPS = 1e-6
GRID_W = 64
ATT_HEADS = 8
ATT_KV_HEADS = 2
HEAD_DIM = 64
GQA_GROUP = ATT_HEADS // ATT_KV_HEADS
ATT_Q = ATT_HEADS * HEAD_DIM
ATT_KV = ATT_KV_HEADS * HEAD_DIM
ROPE_THETA = 10000.0
RWKV_HEADS = 8
RWKV_HD = 64
RWKV_DIM = RWKV_HEADS * RWKV_HD
DECAY_RANK = 64
ICLR_RANK = 64
GATE_RANK = 128
RWKV_COLS = 3 * RWKV_DIM + 2 * DECAY_RANK + 2 * ICLR_RANK + GATE_RANK
GN_EPS = 64e-5
D_RNN = D_MODEL
LRU_BS = 64
LRU_GROUP = 256
CONV_W = 4
LRU_C = 8.0
N_EXPERTS = 32
TOP_K = 4
D_EXPERT = D_MODEL
SWIGLU_LIMIT = 7.0
SWIGLU_ALPHA = 1.702

TOKEN_TILE = 256
RWKV_CHUNK = 64
INV_BLOCK = 16
ATT_Q_TILE = 128
MOE_TILE = 256
HALO = 8
MOD_ROWS = 16
VMEM_LIMIT = 56 * 1024 * 1024


class Layout(NamedTuple):
    bp: int
    tp: int
    bs: int
    ts: int

    @property
    def n_p(self):
        return self.bp * self.tp

    @property
    def n(self):
        return self.bp * self.tp + self.bs * self.ts

    def tiles(self, tile):
        return self.n // tile

    def seq_of_tile(self, i, tile):
        npt = self.n_p // tile
        is_p = i < npt
        ii = jnp.where(is_p, i, i - npt)
        per = jnp.where(is_p, self.tp // tile, self.ts // tile)
        return is_p, ii // per, ii % per, per

    def mod_row(self, i, tile):
        is_p, seq, _, _ = self.seq_of_tile(i, tile)
        return jnp.where(is_p, self.bs, seq)


def _cparams(sem):
    return pltpu.CompilerParams(dimension_semantics=sem, vmem_limit_bytes=VMEM_LIMIT)


def _dot(a, b):
    return jnp.dot(a, b, preferred_element_type=F32)


def _dot_nt(a, b):
    return lax.dot_general(a, b, (((1,), (1,)), ((), ())), preferred_element_type=F32)


def _dot_tn(a, b):
    return lax.dot_general(a, b, (((0,), (0,)), ((), ())), preferred_element_type=F32)


def _split_dot(x, m01):
    hi = x.astype(BF16)
    lo = (x - hi.astype(F32)).astype(BF16)
    return _dot(hi, m01) + _dot(lo, m01)


def _split_dot_left(m01, x):
    hi = x.astype(BF16)
    lo = (x - hi.astype(F32)).astype(BF16)
    return _dot(m01, hi) + _dot(m01, lo)


def _sigmoid(x):
    return 1.0 / (1.0 + jnp.exp(-x))


def _block_ones(n, blk):
    idx = np.arange(n) // blk
    return jnp.asarray((idx[:, None] == idx[None, :]).astype(np.float32), dtype=BF16)


def _mod_kernel(c_ref, w_ref, b_ref, o_ref):
    c = c_ref[...]
    s = c * _sigmoid(c)
    o_ref[0] = jnp.dot(s, w_ref[0], preferred_element_type=F32, precision=HIGHEST) + b_ref[0]


def modulation(cvec, mod_w, mod_b):
    depth, d, six_d = mod_w.shape
    nchunk = six_d // d
    out = pl.pallas_call(
        _mod_kernel,
        grid=(depth, nchunk),
        in_specs=[
            pl.BlockSpec((MOD_ROWS, d), lambda l, k: (0, 0)),
            pl.BlockSpec((1, d, d), lambda l, k: (l, 0, k)),
            pl.BlockSpec((1, 1, d), lambda l, k: (l, 0, k)),
        ],
        out_specs=pl.BlockSpec((1, MOD_ROWS, d), lambda l, k: (l, 0, k)),
        out_shape=jax.ShapeDtypeStruct((depth, MOD_ROWS, six_d), F32),
        compiler_params=_cparams(("arbitrary", "arbitrary")),
        name="modulation",
    )(cvec, mod_w, mod_b.reshape(depth, 1, six_d))
    return out.reshape(depth, MOD_ROWS * nchunk, 1, d)


def _mod_spec(lay, k, tile):
    return pl.BlockSpec((1, 1, D_MODEL), lambda i, *_: (lay.mod_row(i, tile) * 6 + k, 0, 0))


def _norm_mod(x, g, shift, scale):
    ms = jnp.mean(x * x, axis=-1, keepdims=True)
    h = x * lax.rsqrt(ms + EPS) * g
    return h * (1.0 + scale) + shift


def _norm_proj_kernel(x_ref, g_ref, sh_ref, sc_ref, w_ref, *o_refs, splits):
    h = _norm_mod(x_ref[...], g_ref[...], sh_ref[0], sc_ref[0])
    y = _dot(h.astype(BF16), w_ref[...])
    off = 0
    for o_ref, n in zip(o_refs, splits):
        o_ref[...] = y[:, off:off + n]
        off += n


def norm_proj(lay, x, g, mods, k_shift, w_bf16, splits):
    n, d = x.shape
    n_out = w_bf16.shape[1]
    tile = TOKEN_TILE
    return pl.pallas_call(
        functools.partial(_norm_proj_kernel, splits=splits),
        grid=(n // tile,),
        in_specs=[
            pl.BlockSpec((tile, d), lambda i: (i, 0)),
            pl.BlockSpec((1, d), lambda i: (0, 0)),
            _mod_spec(lay, k_shift, tile),
            _mod_spec(lay, k_shift + 1, tile),
            pl.BlockSpec((d, n_out), lambda i: (0, 0)),
        ],
        out_specs=[pl.BlockSpec((tile, s), lambda i: (i, 0)) for s in splits],
        out_shape=[jax.ShapeDtypeStruct((n, s), F32) for s in splits],
        compiler_params=_cparams(("arbitrary",)),
        name="norm_proj",
    )(x, g.reshape(1, d), mods, mods, w_bf16)


def _head_norm(x, g, ones):
    ms = _split_dot(x * x, ones) * (1.0 / HEAD_DIM)
    return x * lax.rsqrt(ms + EPS) * g


def _rope(x, cos, sin_signed):
    n = x.shape[1]
    nxt = pltpu.roll(x, n - 1, 1)
    prv = pltpu.roll(x, 1, 1)
    lane = lax.broadcasted_iota(I32, x.shape, 1)
    swapped = jnp.where(lane % 2 == 0, nxt, prv)
    return x * cos + swapped * sin_signed


def _attn_kernel(*refs, t_len, n_ctx, rotary):
    if rotary:
        (qkv_ref, ck_ref, cv_ref, cos_ref, sin_ref, qg_ref, kg_ref, oq_ref, ok_ref,
         att_ref, k_scr, v_scr) = refs
    else:
        qkv_ref, qg_ref, kg_ref, oq_ref, ok_ref, att_ref, kn_ref, k_scr, v_scr = refs
    qi = pl.program_id(1)
    tq = ATT_Q_TILE

    @pl.when(qi == 0)
    def _prepare_keys():
        k = _head_norm(qkv_ref[:, ATT_Q:ATT_Q + ATT_KV], kg_ref[...], ok_ref[...])
        v = qkv_ref[:, ATT_Q + ATT_KV:ATT_Q + 2 * ATT_KV]
        if rotary:
            k = _rope(k, cos_ref[:, :ATT_KV], sin_ref[:, :ATT_KV])
            k_scr[0:n_ctx, :] = ck_ref[0].astype(BF16)
            v_scr[0:n_ctx, :] = cv_ref[0].astype(BF16)
        else:
            kn_ref[...] = k
        k_scr[n_ctx:n_ctx + t_len, :] = k.astype(BF16)
        v_scr[n_ctx:n_ctx + t_len, :] = v.astype(BF16)

    row0 = pl.multiple_of(qi * tq, tq)
    q = _head_norm(qkv_ref[pl.ds(row0, tq), 0:ATT_Q], qg_ref[...], oq_ref[...])
    if rotary:
        q = _rope(q, cos_ref[pl.ds(row0, tq), :], sin_ref[pl.ds(row0, tq), :])
    q = (q * HEAD_DIM ** -0.5).astype(BF16)
    outs = []
    for j in range(ATT_KV_HEADS):
        kj = k_scr[:, j * HEAD_DIM:(j + 1) * HEAD_DIM]
        vj = v_scr[:, j * HEAD_DIM:(j + 1) * HEAD_DIM]
        qs = jnp.concatenate(
            [q[:, (j * GQA_GROUP + g) * HEAD_DIM:(j * GQA_GROUP + g + 1) * HEAD_DIM] for g in range(GQA_GROUP)],
            axis=0)
        s = _dot_nt(qs, kj)
        p = jnp.exp(s - jnp.max(s, axis=-1, keepdims=True))
        p = p / jnp.sum(p, axis=-1, keepdims=True)
        o = _dot(p.astype(BF16), vj)
        outs.extend(o[g * tq:(g + 1) * tq] for g in range(GQA_GROUP))
    att_ref[...] = jnp.concatenate(outs, axis=1)


def attention(qkv, seq0, n_seq, t_len, q_g, k_g, cache=None, rope=None):
    rotary = cache is not None
    n_ctx = cache[0].shape[1] if rotary else 0
    blk0 = seq0 // t_len
    n_q = t_len // ATT_Q_TILE
    width = qkv.shape[1]
    qg = jnp.tile(q_g, ATT_HEADS).reshape(1, ATT_Q)
    kg = jnp.tile(k_g, ATT_KV_HEADS).reshape(1, ATT_KV)
    const = lambda shape: pl.BlockSpec(shape, lambda b, qi: (0,) * len(shape))
    in_specs = [pl.BlockSpec((t_len, width), lambda b, qi: (blk0 + b, 0))]
    args = [qkv]
    if rotary:
        in_specs += [pl.BlockSpec((1, n_ctx, ATT_KV), lambda b, qi: (b, 0, 0))] * 2
        in_specs += [const((t_len, ATT_Q))] * 2
        args += [cache[0], cache[1], rope[0], rope[1]]
    in_specs += [const((1, ATT_Q)), const((1, ATT_KV)), const((ATT_Q, ATT_Q)), const((ATT_KV, ATT_KV))]
    args += [qg, kg, _block_ones(ATT_Q, HEAD_DIM), _block_ones(ATT_KV, HEAD_DIM)]
    out_specs = [pl.BlockSpec((ATT_Q_TILE, ATT_Q), lambda b, qi: (b * n_q + qi, 0))]
    out_shape = [jax.ShapeDtypeStruct((n_seq * t_len, ATT_Q), F32)]
    if not rotary:
        out_specs.append(pl.BlockSpec((t_len, ATT_KV), lambda b, qi: (b, 0)))
        out_shape.append(jax.ShapeDtypeStruct((n_seq * t_len, ATT_KV), F32))
    return pl.pallas_call(
        functools.partial(_attn_kernel, t_len=t_len, n_ctx=n_ctx, rotary=rotary),
        grid=(n_seq, n_q),
        in_specs=in_specs,
        out_specs=out_specs,
        out_shape=out_shape,
        scratch_shapes=[pltpu.VMEM((n_ctx + t_len, ATT_KV), BF16), pltpu.VMEM((n_ctx + t_len, ATT_KV), BF16)],
        compiler_params=_cparams(("arbitrary", "arbitrary")),
        name="attention_latent" if rotary else "attention_context",
    )(*args)


def rope_tables(t_len):
    t = jnp.arange(t_len)
    pos = jnp.stack([t // GRID_W, t % GRID_W], axis=-1).astype(F32)
    n_freq = HEAD_DIM // 4
    inv = ROPE_THETA ** (-jnp.arange(n_freq, dtype=F32) / n_freq)
    ang = (pos[:, :, None] * inv).reshape(t_len, 2 * n_freq)
    cos = jnp.repeat(jnp.cos(ang), 2, axis=1)
    sin = jnp.repeat(jnp.sin(ang), 2, axis=1) * jnp.tile(jnp.asarray([-1.0, 1.0], F32), HEAD_DIM // 2)
    return jnp.tile(cos, (1, ATT_HEADS)), jnp.tile(sin, (1, ATT_HEADS))


def _shifted_rows(x, prev_row, next_row):
    m = x.shape[0]
    row = lax.broadcasted_iota(I32, x.shape, 0)
    prv = jnp.where(row == 0, prev_row, pltpu.roll(x, 1, 0))
    nxt = jnp.where(row == m - 1, next_row, pltpu.roll(x, m - 1, 0))
    return prv, nxt


def _rwkv_prep_kernel(x_ref, prev_ref, next_ref, mu_ref, kk_ref, ka_ref, rk_ref, w0_ref, wup_ref, a0_ref,
                      aup_ref, gup_ref, ones_ref,
                      r_ref, v_ref, al_ref, lw0_ref, be0_ref, kd0_ref, lw1_ref, be1_ref, kd1_ref, bonus_ref,
                      gate_ref, *, lay):
    i = pl.program_id(0)
    _, _, j, per = lay.seq_of_tile(i, TOKEN_TILE)
    x = x_ref[...]
    prev_row = jnp.where(j == 0, 0.0, prev_ref[HALO - 1:HALO, :])
    next_row = jnp.where(j == per - 1, 0.0, next_ref[0:1, :])
    prv, nxt = _shifted_rows(x, prev_row, next_row)
    u = x + mu_ref[...] * (0.5 * (prv + nxt) - x)

    dim = RWKV_DIM
    r, k, v = u[:, :dim], u[:, dim:2 * dim], u[:, 2 * dim:3 * dim]
    o = 3 * dim
    wd = u[:, o:o + 2 * DECAY_RANK]
    o += 2 * DECAY_RANK
    ad = u[:, o:o + 2 * ICLR_RANK]
    o += 2 * ICLR_RANK
    gd = u[:, o:o + GATE_RANK]

    ones = ones_ref[...]
    kk = k * kk_ref[...]
    norm = jnp.sqrt(_split_dot(kk * kk, ones))
    alpha = kk / jnp.maximum(norm, 1e-12)
    r_ref[...] = r
    v_ref[...] = v
    al_ref[...] = alpha

    tanh_wd = jnp.tanh(wd).astype(BF16)
    ad16 = ad.astype(BF16)
    kd_sum = None
    for d, (lw_ref, be_ref, kd_ref) in enumerate(((lw0_ref, be0_ref, kd0_ref), (lw1_ref, be1_ref, kd1_ref))):
        w_log = w0_ref[d:d + 1, :] + _dot(tanh_wd[:, d * DECAY_RANK:(d + 1) * DECAY_RANK], wup_ref[d])
        lw_ref[...] = -_sigmoid(w_log) * float(np.exp(-0.5))
        a = _sigmoid(a0_ref[d:d + 1, :] + _dot(ad16[:, d * ICLR_RANK:(d + 1) * ICLR_RANK], aup_ref[d]))
        kd = k * (1.0 + (a - 1.0) * ka_ref[...])
        be_ref[...] = alpha * a
        kd_ref[...] = kd
        kd_sum = kd if kd_sum is None else kd_sum + kd
    bonus_ref[...] = _split_dot(r * kd_sum * rk_ref[...], ones) * v
    gate_ref[...] = _dot(_sigmoid(gd).astype(BF16), gup_ref[...])


def rwkv_prep(lay, rw, prm):
    mu, w0, w_up, a0, a_up, g_up, k_k, k_a, r_k = prm
    n, cols = rw.shape
    tile = TOKEN_TILE
    hb = tile // HALO
    n_halo = n // HALO
    dim = RWKV_DIM
    const = lambda shape: pl.BlockSpec(shape, lambda i: (0,) * len(shape))
    row = lambda a: a.reshape(1, -1)
    out_spec = pl.BlockSpec((tile, dim), lambda i: (i, 0))
    return pl.pallas_call(
        functools.partial(_rwkv_prep_kernel, lay=lay),
        grid=(n // tile,),
        in_specs=[
            pl.BlockSpec((tile, cols), lambda i: (i, 0)),
            pl.BlockSpec((HALO, cols), lambda i: (jnp.maximum(i * hb - 1, 0), 0)),
            pl.BlockSpec((HALO, cols), lambda i: (jnp.minimum((i + 1) * hb, n_halo - 1), 0)),
            const((1, cols)), const((1, dim)), const((1, dim)), const((1, dim)),
            const((2, dim)), const((2, DECAY_RANK, dim)), const((2, dim)), const((2, ICLR_RANK, dim)),
            const((GATE_RANK, dim)), const((dim, dim)),
        ],
        out_specs=[out_spec] * 11,
        out_shape=[jax.ShapeDtypeStruct((n, dim), F32)] * 11,
        compiler_params=_cparams(("arbitrary",)),
        name="rwkv_prep",
    )(rw, rw, rw, row(mu), row(k_k), row(k_a), row(r_k), w0, w_up.astype(BF16), a0, a_up.astype(BF16),
      g_up.astype(BF16), _block_ones(dim, RWKV_HD))


def _unit_triangular_inverse(l_mat, eye, diag_blocks):
    b16 = lambda m: m.astype(BF16)
    ld = jnp.where(diag_blocks, l_mat, 0.0)
    lo = l_mat - ld
    x = eye - ld
    p = ld
    n_sq = int(np.log2(INV_BLOCK)) - 1
    for _ in range(n_sq):
        p = _dot(b16(p), b16(p))
        x = x + _dot(b16(x), b16(p))
    nb = _dot(b16(x), b16(lo))
    y = eye - nb
    p = nb
    n_blk_sq = int(np.log2(RWKV_CHUNK // INV_BLOCK)) - 1
    for _ in range(n_blk_sq):
        p = _dot(b16(p), b16(p))
        y = y + _dot(b16(y), b16(p))
    return _dot(b16(y), b16(x))


def _rwkv_chunk_dir(r, v, alpha, lw, beta, kd, h_ref, d, reverse, masks):
    incl01, strict, incl, eye, diag_blocks = masks[1 if reverse else 0]
    c_incl = _split_dot_left(incl01, lw)
    c_excl = c_incl - lw
    c_tot = jnp.sum(lw, axis=0, keepdims=True)
    a_bar = alpha * jnp.exp(c_excl)
    r_bar = r * jnp.exp(c_incl)
    inv_p = jnp.exp(-c_incl)
    b_bar = beta * inv_p
    k_bar = kd * inv_p
    to_end = jnp.exp(c_tot - c_incl)
    b_til = beta * to_end
    k_til = kd * to_end
    p_tot = jnp.exp(c_tot)

    b16 = lambda m: m.astype(BF16)
    c = RWKV_CHUNK
    outs = []
    for h in range(RWKV_HEADS):
        sl = slice(h * RWKV_HD, (h + 1) * RWKV_HD)
        vh = b16(v[:, sl])
        gram = _dot_nt(b16(jnp.concatenate([a_bar[:, sl], r_bar[:, sl]], axis=0)),
                       b16(jnp.concatenate([b_bar[:, sl], k_bar[:, sl]], axis=0)))
        l_mat = jnp.where(strict, gram[:c, :c], 0.0)
        a_k = jnp.where(strict, gram[:c, c:], 0.0)
        r_b = jnp.where(incl, gram[c:, :c], 0.0)
        r_k = jnp.where(incl, gram[c:, c:], 0.0)
        t_inv = _unit_triangular_inverse(l_mat, eye, diag_blocks)
        akv = _dot(b16(a_k), vh)
        mw = _dot(b16(t_inv), b16(jnp.concatenate([a_bar[:, sl], akv], axis=1)))
        mw16 = b16(mw)
        bt_mw = _dot_tn(b16(b_til[:, sl]), mw16)
        rb_mw = _dot(b16(r_b), mw16)
        d_mat = _dot_tn(b16(k_til[:, sl]), vh) - bt_mw[:, RWKV_HD:]
        q_eff = r_bar[:, sl] - rb_mw[:, :RWKV_HD]
        o_intra = _dot(b16(r_k), vh) - rb_mw[:, RWKV_HD:]
        h0 = h_ref[d * RWKV_HEADS + h]
        h016 = b16(h0)
        outs.append(_dot(b16(q_eff), h016) + o_intra)
        decay_col = jnp.sum(eye * p_tot[:, sl], axis=1, keepdims=True)
        h_ref[d * RWKV_HEADS + h] = decay_col * h0 - _dot(b16(bt_mw[:, :RWKV_HD]), h016) + d_mat
    return jnp.concatenate(outs, axis=1)


def _rwkv_chunk_kernel(rf_ref, vf_ref, af_ref, lwf_ref, bef_ref, kdf_ref,
                       rb_ref, vb_ref, ab_ref, lwb_ref, beb_ref, kdb_ref, s0_ref,
                       of_ref, ob_ref, sfin_ref, h_scr, *, lay):
    s = pl.program_id(0)
    _, _, c, per = lay.seq_of_tile(s, RWKV_CHUNK)

    @pl.when(c == 0)
    def _load_state():
        h_scr[...] = s0_ref[0]

    n = RWKV_CHUNK
    row = lax.broadcasted_iota(I32, (n, n), 0)
    col = lax.broadcasted_iota(I32, (n, n), 1)
    eye = (row == col).astype(F32)
    diag_blocks = (row // INV_BLOCK) == (col // INV_BLOCK)
    masks = []
    for reverse in (False, True):
        strict = (col > row) if reverse else (col < row)
        incl = (col >= row) if reverse else (col <= row)
        masks.append((jnp.where(incl, 1.0, 0.0).astype(BF16), strict, incl, eye, diag_blocks))

    of_ref[...] = _rwkv_chunk_dir(rf_ref[...], vf_ref[...], af_ref[...], lwf_ref[...], bef_ref[...], kdf_ref[...],
                                  h_scr, 0, False, masks)
    ob_ref[...] = _rwkv_chunk_dir(rb_ref[...], vb_ref[...], ab_ref[...], lwb_ref[...], beb_ref[...], kdb_ref[...],
                                  h_scr, 1, True, masks)

    @pl.when(c == per - 1)
    def _store_state():
        sfin_ref[0] = h_scr[...]


def rwkv_chunks(lay, r, v, alpha, lw0, be0, kd0, lw1, be1, kd1, s0):
    n, dim = r.shape
    ch = RWKV_CHUNK
    n_seq = lay.bp + lay.bs

    def fwd(s):
        return (s, 0)

    def bwd(s):
        _, _, c, per = lay.seq_of_tile(s, ch)
        return (s - c + (per - 1 - c), 0)

    def seq(s):
        is_p, q, _, _ = lay.seq_of_tile(s, ch)
        return (jnp.where(is_p, q, lay.bp + q), 0, 0, 0)

    state_block = (1, 2 * RWKV_HEADS, RWKV_HD, RWKV_HD)
    return pl.pallas_call(
        functools.partial(_rwkv_chunk_kernel, lay=lay),
        grid=(n // ch,),
        in_specs=[pl.BlockSpec((ch, dim), fwd)] * 6 + [pl.BlockSpec((ch, dim), bwd)] * 6
        + [pl.BlockSpec(state_block, seq)],
        out_specs=[pl.BlockSpec((ch, dim), fwd), pl.BlockSpec((ch, dim), bwd), pl.BlockSpec(state_block, seq)],
        out_shape=[jax.ShapeDtypeStruct((n, dim), F32), jax.ShapeDtypeStruct((n, dim), F32),
                   jax.ShapeDtypeStruct((n_seq,) + state_block[1:], F32)],
        scratch_shapes=[pltpu.VMEM(state_block[1:], F32)],
        compiler_params=_cparams(("arbitrary",)),
        name="rwkv_chunks",
    )(r, v, alpha, lw0, be0, kd0, r, v, alpha, lw1, be1, kd1, s0)


def _even_out_kernel(x_ref, att_ref, of_ref, ob_ref, bonus_ref, gate_ref, lng_ref, lnb_ref, ones_ref,
                     wa_ref, wr_ref, g1_ref, o_ref):
    ones = ones_ref[...]
    osum = of_ref[...] + ob_ref[...]
    mean = _split_dot(osum, ones) * (1.0 / RWKV_HD)
    cen = osum - mean
    var = _split_dot(cen * cen, ones) * (1.0 / RWKV_HD)
    on = cen * lax.rsqrt(var + GN_EPS) * lng_ref[...] + lnb_ref[...]
    rw = (on + bonus_ref[...]) * gate_ref[...]
    y = _dot(att_ref[...].astype(BF16), wa_ref[...]) + _dot(rw.astype(BF16), wr_ref[...])
    o_ref[...] = x_ref[...] + g1_ref[0] * y


def even_out(lay, x, att, o_f, o_b, bonus, gate, ln_g, ln_b, w_out, mods):
    n, d = x.shape
    tile = TOKEN_TILE
    dim = RWKV_DIM
    const = lambda shape: pl.BlockSpec(shape, lambda i: (0,) * len(shape))
    tok = lambda w: pl.BlockSpec((tile, w), lambda i: (i, 0))
    w16 = w_out.astype(BF16)
    return pl.pallas_call(
        _even_out_kernel,
        grid=(n // tile,),
        in_specs=[tok(d), tok(ATT_Q), tok(dim), tok(dim), tok(dim), tok(dim),
                  const((1, dim)), const((1, dim)), const((dim, dim)),
                  const((ATT_Q, d)), const((dim, d)), _mod_spec(lay, 2, tile)],
        out_specs=tok(d),
        out_shape=jax.ShapeDtypeStruct((n, d), F32),
        compiler_params=_cparams(("arbitrary",)),
        name="even_out",
    )(x, att, o_f, o_b, bonus, gate, ln_g.reshape(1, dim), ln_b.reshape(1, dim), _block_ones(dim, RWKV_HD),
      w16[:ATT_Q], w16[ATT_Q:], mods)


def _gelu_tanh(x):
    return 0.5 * x * (1.0 + jnp.tanh(float(np.sqrt(2.0 / np.pi)) * (x + 0.044715 * (x * x * x))))


def _softplus(x):
    return jnp.maximum(x, 0.0) + jnp.log(1.0 + jnp.exp(-jnp.abs(x)))


def _lru_prep_kernel(x_ref, prev_ref, next_ref, cw_ref, cb_ref, wbd_ref, ba_ref, bx_ref, lam_ref,
                     gate_ref, a0_ref, b0_ref, a1_ref, b1_ref, *, lay):
    i = pl.program_id(0)
    _, _, j, per = lay.seq_of_tile(i, TOKEN_TILE)
    tile = TOKEN_TILE
    gate_ref[...] = _gelu_tanh(x_ref[:, :D_RNN])
    x = x_ref[:, D_RNN:]
    first = j == 0
    last = j == per - 1
    row = lax.broadcasted_iota(I32, x.shape, 0)
    p1 = jnp.where(first, 0.0, prev_ref[HALO - 1:HALO, :])
    p2 = jnp.where(first, 0.0, prev_ref[HALO - 2:HALO - 1, :])
    n1 = jnp.where(last, 0.0, next_ref[0:1, :])
    xm1 = jnp.where(row == 0, p1, pltpu.roll(x, 1, 0))
    xm2 = jnp.where(row == 0, p2, jnp.where(row == 1, p1, pltpu.roll(x, 2, 0)))
    xp1 = jnp.where(row == tile - 1, n1, pltpu.roll(x, tile - 1, 0))
    u = cb_ref[...] + xm2 * cw_ref[0:1, :] + xm1 * cw_ref[1:2, :] + x * cw_ref[2:3, :] + xp1 * cw_ref[3:4, :]

    u16 = u.astype(BF16)
    n_grp = D_RNN // LRU_GROUP
    z = [_dot(u16[:, g * LRU_GROUP:(g + 1) * LRU_GROUP], wbd_ref[g]) for g in range(n_grp)]
    pick = lambda m: jnp.concatenate([zg[:, m * LRU_GROUP:(m + 1) * LRU_GROUP] for zg in z], axis=1)
    for d, (a_ref, b_ref) in enumerate(((a0_ref, b0_ref), (a1_ref, b1_ref))):
        ga = _sigmoid(pick(2 * d) + ba_ref[d:d + 1, :])
        gx = _sigmoid(pick(2 * d + 1) + bx_ref[d:d + 1, :])
        log_a = -LRU_C * ga * _softplus(-lam_ref[d:d + 1, :])
        a_ref[...] = jnp.exp(log_a)
        b_ref[...] = jnp.sqrt(1.0 - jnp.exp(2.0 * log_a)) * gx * u


def lru_prep(lay, proj, conv_w, conv_b, wa, ba, wx, bx, lam):
    n, cols = proj.shape
    tile = TOKEN_TILE
    hb = tile // HALO
    n_halo = n // HALO
    dr = D_RNN
    per = LRU_GROUP // LRU_BS
    n_grp = dr // LRU_GROUP

    def block_diag(w):
        w = w.reshape(n_grp, per, LRU_BS, LRU_BS)
        eye = jnp.eye(per, dtype=w.dtype)
        return jnp.einsum("gpcd,pq->gpcqd", w, eye).reshape(n_grp, LRU_GROUP, LRU_GROUP)

    wbd = jnp.concatenate([block_diag(wa[0]), block_diag(wx[0]), block_diag(wa[1]), block_diag(wx[1])],
                          axis=2).astype(BF16)
    const = lambda shape: pl.BlockSpec(shape, lambda i: (0,) * len(shape))
    out_spec = pl.BlockSpec((tile, dr), lambda i: (i, 0))
    return pl.pallas_call(
        functools.partial(_lru_prep_kernel, lay=lay),
        grid=(n // tile,),
        in_specs=[
            pl.BlockSpec((tile, cols), lambda i: (i, 0)),
            pl.BlockSpec((HALO, dr), lambda i: (jnp.maximum(i * hb - 1, 0), 1)),
            pl.BlockSpec((HALO, dr), lambda i: (jnp.minimum((i + 1) * hb, n_halo - 1), 1)),
            const((CONV_W, dr)), const((1, dr)), const((n_grp, LRU_GROUP, 4 * LRU_GROUP)),
            const((2, dr)), const((2, dr)), const((2, dr)),
        ],
        out_specs=[out_spec] * 5,
        out_shape=[jax.ShapeDtypeStruct((n, dr), F32)] * 5,
        compiler_params=_cparams(("arbitrary",)),
        name="lru_prep",
    )(proj, proj, proj, conv_w, conv_b.reshape(1, dr), wbd, ba, bx, lam)


def _lru_scan_kernel(af_ref, bf_ref, ab_ref, bb_ref, h0_ref, hf_ref, hb_ref, carry, *, lay):
    i = pl.program_id(0)
    _, _, j, _ = lay.seq_of_tile(i, TOKEN_TILE)
    tile = TOKEN_TILE

    @pl.when(j == 0)
    def _load_state():
        carry[...] = h0_ref[0]

    def step(t, hs):
        hf, hb = hs
        tb = tile - 1 - t
        hf = af_ref[pl.ds(t, 1), :] * hf + bf_ref[pl.ds(t, 1), :]
        hb = ab_ref[pl.ds(tb, 1), :] * hb + bb_ref[pl.ds(tb, 1), :]
        hf_ref[pl.ds(t, 1), :] = hf
        hb_ref[pl.ds(tb, 1), :] = hb
        return hf, hb

    hf, hb = lax.fori_loop(0, tile, step, (carry[0:1, :], carry[1:2, :]))
    carry[0:1, :] = hf
    carry[1:2, :] = hb


def lru_scan(lay, a0, b0, a1, b1, h0):
    n, dr = a0.shape
    tile = TOKEN_TILE

    def fwd(i):
        return (i, 0)

    def bwd(i):
        _, _, j, per = lay.seq_of_tile(i, tile)
        return (i - j + (per - 1 - j), 0)

    def seq(i):
        is_p, q, _, _ = lay.seq_of_tile(i, tile)
        return (jnp.where(is_p, q, lay.bp + q), 0, 0)

    return pl.pallas_call(
        functools.partial(_lru_scan_kernel, lay=lay),
        grid=(n // tile,),
        in_specs=[pl.BlockSpec((tile, dr), fwd)] * 2 + [pl.BlockSpec((tile, dr), bwd)] * 2
        + [pl.BlockSpec((1, 2, dr), seq)],
        out_specs=[pl.BlockSpec((tile, dr), fwd), pl.BlockSpec((tile, dr), bwd)],
        out_shape=[jax.ShapeDtypeStruct((n, dr), F32)] * 2,
        scratch_shapes=[pltpu.VMEM((2, dr), F32)],
        compiler_params=_cparams(("arbitrary",)),
        name="lru_scan",
    )(a0, b0, a1, b1, h0)


def _odd_out_kernel(x_ref, gate_ref, hf_ref, hb_ref, w_ref, g1_ref, o_ref):
    y = _dot((gate_ref[...] * (hf_ref[...] + hb_ref[...])).astype(BF16), w_ref[...])
    o_ref[...] = x_ref[...] + g1_ref[0] * y


def odd_out(lay, x, gate, hf, hb, w_out, mods):
    n, d = x.shape
    tile = TOKEN_TILE
    tok = lambda w: pl.BlockSpec((tile, w), lambda i: (i, 0))
    return pl.pallas_call(
        _odd_out_kernel,
        grid=(n // tile,),
        in_specs=[tok(d), tok(D_RNN), tok(D_RNN), tok(D_RNN), pl.BlockSpec((D_RNN, d), lambda i: (0, 0)),
                  _mod_spec(lay, 2, tile)],
        out_specs=tok(d),
        out_shape=jax.ShapeDtypeStruct((n, d), F32),
        compiler_params=_cparams(("arbitrary",)),
        name="odd_out",
    )(x, gate, hf, hb, w_out.astype(BF16), mods)


def _router_kernel(x_ref, g_ref, sh_ref, sc_ref, rw_ref, rb_ref, h_ref, idx_ref, gate_ref):
    h = _norm_mod(x_ref[...], g_ref[...], sh_ref[0], sc_ref[0])
    h_ref[...] = h
    logits = lax.dot_general(rw_ref[...], h, (((1,), (1,)), ((), ())), preferred_element_type=F32,
                             precision=HIGHEST) + rb_ref[...]
    e_id = lax.broadcasted_iota(I32, logits.shape, 0)
    vals, ids = [], []
    for _ in range(TOP_K):
        m = jnp.max(logits, axis=0, keepdims=True)
        pick = jnp.min(jnp.where(logits == m, e_id, N_EXPERTS), axis=0, keepdims=True)
        vals.append(m)
        ids.append(pick)
        logits = jnp.where(e_id == pick, -jnp.inf, logits)
    top = jnp.concatenate(vals, axis=0)
    p = jnp.exp(top - top[0:1, :])
    gate_ref[...] = p / jnp.sum(p, axis=0, keepdims=True)
    idx_ref[...] = jnp.concatenate(ids, axis=0)


def router(lay, x, g, mods, rw, rb):
    n, d = x.shape
    tile = TOKEN_TILE
    return pl.pallas_call(
        _router_kernel,
        grid=(n // tile,),
        in_specs=[pl.BlockSpec((tile, d), lambda i: (i, 0)), pl.BlockSpec((1, d), lambda i: (0, 0)),
                  _mod_spec(lay, 3, tile), _mod_spec(lay, 4, tile),
                  pl.BlockSpec((N_EXPERTS, d), lambda i: (0, 0)), pl.BlockSpec((N_EXPERTS, 1), lambda i: (0, 0))],
        out_specs=[pl.BlockSpec((tile, d), lambda i: (i, 0)), pl.BlockSpec((TOP_K, tile), lambda i: (0, i)),
                   pl.BlockSpec((TOP_K, tile), lambda i: (0, i))],
        out_shape=[jax.ShapeDtypeStruct((n, d), F32), jax.ShapeDtypeStruct((TOP_K, n), I32),
                   jax.ShapeDtypeStruct((TOP_K, n), F32)],
        compiler_params=_cparams(("arbitrary",)),
        name="router",
    )(x, g.reshape(1, d), mods, mods, rw.T, rb.reshape(N_EXPERTS, 1))


def _row_copy(src_hbm, dst_vmem, sem, src_row, dst_row):
    return pltpu.make_async_copy(src_hbm.at[pl.ds(src_row, 1)], dst_vmem.at[pl.ds(dst_row, 1)], sem)


def _gather_rows_kernel(idx_ref, src_hbm, o_ref, sem):
    base = pl.program_id(0) * MOE_TILE

    def start(r, _):
        _row_copy(src_hbm, o_ref, sem, idx_ref[base + r], r).start()
        return 0

    def wait(r, _):
        _row_copy(src_hbm, o_ref, sem, 0, r).wait()
        return 0

    lax.fori_loop(0, MOE_TILE, start, 0)
    lax.fori_loop(0, MOE_TILE, wait, 0)


def gather_rows(src, idx):
    n_out = idx.shape[0]
    d = src.shape[1]
    return pl.pallas_call(
        _gather_rows_kernel,
        grid_spec=pltpu.PrefetchScalarGridSpec(
            num_scalar_prefetch=1,
            grid=(n_out // MOE_TILE,),
            in_specs=[pl.BlockSpec(memory_space=pl.ANY)],
            out_specs=pl.BlockSpec((MOE_TILE, d), lambda i, idx: (i, 0)),
            scratch_shapes=[pltpu.SemaphoreType.DMA(())],
        ),
        out_shape=jax.ShapeDtypeStruct((n_out, d), src.dtype),
        compiler_params=_cparams(("arbitrary",)),
        name="moe_gather",
    )(idx, src)


def _expert_kernel(blk_e_ref, n_used_ref, x_ref, w1_ref, b1_ref, w2_ref, b2_ref, o_ref, w1_scr, w2_scr):
    i = pl.program_id(0)
    prev_e = blk_e_ref[jnp.maximum(i - 1, 0)]
    new_expert = jnp.logical_or(i == 0, blk_e_ref[i] != prev_e)
    used = i < n_used_ref[0]

    @pl.when(jnp.logical_and(used, new_expert))
    def _cast_weights():
        w1_scr[...] = w1_ref[0].astype(BF16)
        w2_scr[...] = w2_ref[0].astype(BF16)

    @pl.when(used)
    def _compute():
        hb = _dot(x_ref[...].astype(BF16), w1_scr[...]) + b1_ref[0]
        glu = jnp.minimum(hb[:, :D_EXPERT], SWIGLU_LIMIT)
        lin = jnp.clip(hb[:, D_EXPERT:], -SWIGLU_LIMIT, SWIGLU_LIMIT)
        act = glu * _sigmoid(SWIGLU_ALPHA * glu) * (lin + 1.0)
        o_ref[...] = _dot(act.astype(BF16), w2_scr[...]) + b2_ref[0]

    @pl.when(jnp.logical_not(used))
    def _clear():
        o_ref[...] = jnp.zeros_like(o_ref)


def experts(x_sorted, blk_expert, n_used, w1, b1, w2, b2):
    n_rows, d = x_sorted.shape
    de2 = w1.shape[2]
    de = w2.shape[1]
    wmap = lambda i, be, nu: (be[i], 0, 0)
    return pl.pallas_call(
        _expert_kernel,
        grid_spec=pltpu.PrefetchScalarGridSpec(
            num_scalar_prefetch=2,
            grid=(n_rows // MOE_TILE,),
            in_specs=[pl.BlockSpec((MOE_TILE, d), lambda i, be, nu: (i, 0)),
                      pl.BlockSpec((1, d, de2), wmap), pl.BlockSpec((1, 1, de2), wmap),
                      pl.BlockSpec((1, de, d), wmap), pl.BlockSpec((1, 1, d), wmap)],
            out_specs=pl.BlockSpec((MOE_TILE, d), lambda i, be, nu: (i, 0)),
            scratch_shapes=[pltpu.VMEM((d, de2), BF16), pltpu.VMEM((de, d), BF16)],
        ),
        out_shape=jax.ShapeDtypeStruct((n_rows, d), F32),
        compiler_params=_cparams(("arbitrary",)),
        name="moe_experts",
    )(blk_expert, n_used, x_sorted, w1, b1.reshape(N_EXPERTS, 1, de2), w2, b2.reshape(N_EXPERTS, 1, d))


def _combine_kernel(dest_ref, y_hbm, x_ref, gate_ref, g2_ref, o_ref, buf, sem):
    tile = TOKEN_TILE
    base = pl.program_id(0) * tile

    def start(r, _):
        for k in range(TOP_K):
            _row_copy(y_hbm, buf.at[k], sem, dest_ref[(base + r) * TOP_K + k], r).start()
        return 0

    def wait(r, _):
        for k in range(TOP_K):
            _row_copy(y_hbm, buf.at[k], sem, 0, r).wait()
        return 0

    lax.fori_loop(0, tile, start, 0)
    lax.fori_loop(0, tile, wait, 0)
    gate = gate_ref[...]
    acc = buf[0] * gate[:, 0:1]
    for k in range(1, TOP_K):
        acc = acc + buf[k] * gate[:, k:k + 1]
    o_ref[...] = x_ref[...] + g2_ref[0] * acc


def combine(lay, x, y_sorted, dest, gate, mods):
    n, d = x.shape
    tile = TOKEN_TILE
    return pl.pallas_call(
        _combine_kernel,
        grid_spec=pltpu.PrefetchScalarGridSpec(
            num_scalar_prefetch=1,
            grid=(n // tile,),
            in_specs=[pl.BlockSpec(memory_space=pl.ANY),
                      pl.BlockSpec((tile, d), lambda i, dest: (i, 0)),
                      pl.BlockSpec((tile, TOP_K), lambda i, dest: (i, 0)),
                      _mod_spec(lay, 5, tile)],
            out_specs=pl.BlockSpec((tile, d), lambda i, dest: (i, 0)),
            scratch_shapes=[pltpu.VMEM((TOP_K, tile, d), F32), pltpu.SemaphoreType.DMA(())],
        ),
        out_shape=jax.ShapeDtypeStruct((n, d), F32),
        compiler_params=_cparams(("arbitrary",)),
        name="moe_combine",
    )(dest, y_sorted, x, gate, mods)


def moe_layer(lay, x, g, mods, rw, rb, w1, b1, w2, b2):
    n, d = x.shape
    h, idx_t, gate_t = router(lay, x, g, mods, rw, rb)
    expert = idx_t.T.reshape(-1)
    n_asg = expert.shape[0]
    onehot = (expert[:, None] == jnp.arange(N_EXPERTS, dtype=I32)[None, :]).astype(I32)
    csum = jnp.cumsum(onehot, axis=0)
    rank = jnp.sum((csum - onehot) * onehot, axis=1)
    count = csum[-1]
    padded = (count + MOE_TILE - 1) // MOE_TILE * MOE_TILE
    pend = jnp.cumsum(padded)
    dest = ((pend - padded)[expert] + rank).astype(I32)
    n_blk = n_asg // MOE_TILE + N_EXPERTS
    src_tok = jnp.zeros((n_blk * MOE_TILE,), I32).at[dest].set(jnp.arange(n_asg, dtype=I32) // TOP_K)
    blk_expert = jnp.minimum(jnp.searchsorted(pend, jnp.arange(n_blk, dtype=I32) * MOE_TILE, side="right"),
                             N_EXPERTS - 1).astype(I32)
    n_used = (pend[-1] // MOE_TILE).astype(I32).reshape(1)
    x_sorted = gather_rows(h, src_tok)
    y_sorted = experts(x_sorted, blk_expert, n_used, w1, b1, w2, b2)
    return combine(lay, x, y_sorted, dest, gate_t.T, mods)


def kernel(x_prompt, x_sample, cache_attn_k, cache_attn_v, state_rwkv, state_lru, c, c_ctx,
           mod_w, mod_b, norm1_g, norm2_g, ev_w_in, ev_w_out, q_norm_g, k_norm_g,
           rwkv_mu, rwkv_w0, rwkv_w_up, rwkv_a0, rwkv_a_up, rwkv_g_up, rwkv_k_k, rwkv_k_a,
           rwkv_r_k, rwkv_ln_g, rwkv_ln_b, od_w_in, od_w_out, conv_w, conv_b,
           lru_wa, lru_ba, lru_wx, lru_bx, lru_lambda,
           router_w, router_b, exp_w1, exp_b1, exp_w2, exp_b2):
    bp, tp, d = x_prompt.shape
    bs, ts, _ = x_sample.shape
    depth = mod_w.shape[0]
    lay = Layout(bp, tp, bs, ts)
    assert bs < MOD_ROWS and tp % TOKEN_TILE == 0 and ts % TOKEN_TILE == 0 and d == D_MODEL
    n_p = lay.n_p

    x = jnp.concatenate([x_prompt.reshape(n_p, d), x_sample.reshape(bs * ts, d)], axis=0)
    cvec = jnp.zeros((MOD_ROWS, d), F32).at[:bs].set(c).at[bs].set(c_ctx)
    mods_all = modulation(cvec, mod_w, mod_b)
    rope = rope_tables(ts)

    new_k, new_v, new_rw, new_lru = [], [], [], []
    for l in range(depth):
        j = l // 2
        mods = mods_all[l]
        if l % 2 == 0:
            qkv, rw = norm_proj(lay, x, norm1_g[l], mods, 0, ev_w_in[j].astype(BF16),
                                (ATT_Q + 2 * ATT_KV, RWKV_COLS))
            att_p, k_norm = attention(qkv, 0, bp, tp, q_norm_g[j], k_norm_g[j])
            cache = (cache_attn_k[:, j].reshape(bs, -1, ATT_KV), cache_attn_v[:, j].reshape(bs, -1, ATT_KV))
            (att_s,) = attention(qkv, n_p, bs, ts, q_norm_g[j], k_norm_g[j], cache=cache, rope=rope)
            att = jnp.concatenate([att_p, att_s], axis=0)
            prm = (rwkv_mu[j], rwkv_w0[j], rwkv_w_up[j], rwkv_a0[j], rwkv_a_up[j], rwkv_g_up[j],
                   rwkv_k_k[j], rwkv_k_a[j], rwkv_r_k[j].reshape(-1))
            r, v, alpha, lw0, be0, kd0, lw1, be1, kd1, bonus, gate = rwkv_prep(lay, rw, prm)
            s_lat = jnp.swapaxes(state_rwkv[:, j], -1, -2).reshape(bs, 2 * RWKV_HEADS, RWKV_HD, RWKV_HD)
            s0 = jnp.concatenate([jnp.zeros((bp,) + s_lat.shape[1:], F32), s_lat], axis=0)
            o_f, o_b, s_fin = rwkv_chunks(lay, r, v, alpha, lw0, be0, kd0, lw1, be1, kd1, s0)
            x = even_out(lay, x, att, o_f, o_b, bonus, gate, rwkv_ln_g[j], rwkv_ln_b[j], ev_w_out[j], mods)
            new_k.append(k_norm.reshape(bp, tp, ATT_KV_HEADS, HEAD_DIM))
            new_v.append(qkv[:n_p, ATT_Q + ATT_KV:].reshape(bp, tp, ATT_KV_HEADS, HEAD_DIM))
            new_rw.append(jnp.swapaxes(s_fin[:bp].reshape(bp, 2, RWKV_HEADS, RWKV_HD, RWKV_HD), -1, -2))
        else:
            (proj,) = norm_proj(lay, x, norm1_g[l], mods, 0, od_w_in[j].astype(BF16), (2 * D_RNN,))
            gate, a0, b0, a1, b1 = lru_prep(lay, proj, conv_w[j], conv_b[j], lru_wa[j], lru_ba[j], lru_wx[j],
                                            lru_bx[j], lru_lambda[j])
            h0 = jnp.concatenate([jnp.zeros((bp, 2, D_RNN), F32), state_lru[:, j]], axis=0)
            hf, hb = lru_scan(lay, a0, b0, a1, b1, h0)
            x = odd_out(lay, x, gate, hf, hb, od_w_out[j], mods)
            hf_p = hf[:n_p].reshape(bp, tp, D_RNN)
            hb_p = hb[:n_p].reshape(bp, tp, D_RNN)
            new_lru.append(jnp.stack([hf_p[:, -1], hb_p[:, 0]], axis=1))
        x = moe_layer(lay, x, norm2_g[l], mods, router_w[l], router_b[l], exp_w1[l], exp_b1[l], exp_w2[l],
                      exp_b2[l])

    y_prompt = x[:n_p].reshape(bp, tp, d)
    y_sample = x[n_p:].reshape(bs, ts, d)
    return (y_prompt, y_sample, jnp.stack(new_k, axis=1), jnp.stack(new_v, axis=1), jnp.stack(new_rw, axis=1),
            jnp.stack(new_lru, axis=1))
```

```python
import functools
from typing import NamedTuple

import numpy as np
import jax
import jax.numpy as jnp
from jax import lax
from jax.experimental import pallas as pl
from jax.experimental.pallas import tpu as pltpu

F32 = jnp.float32
BF16 = jnp.bfloat16
I32 = jnp.int32
HIGHEST = lax.Precision.HIGHEST

D_MODEL = 1024
EPS = 1e-6
GRID_W = 64
ATT_HEADS = 8
ATT_KV_HEADS = 2
HEAD_DIM = 64
GQA_GROUP = ATT_HEADS // ATT_KV_HEADS
ATT_Q = ATT_HEADS * HEAD_DIM
ATT_KV = ATT_KV_HEADS * HEAD_DIM
ROPE_THETA = 10000.0
RWKV_HEADS = 8
RWKV_HD = 64
RWKV_DIM = RWKV_HEADS * RWKV_HD
DECAY_RANK = 64
ICLR_RANK = 64
GATE_RANK = 128
RWKV_COLS = 3 * RWKV_DIM + 2 * DECAY_RANK + 2 * ICLR_RANK + GATE_RANK
GN_EPS = 64e-5
D_RNN = D_MODEL
LRU_BS = 64
LRU_GROUP = 256
CONV_W = 4
LRU_C = 8.0
N_EXPERTS = 32
TOP_K = 4
D_EXPERT = D_MODEL
SWIGLU_LIMIT = 7.0
SWIGLU_ALPHA = 1.702

TOKEN_TILE = 256
RWKV_CHUNK = 64
INV_BLOCK = 16
ATT_Q_TILE = 128
MOE_TILE = 256
HALO = 8
MOD_ROWS = 16
VMEM_LIMIT = 56 * 1024 * 1024


class Layout(NamedTuple):
    bp: int
    tp: int
    bs: int
    ts: int

    @property
    def n_p(self):
        return self.bp * self.tp

    @property
    def n(self):
        return self.bp * self.tp + self.bs * self.ts

    def tiles(self, tile):
        return self.n // tile

    def seq_of_tile(self, i, tile):
        npt = self.n_p // tile
        is_p = i < npt
        ii = jnp.where(is_p, i, i - npt)
        per = jnp.where(is_p, self.tp // tile, self.ts // tile)
        return is_p, ii // per, ii % per, per

    def mod_row(self, i, tile):
        is_p, seq, _, _ = self.seq_of_tile(i, tile)
        return jnp.where(is_p, self.bs, seq)


def _cparams(sem):
    return pltpu.CompilerParams(dimension_semantics=sem, vmem_limit_bytes=VMEM_LIMIT)


def _dot(a, b):
    return jnp.dot(a, b, preferred_element_type=F32)


def _dot_nt(a, b):
    return lax.dot_general(a, b, (((1,), (1,)), ((), ())), preferred_element_type=F32)


def _dot_tn(a, b):
    return lax.dot_general(a, b, (((0,), (0,)), ((), ())), preferred_element_type=F32)


def _split_dot(x, m01):
    hi = x.astype(BF16)
    lo = (x - hi.astype(F32)).astype(BF16)
    return _dot(hi, m01) + _dot(lo, m01)


def _split_dot_left(m01, x):
    hi = x.astype(BF16)
    lo = (x - hi.astype(F32)).astype(BF16)
    return _dot(m01, hi) + _dot(m01, lo)


def _sigmoid(x):
    return 1.0 / (1.0 + jnp.exp(-x))


def _block_ones(n, blk):
    idx = np.arange(n) // blk
    return jnp.asarray((idx[:, None] == idx[None, :]).astype(np.float32), dtype=BF16)


def _mod_kernel(c_ref, w_ref, b_ref, o_ref):
    c = c_ref[...]
    s = c * _sigmoid(c)
    o_ref[0] = jnp.dot(s, w_ref[0], preferred_element_type=F32, precision=HIGHEST) + b_ref[0]


def modulation(cvec, mod_w, mod_b):
    depth, d, six_d = mod_w.shape
    nchunk = six_d // d
    out = pl.pallas_call(
        _mod_kernel,
        grid=(depth, nchunk),
        in_specs=[
            pl.BlockSpec((MOD_ROWS, d), lambda l, k: (0, 0)),
            pl.BlockSpec((1, d, d), lambda l, k: (l, 0, k)),
            pl.BlockSpec((1, 1, d), lambda l, k: (l, 0, k)),
        ],
        out_specs=pl.BlockSpec((1, MOD_ROWS, d), lambda l, k: (l, 0, k)),
        out_shape=jax.ShapeDtypeStruct((depth, MOD_ROWS, six_d), F32),
        compiler_params=_cparams(("arbitrary", "arbitrary")),
        name="modulation",
    )(cvec, mod_w, mod_b.reshape(depth, 1, six_d))
    return out.reshape(depth, MOD_ROWS * nchunk, 1, d)


def _mod_spec(lay, k, tile):
    return pl.BlockSpec((1, 1, D_MODEL), lambda i, *_: (lay.mod_row(i, tile) * 6 + k, 0, 0))


def _norm_mod(x, g, shift, scale):
    ms = jnp.mean(x * x, axis=-1, keepdims=True)
    h = x * lax.rsqrt(ms + EPS) * g
    return h * (1.0 + scale) + shift


def _norm_proj_kernel(x_ref, g_ref, sh_ref, sc_ref, w_ref, *o_refs, splits):
    h = _norm_mod(x_ref[...], g_ref[...], sh_ref[0], sc_ref[0])
    y = _dot(h.astype(BF16), w_ref[...])
    off = 0
    for o_ref, n in zip(o_refs, splits):
        o_ref[...] = y[:, off:off + n]
        off += n


def norm_proj(lay, x, g, mods, k_shift, w_bf16, splits):
    n, d = x.shape
    n_out = w_bf16.shape[1]
    tile = TOKEN_TILE
    return pl.pallas_call(
        functools.partial(_norm_proj_kernel, splits=splits),
        grid=(n // tile,),
        in_specs=[
            pl.BlockSpec((tile, d), lambda i: (i, 0)),
            pl.BlockSpec((1, d), lambda i: (0, 0)),
            _mod_spec(lay, k_shift, tile),
            _mod_spec(lay, k_shift + 1, tile),
            pl.BlockSpec((d, n_out), lambda i: (0, 0)),
        ],
        out_specs=[pl.BlockSpec((tile, s), lambda i: (i, 0)) for s in splits],
        out_shape=[jax.ShapeDtypeStruct((n, s), F32) for s in splits],
        compiler_params=_cparams(("arbitrary",)),
        name="norm_proj",
    )(x, g.reshape(1, d), mods, mods, w_bf16)


def _head_norm(x, g, ones):
    ms = _split_dot(x * x, ones) * (1.0 / HEAD_DIM)
    return x * lax.rsqrt(ms + EPS) * g


def _rope(x, cos, sin_signed):
    n = x.shape[1]
    nxt = pltpu.roll(x, n - 1, 1)
    prv = pltpu.roll(x, 1, 1)
    lane = lax.broadcasted_iota(I32, x.shape, 1)
    swapped = jnp.where(lane % 2 == 0, nxt, prv)
    return x * cos + swapped * sin_signed


def _attn_kernel(*refs, t_len, n_ctx, rotary):
    if rotary:
        (qkv_ref, ck_ref, cv_ref, cos_ref, sin_ref, qg_ref, kg_ref, oq_ref, ok_ref,
         att_ref, k_scr, v_scr) = refs
    else:
        qkv_ref, qg_ref, kg_ref, oq_ref, ok_ref, att_ref, kn_ref, k_scr, v_scr = refs
    qi = pl.program_id(1)
    tq = ATT_Q_TILE

    @pl.when(qi == 0)
    def _prepare_keys():
        k = _head_norm(qkv_ref[:, ATT_Q:ATT_Q + ATT_KV], kg_ref[...], ok_ref[...])
        v = qkv_ref[:, ATT_Q + ATT_KV:ATT_Q + 2 * ATT_KV]
        if rotary:
            k = _rope(k, cos_ref[:, :ATT_KV], sin_ref[:, :ATT_KV])
            k_scr[0:n_ctx, :] = ck_ref[0].astype(BF16)
            v_scr[0:n_ctx, :] = cv_ref[0].astype(BF16)
        else:
            kn_ref[...] = k
        k_scr[n_ctx:n_ctx + t_len, :] = k.astype(BF16)
        v_scr[n_ctx:n_ctx + t_len, :] = v.astype(BF16)

    row0 = pl.multiple_of(qi * tq, tq)
    q = _head_norm(qkv_ref[pl.ds(row0, tq), 0:ATT_Q], qg_ref[...], oq_ref[...])
    if rotary:
        q = _rope(q, cos_ref[pl.ds(row0, tq), :], sin_ref[pl.ds(row0, tq), :])
    q = (q * HEAD_DIM ** -0.5).astype(BF16)
    outs = []
    for j in range(ATT_KV_HEADS):
        kj = k_scr[:, j * HEAD_DIM:(j + 1) * HEAD_DIM]
        vj = v_scr[:, j * HEAD_DIM:(j + 1) * HEAD_DIM]
        qs = jnp.concatenate(
            [q[:, (j * GQA_GROUP + g) * HEAD_DIM:(j * GQA_GROUP + g + 1) * HEAD_DIM] for g in range(GQA_GROUP)],
            axis=0)
        s = _dot_nt(qs, kj)
        p = jnp.exp(s - jnp.max(s, axis=-1, keepdims=True))
        p = p / jnp.sum(p, axis=-1, keepdims=True)
        o = _dot(p.astype(BF16), vj)
        outs.extend(o[g * tq:(g + 1) * tq] for g in range(GQA_GROUP))
    att_ref[...] = jnp.concatenate(outs, axis=1)


def attention(qkv, seq0, n_seq, t_len, q_g, k_g, cache=None, rope=None):
    rotary = cache is not None
    n_ctx = cache[0].shape[1] if rotary else 0
    blk0 = seq0 // t_len
    n_q = t_len // ATT_Q_TILE
    width = qkv.shape[1]
    qg = jnp.tile(q_g, ATT_HEADS).reshape(1, ATT_Q)
    kg = jnp.tile(k_g, ATT_KV_HEADS).reshape(1, ATT_KV)
    const = lambda shape: pl.BlockSpec(shape, lambda b, qi: (0,) * len(shape))
    in_specs = [pl.BlockSpec((t_len, width), lambda b, qi: (blk0 + b, 0))]
    args = [qkv]
    if rotary:
        in_specs += [pl.BlockSpec((1, n_ctx, ATT_KV), lambda b, qi: (b, 0, 0))] * 2
        in_specs += [const((t_len, ATT_Q))] * 2
        args += [cache[0], cache[1], rope[0], rope[1]]
    in_specs += [const((1, ATT_Q)), const((1, ATT_KV)), const((ATT_Q, ATT_Q)), const((ATT_KV, ATT_KV))]
    args += [qg, kg, _block_ones(ATT_Q, HEAD_DIM), _block_ones(ATT_KV, HEAD_DIM)]
    out_specs = [pl.BlockSpec((ATT_Q_TILE, ATT_Q), lambda b, qi: (b * n_q + qi, 0))]
    out_shape = [jax.ShapeDtypeStruct((n_seq * t_len, ATT_Q), F32)]
    if not rotary:
        out_specs.append(pl.BlockSpec((t_len, ATT_KV), lambda b, qi: (b, 0)))
        out_shape.append(jax.ShapeDtypeStruct((n_seq * t_len, ATT_KV), F32))
    return pl.pallas_call(
        functools.partial(_attn_kernel, t_len=t_len, n_ctx=n_ctx, rotary=rotary),
        grid=(n_seq, n_q),
        in_specs=in_specs,
        out_specs=out_specs,
        out_shape=out_shape,
        scratch_shapes=[pltpu.VMEM((n_ctx + t_len, ATT_KV), BF16), pltpu.VMEM((n_ctx + t_len, ATT_KV), BF16)],
        compiler_params=_cparams(("arbitrary", "arbitrary")),
        name="attention_latent" if rotary else "attention_context",
    )(*args)


def rope_tables(t_len):
    t = jnp.arange(t_len)
    pos = jnp.stack([t // GRID_W, t % GRID_W], axis=-1).astype(F32)
    n_freq = HEAD_DIM // 4
    inv = ROPE_THETA ** (-jnp.arange(n_freq, dtype=F32) / n_freq)
    ang = (pos[:, :, None] * inv).reshape(t_len, 2 * n_freq)
    cos = jnp.repeat(jnp.cos(ang), 2, axis=1)
    sin = jnp.repeat(jnp.sin(ang), 2, axis=1) * jnp.tile(jnp.asarray([-1.0, 1.0], F32), HEAD_DIM // 2)
    return jnp.tile(cos, (1, ATT_HEADS)), jnp.tile(sin, (1, ATT_HEADS))


def _shifted_rows(x, prev_row, next_row):
    m = x.shape[0]
    row = lax.broadcasted_iota(I32, x.shape, 0)
    prv = jnp.where(row == 0, prev_row, pltpu.roll(x, 1, 0))
    nxt = jnp.where(row == m - 1, next_row, pltpu.roll(x, m - 1, 0))
    return prv, nxt


def _rwkv_prep_kernel(x_ref, prev_ref, next_ref, mu_ref, kk_ref, ka_ref, rk_ref, w0_ref, wup_ref, a0_ref,
                      aup_ref, gup_ref, ones_ref,
                      r_ref, v_ref, al_ref, lw0_ref, be0_ref, kd0_ref, lw1_ref, be1_ref, kd1_ref, bonus_ref,
                      gate_ref, *, lay):
    i = pl.program_id(0)
    _, _, j, per = lay.seq_of_tile(i, TOKEN_TILE)
    x = x_ref[...]
    prev_row = jnp.where(j == 0, 0.0, prev_ref[HALO - 1:HALO, :])
    next_row = jnp.where(j == per - 1, 0.0, next_ref[0:1, :])
    prv, nxt = _shifted_rows(x, prev_row, next_row)
    u = x + mu_ref[...] * (0.5 * (prv + nxt) - x)

    dim = RWKV_DIM
    r, k, v = u[:, :dim], u[:, dim:2 * dim], u[:, 2 * dim:3 * dim]
    o = 3 * dim
    wd = u[:, o:o + 2 * DECAY_RANK]
    o += 2 * DECAY_RANK
    ad = u[:, o:o + 2 * ICLR_RANK]
    o += 2 * ICLR_RANK
    gd = u[:, o:o + GATE_RANK]

    ones = ones_ref[...]
    kk = k * kk_ref[...]
    norm = jnp.sqrt(_split_dot(kk * kk, ones))
    alpha = kk / jnp.maximum(norm, 1e-12)
    r_ref[...] = r
    v_ref[...] = v
    al_ref[...] = alpha

    tanh_wd = jnp.tanh(wd).astype(BF16)
    ad16 = ad.astype(BF16)
    kd_sum = None
    for d, (lw_ref, be_ref, kd_ref) in enumerate(((lw0_ref, be0_ref, kd0_ref), (lw1_ref, be1_ref, kd1_ref))):
        w_log = w0_ref[d:d + 1, :] + _dot(tanh_wd[:, d * DECAY_RANK:(d + 1) * DECAY_RANK], wup_ref[d])
        lw_ref[...] = -_sigmoid(w_log) * float(np.exp(-0.5))
        a = _sigmoid(a0_ref[d:d + 1, :] + _dot(ad16[:, d * ICLR_RANK:(d + 1) * ICLR_RANK], aup_ref[d]))
        kd = k * (1.0 + (a - 1.0) * ka_ref[...])
        be_ref[...] = alpha * a
        kd_ref[...] = kd
        kd_sum = kd if kd_sum is None else kd_sum + kd
    bonus_ref[...] = _split_dot(r * kd_sum * rk_ref[...], ones) * v
    gate_ref[...] = _dot(_sigmoid(gd).astype(BF16), gup_ref[...])


def rwkv_prep(lay, rw, prm):
    mu, w0, w_up, a0, a_up, g_up, k_k, k_a, r_k = prm
    n, cols = rw.shape
    tile = TOKEN_TILE
    hb = tile // HALO
    n_halo = n // HALO
    dim = RWKV_DIM
    const = lambda shape: pl.BlockSpec(shape, lambda i: (0,) * len(shape))
    row = lambda a: a.reshape(1, -1)
    out_spec = pl.BlockSpec((tile, dim), lambda i: (i, 0))
    return pl.pallas_call(
        functools.partial(_rwkv_prep_kernel, lay=lay),
        grid=(n // tile,),
        in_specs=[
            pl.BlockSpec((tile, cols), lambda i: (i, 0)),
            pl.BlockSpec((HALO, cols), lambda i: (jnp.maximum(i * hb - 1, 0), 0)),
            pl.BlockSpec((HALO, cols), lambda i: (jnp.minimum((i + 1) * hb, n_halo - 1), 0)),
            const((1, cols)), const((1, dim)), const((1, dim)), const((1, dim)),
            const((2, dim)), const((2, DECAY_RANK, dim)), const((2, dim)), const((2, ICLR_RANK, dim)),
            const((GATE_RANK, dim)), const((dim, dim)),
        ],
        out_specs=[out_spec] * 11,
        out_shape=[jax.ShapeDtypeStruct((n, dim), F32)] * 11,
        compiler_params=_cparams(("arbitrary",)),
        name="rwkv_prep",
    )(rw, rw, rw, row(mu), row(k_k), row(k_a), row(r_k), w0, w_up.astype(BF16), a0, a_up.astype(BF16),
      g_up.astype(BF16), _block_ones(dim, RWKV_HD))


def _each(fn, *lists):
    return [fn(*args) for args in zip(*lists)]


def _unit_triangular_inverses(l_mats, eye, diag_blocks):
    mm = lambda a, b: _dot(a.astype(BF16), b.astype(BF16))
    ld = [jnp.where(diag_blocks, l, 0.0) for l in l_mats]
    lo = _each(lambda l, d: l - d, l_mats, ld)
    x = [eye - d for d in ld]
    p = ld
    for _ in range(int(np.log2(INV_BLOCK)) - 1):
        p = _each(mm, p, p)
        x = _each(lambda xi, pi: xi + mm(xi, pi), x, p)
    nb = _each(mm, x, lo)
    y = [eye - m for m in nb]
    p = nb
    for _ in range(int(np.log2(RWKV_CHUNK // INV_BLOCK)) - 1):
        p = _each(mm, p, p)
        y = _each(lambda yi, pi: yi + mm(yi, pi), y, p)
    return _each(mm, y, x)


def _chunk_factors(r, alpha, lw, beta, kd, incl01):
    c_incl = _split_dot_left(incl01, lw)
    c_tot = jnp.sum(lw, axis=0, keepdims=True)
    inv_p = jnp.exp(-c_incl)
    to_end = jnp.exp(c_tot - c_incl)
    r_bar = r * jnp.exp(c_incl)
    b16 = lambda m: m.astype(BF16)
    return dict(a_bar=b16(alpha * jnp.exp(c_incl - lw)), r_bar=r_bar, r_bar16=b16(r_bar),
                b_bar=b16(beta * inv_p), k_bar=b16(kd * inv_p), b_til=b16(beta * to_end),
                k_til=b16(kd * to_end), p_tot=jnp.exp(c_tot))


def _rwkv_chunk_kernel(rf_ref, vf_ref, af_ref, lwf_ref, bef_ref, kdf_ref,
                       rb_ref, vb_ref, ab_ref, lwb_ref, beb_ref, kdb_ref, s0_ref,
                       of_ref, ob_ref, sfin_ref, h_scr, *, lay):
    s = pl.program_id(0)
    _, _, c, per = lay.seq_of_tile(s, RWKV_CHUNK)

    @pl.when(c == 0)
    def _load_state():
        h_scr[...] = s0_ref[0]

    n = RWKV_CHUNK
    hd = RWKV_HD
    row = lax.broadcasted_iota(I32, (n, n), 0)
    col = lax.broadcasted_iota(I32, (n, n), 1)
    eye = (row == col).astype(F32)
    diag_blocks = (row // INV_BLOCK) == (col // INV_BLOCK)
    b16 = lambda m: m.astype(BF16)
    mm = lambda a, b: _dot(b16(a), b16(b))

    fac, strict_m, incl_m, v16 = [], [], [], []
    for d, refs in enumerate(((rf_ref, vf_ref, af_ref, lwf_ref, bef_ref, kdf_ref),
                              (rb_ref, vb_ref, ab_ref, lwb_ref, beb_ref, kdb_ref))):
        r_ref, v_ref, a_ref, lw_ref, be_ref, kd_ref = refs
        strict = (col > row) if d else (col < row)
        incl = (col >= row) if d else (col <= row)
        f = _chunk_factors(r_ref[...], a_ref[...], lw_ref[...], be_ref[...], kd_ref[...],
                           jnp.where(incl, 1.0, 0.0).astype(BF16))
        v = b16(v_ref[...])
        for h in range(RWKV_HEADS):
            sl = slice(h * hd, (h + 1) * hd)
            fac.append({k: a[:, sl] for k, a in f.items()})
            strict_m.append(strict)
            incl_m.append(incl)
            v16.append(v[:, sl])

    gram = [_dot_nt(jnp.concatenate([f["a_bar"], f["r_bar16"]], axis=0),
                    jnp.concatenate([f["b_bar"], f["k_bar"]], axis=0)) for f in fac]
    l_mat = _each(lambda g, m: jnp.where(m, g[:n, :n], 0.0), gram, strict_m)
    a_k = _each(lambda g, m: jnp.where(m, g[:n, n:], 0.0), gram, strict_m)
    r_b = _each(lambda g, m: jnp.where(m, g[n:, :n], 0.0), gram, incl_m)
    r_k = _each(lambda g, m: jnp.where(m, g[n:, n:], 0.0), gram, incl_m)
    akv = _each(mm, a_k, v16)
    kt_v = _each(lambda f, v: _dot_tn(f["k_til"], v), fac, v16)
    rk_v = _each(mm, r_k, v16)
    t_inv = _unit_triangular_inverses(l_mat, eye, diag_blocks)
    mw = _each(lambda t, f, w: b16(mm(t, jnp.concatenate([f["a_bar"], b16(w)], axis=1))), t_inv, fac, akv)
    bt_mw = _each(lambda f, m: _dot_tn(f["b_til"], m), fac, mw)
    rb_mw = _each(mm, r_b, mw)
    h0 = [h_scr[u] for u in range(len(fac))]
    h016 = [b16(h) for h in h0]
    outs = _each(lambda f, rb, rk, h: mm(f["r_bar"] - rb[:, :hd], h) + rk - rb[:, hd:], fac, rb_mw, rk_v, h016)
    for u, (f, bt, kv, h, h16) in enumerate(zip(fac, bt_mw, kt_v, h0, h016)):
        decay_col = jnp.sum(eye * f["p_tot"], axis=1, keepdims=True)
        h_scr[u] = decay_col * h - mm(bt[:, :hd], h16) + kv - bt[:, hd:]
    of_ref[...] = jnp.concatenate(outs[:RWKV_HEADS], axis=1)
    ob_ref[...] = jnp.concatenate(outs[RWKV_HEADS:], axis=1)

    @pl.when(c == per - 1)
    def _store_state():
        sfin_ref[0] = h_scr[...]


def rwkv_chunks(lay, r, v, alpha, lw0, be0, kd0, lw1, be1, kd1, s0):
    n, dim = r.shape
    ch = RWKV_CHUNK
    n_seq = lay.bp + lay.bs

    def fwd(s):
        return (s, 0)

    def bwd(s):
        _, _, c, per = lay.seq_of_tile(s, ch)
        return (s - c + (per - 1 - c), 0)

    def seq(s):
        is_p, q, _, _ = lay.seq_of_tile(s, ch)
        return (jnp.where(is_p, q, lay.bp + q), 0, 0, 0)

    state_block = (1, 2 * RWKV_HEADS, RWKV_HD, RWKV_HD)
    return pl.pallas_call(
        functools.partial(_rwkv_chunk_kernel, lay=lay),
        grid=(n // ch,),
        in_specs=[pl.BlockSpec((ch, dim), fwd)] * 6 + [pl.BlockSpec((ch, dim), bwd)] * 6
        + [pl.BlockSpec(state_block, seq)],
        out_specs=[pl.BlockSpec((ch, dim), fwd), pl.BlockSpec((ch, dim), bwd), pl.BlockSpec(state_block, seq)],
        out_shape=[jax.ShapeDtypeStruct((n, dim), F32), jax.ShapeDtypeStruct((n, dim), F32),
                   jax.ShapeDtypeStruct((n_seq,) + state_block[1:], F32)],
        scratch_shapes=[pltpu.VMEM(state_block[1:], F32)],
        compiler_params=_cparams(("arbitrary",)),
        name="rwkv_chunks",
    )(r, v, alpha, lw0, be0, kd0, r, v, alpha, lw1, be1, kd1, s0)


def _even_out_kernel(x_ref, att_ref, of_ref, ob_ref, bonus_ref, gate_ref, lng_ref, lnb_ref, ones_ref,
                     wa_ref, wr_ref, g1_ref, o_ref):
    ones = ones_ref[...]
    osum = of_ref[...] + ob_ref[...]
    mean = _split_dot(osum, ones) * (1.0 / RWKV_HD)
    cen = osum - mean
    var = _split_dot(cen * cen, ones) * (1.0 / RWKV_HD)
    on = cen * lax.rsqrt(var + GN_EPS) * lng_ref[...] + lnb_ref[...]
    rw = (on + bonus_ref[...]) * gate_ref[...]
    y = _dot(att_ref[...].astype(BF16), wa_ref[...]) + _dot(rw.astype(BF16), wr_ref[...])
    o_ref[...] = x_ref[...] + g1_ref[0] * y


def even_out(lay, x, att, o_f, o_b, bonus, gate, ln_g, ln_b, w_out, mods):
    n, d = x.shape
    tile = TOKEN_TILE
    dim = RWKV_DIM
    const = lambda shape: pl.BlockSpec(shape, lambda i: (0,) * len(shape))
    tok = lambda w: pl.BlockSpec((tile, w), lambda i: (i, 0))
    w16 = w_out.astype(BF16)
    return pl.pallas_call(
        _even_out_kernel,
        grid=(n // tile,),
        in_specs=[tok(d), tok(ATT_Q), tok(dim), tok(dim), tok(dim), tok(dim),
                  const((1, dim)), const((1, dim)), const((dim, dim)),
                  const((ATT_Q, d)), const((dim, d)), _mod_spec(lay, 2, tile)],
        out_specs=tok(d),
        out_shape=jax.ShapeDtypeStruct((n, d), F32),
        compiler_params=_cparams(("arbitrary",)),
        name="even_out",
    )(x, att, o_f, o_b, bonus, gate, ln_g.reshape(1, dim), ln_b.reshape(1, dim), _block_ones(dim, RWKV_HD),
      w16[:ATT_Q], w16[ATT_Q:], mods)


def _gelu_tanh(x):
    return 0.5 * x * (1.0 + jnp.tanh(float(np.sqrt(2.0 / np.pi)) * (x + 0.044715 * (x * x * x))))


def _softplus(x):
    return jnp.maximum(x, 0.0) + jnp.log(1.0 + jnp.exp(-jnp.abs(x)))


def _lru_prep_kernel(x_ref, prev_ref, next_ref, cw_ref, cb_ref, wbd_ref, ba_ref, bx_ref, lam_ref,
                     gate_ref, a0_ref, b0_ref, a1_ref, b1_ref, *, lay):
    i = pl.program_id(0)
    _, _, j, per = lay.seq_of_tile(i, TOKEN_TILE)
    tile = TOKEN_TILE
    gate_ref[...] = _gelu_tanh(x_ref[:, :D_RNN])
    x = x_ref[:, D_RNN:]
    first = j == 0
    last = j == per - 1
    row = lax.broadcasted_iota(I32, x.shape, 0)
    p1 = jnp.where(first, 0.0, prev_ref[HALO - 1:HALO, :])
    p2 = jnp.where(first, 0.0, prev_ref[HALO - 2:HALO - 1, :])
    n1 = jnp.where(last, 0.0, next_ref[0:1, :])
    xm1 = jnp.where(row == 0, p1, pltpu.roll(x, 1, 0))
    xm2 = jnp.where(row == 0, p2, jnp.where(row == 1, p1, pltpu.roll(x, 2, 0)))
    xp1 = jnp.where(row == tile - 1, n1, pltpu.roll(x, tile - 1, 0))
    u = cb_ref[...] + xm2 * cw_ref[0:1, :] + xm1 * cw_ref[1:2, :] + x * cw_ref[2:3, :] + xp1 * cw_ref[3:4, :]

    u16 = u.astype(BF16)
    n_grp = D_RNN // LRU_GROUP
    z = [_dot(u16[:, g * LRU_GROUP:(g + 1) * LRU_GROUP], wbd_ref[g]) for g in range(n_grp)]
    pick = lambda m: jnp.concatenate([zg[:, m * LRU_GROUP:(m + 1) * LRU_GROUP] for zg in z], axis=1)
    for d, (a_ref, b_ref) in enumerate(((a0_ref, b0_ref), (a1_ref, b1_ref))):
        ga = _sigmoid(pick(2 * d) + ba_ref[d:d + 1, :])
        gx = _sigmoid(pick(2 * d + 1) + bx_ref[d:d + 1, :])
        log_a = -LRU_C * ga * _softplus(-lam_ref[d:d + 1, :])
        a_ref[...] = jnp.exp(log_a)
        b_ref[...] = jnp.sqrt(1.0 - jnp.exp(2.0 * log_a)) * gx * u


def lru_prep(lay, proj, conv_w, conv_b, wa, ba, wx, bx, lam):
    n, cols = proj.shape
    tile = TOKEN_TILE
    hb = tile // HALO
    n_halo = n // HALO
    dr = D_RNN
    per = LRU_GROUP // LRU_BS
    n_grp = dr // LRU_GROUP

    def block_diag(w):
        w = w.reshape(n_grp, per, LRU_BS, LRU_BS)
        eye = jnp.eye(per, dtype=w.dtype)
        return jnp.einsum("gpcd,pq->gpcqd", w, eye).reshape(n_grp, LRU_GROUP, LRU_GROUP)

    wbd = jnp.concatenate([block_diag(wa[0]), block_diag(wx[0]), block_diag(wa[1]), block_diag(wx[1])],
                          axis=2).astype(BF16)
    const = lambda shape: pl.BlockSpec(shape, lambda i: (0,) * len(shape))
    out_spec = pl.BlockSpec((tile, dr), lambda i: (i, 0))
    return pl.pallas_call(
        functools.partial(_lru_prep_kernel, lay=lay),
        grid=(n // tile,),
        in_specs=[
            pl.BlockSpec((tile, cols), lambda i: (i, 0)),
            pl.BlockSpec((HALO, dr), lambda i: (jnp.maximum(i * hb - 1, 0), 1)),
            pl.BlockSpec((HALO, dr), lambda i: (jnp.minimum((i + 1) * hb, n_halo - 1), 1)),
            const((CONV_W, dr)), const((1, dr)), const((n_grp, LRU_GROUP, 4 * LRU_GROUP)),
            const((2, dr)), const((2, dr)), const((2, dr)),
        ],
        out_specs=[out_spec] * 5,
        out_shape=[jax.ShapeDtypeStruct((n, dr), F32)] * 5,
        compiler_params=_cparams(("arbitrary",)),
        name="lru_prep",
    )(proj, proj, proj, conv_w, conv_b.reshape(1, dr), wbd, ba, bx, lam)


def _lru_scan_kernel(af_ref, bf_ref, ab_ref, bb_ref, h0_ref, hf_ref, hb_ref, carry, *, lay):
    i = pl.program_id(0)
    _, _, j, _ = lay.seq_of_tile(i, TOKEN_TILE)
    tile = TOKEN_TILE

    @pl.when(j == 0)
    def _load_state():
        carry[...] = h0_ref[0]

    def step(t, hs):
        hf, hb = hs
        tb = tile - 1 - t
        hf = af_ref[pl.ds(t, 1), :] * hf + bf_ref[pl.ds(t, 1), :]
        hb = ab_ref[pl.ds(tb, 1), :] * hb + bb_ref[pl.ds(tb, 1), :]
        hf_ref[pl.ds(t, 1), :] = hf
        hb_ref[pl.ds(tb, 1), :] = hb
        return hf, hb

    hf, hb = lax.fori_loop(0, tile, step, (carry[0:1, :], carry[1:2, :]))
    carry[0:1, :] = hf
    carry[1:2, :] = hb


def lru_scan(lay, a0, b0, a1, b1, h0):
    n, dr = a0.shape
    tile = TOKEN_TILE

    def fwd(i):
        return (i, 0)

    def bwd(i):
        _, _, j, per = lay.seq_of_tile(i, tile)
        return (i - j + (per - 1 - j), 0)

    def seq(i):
        is_p, q, _, _ = lay.seq_of_tile(i, tile)
        return (jnp.where(is_p, q, lay.bp + q), 0, 0)

    return pl.pallas_call(
        functools.partial(_lru_scan_kernel, lay=lay),
        grid=(n // tile,),
        in_specs=[pl.BlockSpec((tile, dr), fwd)] * 2 + [pl.BlockSpec((tile, dr), bwd)] * 2
        + [pl.BlockSpec((1, 2, dr), seq)],
        out_specs=[pl.BlockSpec((tile, dr), fwd), pl.BlockSpec((tile, dr), bwd)],
        out_shape=[jax.ShapeDtypeStruct((n, dr), F32)] * 2,
        scratch_shapes=[pltpu.VMEM((2, dr), F32)],
        compiler_params=_cparams(("arbitrary",)),
        name="lru_scan",
    )(a0, b0, a1, b1, h0)


def _odd_out_kernel(x_ref, gate_ref, hf_ref, hb_ref, w_ref, g1_ref, o_ref):
    y = _dot((gate_ref[...] * (hf_ref[...] + hb_ref[...])).astype(BF16), w_ref[...])
    o_ref[...] = x_ref[...] + g1_ref[0] * y


def odd_out(lay, x, gate, hf, hb, w_out, mods):
    n, d = x.shape
    tile = TOKEN_TILE
    tok = lambda w: pl.BlockSpec((tile, w), lambda i: (i, 0))
    return pl.pallas_call(
        _odd_out_kernel,
        grid=(n // tile,),
        in_specs=[tok(d), tok(D_RNN), tok(D_RNN), tok(D_RNN), pl.BlockSpec((D_RNN, d), lambda i: (0, 0)),
                  _mod_spec(lay, 2, tile)],
        out_specs=tok(d),
        out_shape=jax.ShapeDtypeStruct((n, d), F32),
        compiler_params=_cparams(("arbitrary",)),
        name="odd_out",
    )(x, gate, hf, hb, w_out.astype(BF16), mods)


def _pack_bf16_pairs(x):
    n = x.shape[1] // 2
    hi = pltpu.bitcast(x[:, :n].astype(BF16).astype(F32), jnp.uint32)
    lo = pltpu.bitcast(x[:, n:].astype(BF16).astype(F32), jnp.uint32)
    return hi | (lo >> 16)


def _unpack_bf16_pairs(u):
    a = pltpu.bitcast(u & jnp.uint32(0xFFFF0000), F32)
    b = pltpu.bitcast(u << 16, F32)
    return jnp.concatenate([a, b], axis=1).astype(BF16)


def _router_kernel(x_ref, g_ref, sh_ref, sc_ref, rw_ref, rb_ref, before_ref,
                   h_ref, idx_ref, gate_ref, rank_ref, count_ref, cnt_scr):
    @pl.when(pl.program_id(0) == 0)
    def _reset():
        cnt_scr[...] = jnp.zeros_like(cnt_scr)

    h = _norm_mod(x_ref[...], g_ref[...], sh_ref[0], sc_ref[0])
    h_ref[...] = _pack_bf16_pairs(h)
    logits = lax.dot_general(rw_ref[...], h, (((1,), (1,)), ((), ())), preferred_element_type=F32,
                             precision=HIGHEST) + rb_ref[...]
    e_id = lax.broadcasted_iota(I32, logits.shape, 0)
    vals, ids, hots = [], [], []
    for _ in range(TOP_K):
        m = jnp.max(logits, axis=0, keepdims=True)
        pick = jnp.min(jnp.where(logits == m, e_id, N_EXPERTS), axis=0, keepdims=True)
        hot = e_id == pick
        vals.append(m)
        ids.append(pick)
        hots.append(jnp.where(hot, 1.0, 0.0))
        logits = jnp.where(hot, -jnp.inf, logits)
    top = jnp.concatenate(vals, axis=0)
    p = jnp.exp(top - top[0:1, :])
    gate_ref[...] = p / jnp.sum(p, axis=0, keepdims=True)
    idx_ref[...] = jnp.concatenate(ids, axis=0)
    hot_all = hots[0] + hots[1] + hots[2] + hots[3]
    seen = cnt_scr[:, 0:1] + _dot(hot_all.astype(BF16), before_ref[...])
    rank_ref[...] = jnp.concatenate([jnp.sum(hk * seen, axis=0, keepdims=True) for hk in hots],
                                    axis=0).astype(I32)
    cnt_scr[...] = cnt_scr[...] + jnp.sum(hot_all, axis=1, keepdims=True)
    count_ref[...] = cnt_scr[...]


def router(lay, x, g, mods, rw, rb):
    n, d = x.shape
    tile = TOKEN_TILE
    before = jnp.asarray(np.triu(np.ones((tile, tile), np.float32), 1), dtype=BF16)
    per_tok = pl.BlockSpec((TOP_K, tile), lambda i: (0, i))
    return pl.pallas_call(
        _router_kernel,
        grid=(n // tile,),
        in_specs=[pl.BlockSpec((tile, d), lambda i: (i, 0)), pl.BlockSpec((1, d), lambda i: (0, 0)),
                  _mod_spec(lay, 3, tile), _mod_spec(lay, 4, tile),
                  pl.BlockSpec((N_EXPERTS, d), lambda i: (0, 0)), pl.BlockSpec((N_EXPERTS, 1), lambda i: (0, 0)),
                  pl.BlockSpec((tile, tile), lambda i: (0, 0))],
        out_specs=[pl.BlockSpec((tile, d // 2), lambda i: (i, 0)), per_tok, per_tok, per_tok,
                   pl.BlockSpec((N_EXPERTS, 128), lambda i: (0, 0))],
        out_shape=[jax.ShapeDtypeStruct((n, d // 2), jnp.uint32), jax.ShapeDtypeStruct((TOP_K, n), I32),
                   jax.ShapeDtypeStruct((TOP_K, n), F32), jax.ShapeDtypeStruct((TOP_K, n), I32),
                   jax.ShapeDtypeStruct((N_EXPERTS, 128), F32)],
        scratch_shapes=[pltpu.VMEM((N_EXPERTS, 128), F32)],
        compiler_params=_cparams(("arbitrary",)),
        name="router",
    )(x, g.reshape(1, d), mods, mods, rw.T, rb.reshape(N_EXPERTS, 1), before)


def _row_copy(src_hbm, dst_vmem, sem, src_row, dst_row):
    return pltpu.make_async_copy(src_hbm.at[pl.ds(src_row, 1)], dst_vmem.at[pl.ds(dst_row, 1)], sem)


def _gather_rows_kernel(idx_ref, src_ref, o_ref):
    base = pl.program_id(0) * MOE_TILE

    def copy_row(r, _):
        o_ref[pl.ds(r, 1), :] = src_ref[pl.ds(idx_ref[base + r], 1), :]
        return 0

    lax.fori_loop(0, MOE_TILE, copy_row, 0, unroll=8)


def gather_rows(src, idx):
    n_out = idx.shape[0]
    n, d = src.shape
    return pl.pallas_call(
        _gather_rows_kernel,
        grid_spec=pltpu.PrefetchScalarGridSpec(
            num_scalar_prefetch=1,
            grid=(n_out // MOE_TILE,),
            in_specs=[pl.BlockSpec((n, d), lambda i, idx: (0, 0), pipeline_mode=pl.Buffered(1))],
            out_specs=pl.BlockSpec((MOE_TILE, d), lambda i, idx: (i, 0)),
        ),
        out_shape=jax.ShapeDtypeStruct((n_out, d), src.dtype),
        compiler_params=_cparams(("arbitrary",)),
        name="moe_gather",
    )(idx, src)


def _expert_kernel(blk_e_ref, n_used_ref, x_ref, w1_ref, b1_ref, w2_ref, b2_ref, o_ref, w1_scr, w2_scr):
    i = pl.program_id(0)
    prev_e = blk_e_ref[jnp.maximum(i - 1, 0)]
    new_expert = jnp.logical_or(i == 0, blk_e_ref[i] != prev_e)
    used = i < n_used_ref[0]

    @pl.when(jnp.logical_and(used, new_expert))
    def _cast_weights():
        w1_scr[...] = w1_ref[0].astype(BF16)
        w2_scr[...] = w2_ref[0].astype(BF16)

    @pl.when(used)
    def _compute():
        hb = _dot(_unpack_bf16_pairs(x_ref[...]), w1_scr[...]) + b1_ref[0]
        glu = jnp.minimum(hb[:, :D_EXPERT], SWIGLU_LIMIT)
        lin = jnp.clip(hb[:, D_EXPERT:], -SWIGLU_LIMIT, SWIGLU_LIMIT)
        act = glu * _sigmoid(SWIGLU_ALPHA * glu) * (lin + 1.0)
        o_ref[...] = _dot(act.astype(BF16), w2_scr[...]) + b2_ref[0]

    @pl.when(jnp.logical_not(used))
    def _clear():
        o_ref[...] = jnp.zeros_like(o_ref)


def experts(x_sorted, blk_expert, n_used, layer, w1, b1, w2, b2):
    n_rows = x_sorted.shape[0]
    d, de2 = w1.shape[1:]
    de = w2.shape[1]
    wmap = lambda i, be, nu: (layer * N_EXPERTS + be[i], 0, 0)
    return pl.pallas_call(
        _expert_kernel,
        grid_spec=pltpu.PrefetchScalarGridSpec(
            num_scalar_prefetch=2,
            grid=(n_rows // MOE_TILE,),
            in_specs=[pl.BlockSpec((MOE_TILE, d // 2), lambda i, be, nu: (i, 0)),
                      pl.BlockSpec((1, d, de2), wmap), pl.BlockSpec((1, 1, de2), wmap),
                      pl.BlockSpec((1, de, d), wmap), pl.BlockSpec((1, 1, d), wmap)],
            out_specs=pl.BlockSpec((MOE_TILE, d), lambda i, be, nu: (i, 0)),
            scratch_shapes=[pltpu.VMEM((d, de2), BF16), pltpu.VMEM((de, d), BF16)],
        ),
        out_shape=jax.ShapeDtypeStruct((n_rows, d), F32),
        compiler_params=_cparams(("arbitrary",)),
        name="moe_experts",
    )(blk_expert, n_used, x_sorted, w1, b1, w2, b2)


def _combine_kernel(dest_ref, y_hbm, x_ref, gate_ref, g2_ref, o_ref, buf, sem):
    tile = TOKEN_TILE
    base = pl.program_id(0) * tile

    def start(r, _):
        for k in range(TOP_K):
            _row_copy(y_hbm, buf.at[k], sem, dest_ref[(base + r) * TOP_K + k], r).start()
        return 0

    def wait(r, _):
        for k in range(TOP_K):
            _row_copy(y_hbm, buf.at[k], sem, 0, r).wait()
        return 0

    lax.fori_loop(0, tile, start, 0)
    lax.fori_loop(0, tile, wait, 0)
    gate = gate_ref[...]
    acc = buf[0] * gate[:, 0:1]
    for k in range(1, TOP_K):
        acc = acc + buf[k] * gate[:, k:k + 1]
    o_ref[...] = x_ref[...] + g2_ref[0] * acc


def combine(lay, x, y_sorted, dest, gate, mods):
    n, d = x.shape
    tile = TOKEN_TILE
    return pl.pallas_call(
        _combine_kernel,
        grid_spec=pltpu.PrefetchScalarGridSpec(
            num_scalar_prefetch=1,
            grid=(n // tile,),
            in_specs=[pl.BlockSpec(memory_space=pl.ANY),
                      pl.BlockSpec((tile, d), lambda i, dest: (i, 0)),
                      pl.BlockSpec((tile, TOP_K), lambda i, dest: (i, 0)),
                      _mod_spec(lay, 5, tile)],
            out_specs=pl.BlockSpec((tile, d), lambda i, dest: (i, 0)),
            scratch_shapes=[pltpu.VMEM((TOP_K, tile, d), F32), pltpu.SemaphoreType.DMA(())],
        ),
        out_shape=jax.ShapeDtypeStruct((n, d), F32),
        compiler_params=_cparams(("arbitrary",)),
        name="moe_combine",
    )(dest, y_sorted, x, gate, mods)


def moe_layer(lay, x, g, mods, rw, rb, layer, w1, b1, w2, b2):
    n, d = x.shape
    h, idx_t, gate_t, rank_t, count = router(lay, x, g, mods, rw, rb)
    expert = idx_t.T.reshape(-1)
    rank = rank_t.T.reshape(-1)
    n_asg = expert.shape[0]
    count = count[:, 0].astype(I32)
    padded = (count + MOE_TILE - 1) // MOE_TILE * MOE_TILE
    pend = jnp.cumsum(padded)
    dest = ((pend - padded)[expert] + rank).astype(I32)
    n_blk = n_asg // MOE_TILE + N_EXPERTS
    src_tok = jnp.zeros((n_blk * MOE_TILE,), I32).at[dest].set(jnp.arange(n_asg, dtype=I32) // TOP_K)
    blk_start = jnp.arange(n_blk, dtype=I32) * MOE_TILE
    blk_expert = jnp.minimum(jnp.sum((pend[None, :] <= blk_start[:, None]).astype(I32), axis=1), N_EXPERTS - 1)
    n_used = (pend[-1] // MOE_TILE).astype(I32).reshape(1)
    x_sorted = gather_rows(h, src_tok)
    y_sorted = experts(x_sorted, blk_expert, n_used, layer, w1, b1, w2, b2)
    return combine(lay, x, y_sorted, dest, gate_t.T, mods)


def kernel(x_prompt, x_sample, cache_attn_k, cache_attn_v, state_rwkv, state_lru, c, c_ctx,
           mod_w, mod_b, norm1_g, norm2_g, ev_w_in, ev_w_out, q_norm_g, k_norm_g,
           rwkv_mu, rwkv_w0, rwkv_w_up, rwkv_a0, rwkv_a_up, rwkv_g_up, rwkv_k_k, rwkv_k_a,
           rwkv_r_k, rwkv_ln_g, rwkv_ln_b, od_w_in, od_w_out, conv_w, conv_b,
           lru_wa, lru_ba, lru_wx, lru_bx, lru_lambda,
           router_w, router_b, exp_w1, exp_b1, exp_w2, exp_b2):
    bp, tp, d = x_prompt.shape
    bs, ts, _ = x_sample.shape
    depth = mod_w.shape[0]
    lay = Layout(bp, tp, bs, ts)
    assert bs < MOD_ROWS and tp % TOKEN_TILE == 0 and ts % TOKEN_TILE == 0 and d == D_MODEL
    n_p = lay.n_p

    x = jnp.concatenate([x_prompt.reshape(n_p, d), x_sample.reshape(bs * ts, d)], axis=0)
    cvec = jnp.zeros((MOD_ROWS, d), F32).at[:bs].set(c).at[bs].set(c_ctx)
    mods_all = modulation(cvec, mod_w, mod_b)
    rope = rope_tables(ts)
    n_le = depth * N_EXPERTS
    expert_prm = (exp_w1.reshape(n_le, d, -1), exp_b1.reshape(n_le, 1, -1),
                  exp_w2.reshape(n_le, -1, d), exp_b2.reshape(n_le, 1, d))

    new_k, new_v, new_rw, new_lru = [], [], [], []
    for l in range(depth):
        j = l // 2
        mods = mods_all[l]
        if l % 2 == 0:
            qkv, rw = norm_proj(lay, x, norm1_g[l], mods, 0, ev_w_in[j].astype(BF16),
                                (ATT_Q + 2 * ATT_KV, RWKV_COLS))
            att_p, k_norm = attention(qkv, 0, bp, tp, q_norm_g[j], k_norm_g[j])
            cache = (cache_attn_k[:, j].reshape(bs, -1, ATT_KV), cache_attn_v[:, j].reshape(bs, -1, ATT_KV))
            (att_s,) = attention(qkv, n_p, bs, ts, q_norm_g[j], k_norm_g[j], cache=cache, rope=rope)
            att = jnp.concatenate([att_p, att_s], axis=0)
            prm = (rwkv_mu[j], rwkv_w0[j], rwkv_w_up[j], rwkv_a0[j], rwkv_a_up[j], rwkv_g_up[j],
                   rwkv_k_k[j], rwkv_k_a[j], rwkv_r_k[j].reshape(-1))
            r, v, alpha, lw0, be0, kd0, lw1, be1, kd1, bonus, gate = rwkv_prep(lay, rw, prm)
            s_lat = jnp.swapaxes(state_rwkv[:, j], -1, -2).reshape(bs, 2 * RWKV_HEADS, RWKV_HD, RWKV_HD)
            s0 = jnp.concatenate([jnp.zeros((bp,) + s_lat.shape[1:], F32), s_lat], axis=0)
            o_f, o_b, s_fin = rwkv_chunks(lay, r, v, alpha, lw0, be0, kd0, lw1, be1, kd1, s0)
            x = even_out(lay, x, att, o_f, o_b, bonus, gate, rwkv_ln_g[j], rwkv_ln_b[j], ev_w_out[j], mods)
            new_k.append(k_norm.reshape(bp, tp, ATT_KV_HEADS, HEAD_DIM))
            new_v.append(qkv[:n_p, ATT_Q + ATT_KV:].reshape(bp, tp, ATT_KV_HEADS, HEAD_DIM))
            new_rw.append(jnp.swapaxes(s_fin[:bp].reshape(bp, 2, RWKV_HEADS, RWKV_HD, RWKV_HD), -1, -2))
        else:
            (proj,) = norm_proj(lay, x, norm1_g[l], mods, 0, od_w_in[j].astype(BF16), (2 * D_RNN,))
            gate, a0, b0, a1, b1 = lru_prep(lay, proj, conv_w[j], conv_b[j], lru_wa[j], lru_ba[j], lru_wx[j],
                                            lru_bx[j], lru_lambda[j])
            h0 = jnp.concatenate([jnp.zeros((bp, 2, D_RNN), F32), state_lru[:, j]], axis=0)
            hf, hb = lru_scan(lay, a0, b0, a1, b1, h0)
            x = odd_out(lay, x, gate, hf, hb, od_w_out[j], mods)
            hf_p = hf[:n_p].reshape(bp, tp, D_RNN)
            hb_p = hb[:n_p].reshape(bp, tp, D_RNN)
            new_lru.append(jnp.stack([hf_p[:, -1], hb_p[:, 0]], axis=1))
        x = moe_layer(lay, x, norm2_g[l], mods, router_w[l], router_b[l], l, *expert_prm)

    y_prompt = x[:n_p].reshape(bp, tp, d)
    y_sample = x[n_p:].reshape(bs, ts, d)
    return (y_prompt, y_sample, jnp.stack(new_k, axis=1), jnp.stack(new_v, axis=1), jnp.stack(new_rw, axis=1),
            jnp.stack(new_lru, axis=1))
```

```python
import functools
from typing import NamedTuple

import numpy as np
import jax
import jax.numpy as jnp
from jax import lax
from jax.experimental import pallas as pl
from jax.experimental.pallas import tpu as pltpu

F32 = jnp.float32
BF16 = jnp.bfloat16
I32 = jnp.int32
HIGHEST = lax.Precision.HIGHEST

D_MODEL = 1024
EPS = 1e-6
GRID_W = 64
ATT_HEADS = 8
ATT_KV_HEADS = 2
HEAD_DIM = 64
GQA_GROUP = ATT_HEADS // ATT_KV_HEADS
ATT_Q = ATT_HEADS * HEAD_DIM
ATT_KV = ATT_KV_HEADS * HEAD_DIM
ROPE_THETA = 10000.0
LOG2_E = 1.4426950408889634
RWKV_HEADS = 8
RWKV_HD = 64
RWKV_DIM = RWKV_HEADS * RWKV_HD
DECAY_RANK = 64
ICLR_RANK = 64
GATE_RANK = 128
RWKV_COLS = 3 * RWKV_DIM + 2 * DECAY_RANK + 2 * ICLR_RANK + GATE_RANK
GN_EPS = 64e-5
D_RNN = D_MODEL
LRU_BS = 64
LRU_GROUP = 256
CONV_W = 4
LRU_C = 8.0
N_EXPERTS = 32
TOP_K = 4
D_EXPERT = D_MODEL
SWIGLU_LIMIT = 7.0
SWIGLU_ALPHA = 1.702

TOKEN_TILE = 256
RWKV_CHUNK = 64
INV_BLOCK = 16
ATT_Q_TILE = 128
MOE_TILE = 256
HALO = 8
MOD_ROWS = 16
VMEM_LIMIT = 56 * 1024 * 1024


class Layout(NamedTuple):
    bp: int
    tp: int
    bs: int
    ts: int

    @property
    def n_p(self):
        return self.bp * self.tp

    @property
    def n(self):
        return self.bp * self.tp + self.bs * self.ts

    def tiles(self, tile):
        return self.n // tile

    def seq_of_tile(self, i, tile):
        npt = self.n_p // tile
        is_p = i < npt
        ii = jnp.where(is_p, i, i - npt)
        per = jnp.where(is_p, self.tp // tile, self.ts // tile)
        return is_p, ii // per, ii % per, per

    def mod_row(self, i, tile):
        is_p, seq, _, _ = self.seq_of_tile(i, tile)
        return jnp.where(is_p, self.bs, seq)


def _cparams(sem):
    return pltpu.CompilerParams(dimension_semantics=sem, vmem_limit_bytes=VMEM_LIMIT)


def _dot(a, b):
    return jnp.dot(a, b, preferred_element_type=F32)


def _dot_nt(a, b):
    return lax.dot_general(a, b, (((1,), (1,)), ((), ())), preferred_element_type=F32)


def _dot_tn(a, b):
    return lax.dot_general(a, b, (((0,), (0,)), ((), ())), preferred_element_type=F32)


def _split_dot(x, m01):
    hi = x.astype(BF16)
    lo = (x - hi.astype(F32)).astype(BF16)
    return _dot(hi, m01) + _dot(lo, m01)


def _split_dot_left(m01, x):
    hi = x.astype(BF16)
    lo = (x - hi.astype(F32)).astype(BF16)
    return _dot(m01, hi) + _dot(m01, lo)


def _sigmoid(x):
    return 1.0 / (1.0 + jnp.exp(-x))


def _block_ones(n, blk):
    idx = np.arange(n) // blk
    return jnp.asarray((idx[:, None] == idx[None, :]).astype(np.float32), dtype=BF16)


def _mod_kernel(c_ref, w_ref, b_ref, o_ref):
    c = c_ref[...]
    s = c * _sigmoid(c)
    o_ref[0] = jnp.dot(s, w_ref[0], preferred_element_type=F32, precision=HIGHEST) + b_ref[0]


def modulation(cvec, mod_w, mod_b):
    depth, d, six_d = mod_w.shape
    nchunk = six_d // d
    out = pl.pallas_call(
        _mod_kernel,
        grid=(depth, nchunk),
        in_specs=[
            pl.BlockSpec((MOD_ROWS, d), lambda l, k: (0, 0)),
            pl.BlockSpec((1, d, d), lambda l, k: (l, 0, k)),
            pl.BlockSpec((1, 1, d), lambda l, k: (l, 0, k)),
        ],
        out_specs=pl.BlockSpec((1, MOD_ROWS, d), lambda l, k: (l, 0, k)),
        out_shape=jax.ShapeDtypeStruct((depth, MOD_ROWS, six_d), F32),
        compiler_params=_cparams(("arbitrary", "arbitrary")),
        name="modulation",
    )(cvec, mod_w, mod_b.reshape(depth, 1, six_d))
    return out.reshape(depth, MOD_ROWS * nchunk, 1, d)


def _mod_spec(lay, k, tile):
    return pl.BlockSpec((1, 1, D_MODEL), lambda i, *_: (lay.mod_row(i, tile) * 6 + k, 0, 0))


def _norm_mod(x, g, shift, scale):
    ms = jnp.mean(x * x, axis=-1, keepdims=True)
    h = x * lax.rsqrt(ms + EPS) * g
    return h * (1.0 + scale) + shift


def _norm_proj_kernel(x_ref, g_ref, sh_ref, sc_ref, w_ref, *o_refs, splits):
    h = _norm_mod(x_ref[...], g_ref[...], sh_ref[0], sc_ref[0])
    y = _dot(h.astype(BF16), w_ref[...])
    off = 0
    for o_ref, n in zip(o_refs, splits):
        o_ref[...] = y[:, off:off + n]
        off += n


def norm_proj(lay, x, g, mods, k_shift, w_bf16, splits):
    n, d = x.shape
    n_out = w_bf16.shape[1]
    tile = TOKEN_TILE
    return pl.pallas_call(
        functools.partial(_norm_proj_kernel, splits=splits),
        grid=(n // tile,),
        in_specs=[
            pl.BlockSpec((tile, d), lambda i: (i, 0)),
            pl.BlockSpec((1, d), lambda i: (0, 0)),
            _mod_spec(lay, k_shift, tile),
            _mod_spec(lay, k_shift + 1, tile),
            pl.BlockSpec((d, n_out), lambda i: (0, 0)),
        ],
        out_specs=[pl.BlockSpec((tile, s), lambda i: (i, 0)) for s in splits],
        out_shape=[jax.ShapeDtypeStruct((n, s), F32) for s in splits],
        compiler_params=_cparams(("arbitrary",)),
        name="norm_proj",
    )(x, g.reshape(1, d), mods, mods, w_bf16)


def _head_norm(x, g, ones):
    ms = _split_dot(x * x, ones) * (1.0 / HEAD_DIM)
    return x * lax.rsqrt(ms + EPS) * g


def _rope(x, cos, sin_signed):
    n = x.shape[1]
    nxt = pltpu.roll(x, n - 1, 1)
    prv = pltpu.roll(x, 1, 1)
    lane = lax.broadcasted_iota(I32, x.shape, 1)
    swapped = jnp.where(lane % 2 == 0, nxt, prv)
    return x * cos + swapped * sin_signed


def _attn_kernel(*refs, t_len, n_ctx, rotary):
    if rotary:
        (qkv_ref, ck_ref, cv_ref, cos_ref, sin_ref, qg_ref, kg_ref, oq_ref, ok_ref,
         att_ref, k_scr, v_scr) = refs
    else:
        qkv_ref, qg_ref, kg_ref, oq_ref, ok_ref, att_ref, kn_ref, k_scr, v_scr = refs
    qi = pl.program_id(1)
    tq = ATT_Q_TILE

    @pl.when(qi == 0)
    def _prepare_keys():
        k = _head_norm(qkv_ref[:, ATT_Q:ATT_Q + ATT_KV], kg_ref[...], ok_ref[...])
        v = qkv_ref[:, ATT_Q + ATT_KV:ATT_Q + 2 * ATT_KV]
        if rotary:
            k = _rope(k, cos_ref[:, :ATT_KV], sin_ref[:, :ATT_KV])
            k_scr[0:n_ctx, :] = ck_ref[0].astype(BF16)
            v_scr[0:n_ctx, :] = cv_ref[0].astype(BF16)
        else:
            kn_ref[...] = k
        k_scr[n_ctx:n_ctx + t_len, :] = k.astype(BF16)
        v_scr[n_ctx:n_ctx + t_len, :] = v.astype(BF16)

    row0 = pl.multiple_of(qi * tq, tq)
    q = _head_norm(qkv_ref[pl.ds(row0, tq), 0:ATT_Q], qg_ref[...], oq_ref[...])
    if rotary:
        q = _rope(q, cos_ref[pl.ds(row0, tq), :], sin_ref[pl.ds(row0, tq), :])
    q = (q * (HEAD_DIM ** -0.5 * LOG2_E)).astype(BF16)
    outs = []
    for j in range(ATT_KV_HEADS):
        kj = k_scr[:, j * HEAD_DIM:(j + 1) * HEAD_DIM]
        vj = v_scr[:, j * HEAD_DIM:(j + 1) * HEAD_DIM]
        qs = jnp.concatenate(
            [q[:, (j * GQA_GROUP + g) * HEAD_DIM:(j * GQA_GROUP + g + 1) * HEAD_DIM] for g in range(GQA_GROUP)],
            axis=0)
        s = _dot_nt(qs, kj)
        p = jnp.exp2(s - jnp.max(s, axis=-1, keepdims=True))
        o = _dot(p.astype(BF16), vj) / jnp.sum(p, axis=-1, keepdims=True)
        outs.extend(o[g * tq:(g + 1) * tq] for g in range(GQA_GROUP))
    att_ref[...] = jnp.concatenate(outs, axis=1)


def attention(qkv, seq0, n_seq, t_len, q_g, k_g, cache=None, rope=None):
    rotary = cache is not None
    n_ctx = cache[0].shape[1] if rotary else 0
    blk0 = seq0 // t_len
    n_q = t_len // ATT_Q_TILE
    width = qkv.shape[1]
    qg = jnp.tile(q_g, ATT_HEADS).reshape(1, ATT_Q)
    kg = jnp.tile(k_g, ATT_KV_HEADS).reshape(1, ATT_KV)
    const = lambda shape: pl.BlockSpec(shape, lambda b, qi: (0,) * len(shape))
    in_specs = [pl.BlockSpec((t_len, width), lambda b, qi: (blk0 + b, 0))]
    args = [qkv]
    if rotary:
        in_specs += [pl.BlockSpec((1, n_ctx, ATT_KV), lambda b, qi: (b, 0, 0))] * 2
        in_specs += [const((t_len, ATT_Q))] * 2
        args += [cache[0], cache[1], rope[0], rope[1]]
    in_specs += [const((1, ATT_Q)), const((1, ATT_KV)), const((ATT_Q, ATT_Q)), const((ATT_KV, ATT_KV))]
    args += [qg, kg, _block_ones(ATT_Q, HEAD_DIM), _block_ones(ATT_KV, HEAD_DIM)]
    out_specs = [pl.BlockSpec((ATT_Q_TILE, ATT_Q), lambda b, qi: (b * n_q + qi, 0))]
    out_shape = [jax.ShapeDtypeStruct((n_seq * t_len, ATT_Q), F32)]
    if not rotary:
        out_specs.append(pl.BlockSpec((t_len, ATT_KV), lambda b, qi: (b, 0)))
        out_shape.append(jax.ShapeDtypeStruct((n_seq * t_len, ATT_KV), F32))
    return pl.pallas_call(
        functools.partial(_attn_kernel, t_len=t_len, n_ctx=n_ctx, rotary=rotary),
        grid=(n_seq, n_q),
        in_specs=in_specs,
        out_specs=out_specs,
        out_shape=out_shape,
        scratch_shapes=[pltpu.VMEM((n_ctx + t_len, ATT_KV), BF16), pltpu.VMEM((n_ctx + t_len, ATT_KV), BF16)],
        compiler_params=_cparams(("arbitrary", "arbitrary")),
        name="attention_latent" if rotary else "attention_context",
    )(*args)


def rope_tables(t_len):
    t = jnp.arange(t_len)
    pos = jnp.stack([t // GRID_W, t % GRID_W], axis=-1).astype(F32)
    n_freq = HEAD_DIM // 4
    inv = ROPE_THETA ** (-jnp.arange(n_freq, dtype=F32) / n_freq)
    ang = (pos[:, :, None] * inv).reshape(t_len, 2 * n_freq)
    cos = jnp.repeat(jnp.cos(ang), 2, axis=1)
    sin = jnp.repeat(jnp.sin(ang), 2, axis=1) * jnp.tile(jnp.asarray([-1.0, 1.0], F32), HEAD_DIM // 2)
    return jnp.tile(cos, (1, ATT_HEADS)), jnp.tile(sin, (1, ATT_HEADS))


def _shifted_rows(x, prev_row, next_row):
    m = x.shape[0]
    row = lax.broadcasted_iota(I32, x.shape, 0)
    prv = jnp.where(row == 0, prev_row, pltpu.roll(x, 1, 0))
    nxt = jnp.where(row == m - 1, next_row, pltpu.roll(x, m - 1, 0))
    return prv, nxt


def _rwkv_prep_kernel(x_ref, prev_ref, next_ref, mu_ref, kk_ref, ka_ref, rk_ref, w0_ref, wup_ref, a0_ref,
                      aup_ref, gup_ref, ones_ref,
                      r_ref, v_ref, al_ref, lw0_ref, be0_ref, kd0_ref, lw1_ref, be1_ref, kd1_ref, bonus_ref,
                      gate_ref, *, lay):
    i = pl.program_id(0)
    _, _, j, per = lay.seq_of_tile(i, TOKEN_TILE)
    x = x_ref[...]
    prev_row = jnp.where(j == 0, 0.0, prev_ref[HALO - 1:HALO, :])
    next_row = jnp.where(j == per - 1, 0.0, next_ref[0:1, :])
    prv, nxt = _shifted_rows(x, prev_row, next_row)
    u = x + mu_ref[...] * (0.5 * (prv + nxt) - x)

    dim = RWKV_DIM
    r, k, v = u[:, :dim], u[:, dim:2 * dim], u[:, 2 * dim:3 * dim]
    o = 3 * dim
    wd = u[:, o:o + 2 * DECAY_RANK]
    o += 2 * DECAY_RANK
    ad = u[:, o:o + 2 * ICLR_RANK]
    o += 2 * ICLR_RANK
    gd = u[:, o:o + GATE_RANK]

    ones = ones_ref[...]
    kk = k * kk_ref[...]
    norm = jnp.sqrt(_split_dot(kk * kk, ones))
    alpha = kk / jnp.maximum(norm, 1e-12)
    r_ref[...] = r
    v_ref[...] = v
    al_ref[...] = alpha

    tanh_wd = jnp.tanh(wd).astype(BF16)
    ad16 = ad.astype(BF16)
    kd_sum = None
    for d, (lw_ref, be_ref, kd_ref) in enumerate(((lw0_ref, be0_ref, kd0_ref), (lw1_ref, be1_ref, kd1_ref))):
        w_log = w0_ref[d:d + 1, :] + _dot(tanh_wd[:, d * DECAY_RANK:(d + 1) * DECAY_RANK], wup_ref[d])
        lw_ref[...] = -_sigmoid(w_log) * float(np.exp(-0.5))
        a = _sigmoid(a0_ref[d:d + 1, :] + _dot(ad16[:, d * ICLR_RANK:(d + 1) * ICLR_RANK], aup_ref[d]))
        kd = k * (1.0 + (a - 1.0) * ka_ref[...])
        be_ref[...] = alpha * a
        kd_ref[...] = kd
        kd_sum = kd if kd_sum is None else kd_sum + kd
    bonus_ref[...] = _split_dot(r * kd_sum * rk_ref[...], ones) * v
    gate_ref[...] = _dot(_sigmoid(gd).astype(BF16), gup_ref[...])


def rwkv_prep(lay, rw, prm):
    mu, w0, w_up, a0, a_up, g_up, k_k, k_a, r_k = prm
    n, cols = rw.shape
    tile = TOKEN_TILE
    hb = tile // HALO
    n_halo = n // HALO
    dim = RWKV_DIM
    const = lambda shape: pl.BlockSpec(shape, lambda i: (0,) * len(shape))
    row = lambda a: a.reshape(1, -1)
    out_spec = pl.BlockSpec((tile, dim), lambda i: (i, 0))
    return pl.pallas_call(
        functools.partial(_rwkv_prep_kernel, lay=lay),
        grid=(n // tile,),
        in_specs=[
            pl.BlockSpec((tile, cols), lambda i: (i, 0)),
            pl.BlockSpec((HALO, cols), lambda i: (jnp.maximum(i * hb - 1, 0), 0)),
            pl.BlockSpec((HALO, cols), lambda i: (jnp.minimum((i + 1) * hb, n_halo - 1), 0)),
            const((1, cols)), const((1, dim)), const((1, dim)), const((1, dim)),
            const((2, dim)), const((2, DECAY_RANK, dim)), const((2, dim)), const((2, ICLR_RANK, dim)),
            const((GATE_RANK, dim)), const((dim, dim)),
        ],
        out_specs=[out_spec] * 11,
        out_shape=[jax.ShapeDtypeStruct((n, dim), F32)] * 11,
        compiler_params=_cparams(("arbitrary",)),
        name="rwkv_prep",
    )(rw, rw, rw, row(mu), row(k_k), row(k_a), row(r_k), w0, w_up.astype(BF16), a0, a_up.astype(BF16),
      g_up.astype(BF16), _block_ones(dim, RWKV_HD))


def _each(fn, *lists):
    return [fn(*args) for args in zip(*lists)]


def _unit_triangular_inverses(l_mats, eye, diag_blocks):
    mm = lambda a, b: _dot(a.astype(BF16), b.astype(BF16))
    ld = [jnp.where(diag_blocks, l, 0.0) for l in l_mats]
    lo = _each(lambda l, d: l - d, l_mats, ld)
    x = [eye - d for d in ld]
    p = ld
    for _ in range(int(np.log2(INV_BLOCK)) - 1):
        p = _each(mm, p, p)
        x = _each(lambda xi, pi: xi + mm(xi, pi), x, p)
    nb = _each(mm, x, lo)
    y = [eye - m for m in nb]
    p = nb
    for _ in range(int(np.log2(RWKV_CHUNK // INV_BLOCK)) - 1):
        p = _each(mm, p, p)
        y = _each(lambda yi, pi: yi + mm(yi, pi), y, p)
    return _each(mm, y, x)


def _chunk_factors(r, alpha, lw, beta, kd, incl01):
    c_incl = _split_dot_left(incl01, lw)
    c_tot = jnp.sum(lw, axis=0, keepdims=True)
    inv_p = jnp.exp(-c_incl)
    to_end = jnp.exp(c_tot - c_incl)
    r_bar = r * jnp.exp(c_incl)
    b16 = lambda m: m.astype(BF16)
    return dict(a_bar=b16(alpha * jnp.exp(c_incl - lw)), r_bar=r_bar, r_bar16=b16(r_bar),
                b_bar=b16(beta * inv_p), k_bar=b16(kd * inv_p), b_til=b16(beta * to_end),
                k_til=b16(kd * to_end), p_tot=jnp.exp(c_tot))


def _rwkv_chunk_kernel(rf_ref, vf_ref, af_ref, lwf_ref, bef_ref, kdf_ref,
                       rb_ref, vb_ref, ab_ref, lwb_ref, beb_ref, kdb_ref, s0_ref,
                       of_ref, ob_ref, sfin_ref, h_scr, *, lay):
    s = pl.program_id(0)
    _, _, c, per = lay.seq_of_tile(s, RWKV_CHUNK)

    @pl.when(c == 0)
    def _load_state():
        h_scr[...] = s0_ref[0]

    n = RWKV_CHUNK
    hd = RWKV_HD
    row = lax.broadcasted_iota(I32, (n, n), 0)
    col = lax.broadcasted_iota(I32, (n, n), 1)
    eye = (row == col).astype(F32)
    diag_blocks = (row // INV_BLOCK) == (col // INV_BLOCK)
    b16 = lambda m: m.astype(BF16)
    mm = lambda a, b: _dot(b16(a), b16(b))

    fac, strict_m, incl_m, v16 = [], [], [], []
    for d, refs in enumerate(((rf_ref, vf_ref, af_ref, lwf_ref, bef_ref, kdf_ref),
                              (rb_ref, vb_ref, ab_ref, lwb_ref, beb_ref, kdb_ref))):
        r_ref, v_ref, a_ref, lw_ref, be_ref, kd_ref = refs
        strict = (col > row) if d else (col < row)
        incl = (col >= row) if d else (col <= row)
        f = _chunk_factors(r_ref[...], a_ref[...], lw_ref[...], be_ref[...], kd_ref[...],
                           jnp.where(incl, 1.0, 0.0).astype(BF16))
        v = b16(v_ref[...])
        for h in range(RWKV_HEADS):
            sl = slice(h * hd, (h + 1) * hd)
            fac.append({k: a[:, sl] for k, a in f.items()})
            strict_m.append(strict)
            incl_m.append(incl)
            v16.append(v[:, sl])

    gram = [_dot_nt(jnp.concatenate([f["a_bar"], f["r_bar16"]], axis=0),
                    jnp.concatenate([f["b_bar"], f["k_bar"]], axis=0)) for f in fac]
    l_mat = _each(lambda g, m: jnp.where(m, g[:n, :n], 0.0), gram, strict_m)
    a_k = _each(lambda g, m: jnp.where(m, g[:n, n:], 0.0), gram, strict_m)
    r_b = _each(lambda g, m: jnp.where(m, g[n:, :n], 0.0), gram, incl_m)
    r_k = _each(lambda g, m: jnp.where(m, g[n:, n:], 0.0), gram, incl_m)
    akv = _each(mm, a_k, v16)
    kt_v = _each(lambda f, v: _dot_tn(f["k_til"], v), fac, v16)
    rk_v = _each(mm, r_k, v16)
    t_inv = _unit_triangular_inverses(l_mat, eye, diag_blocks)
    mw = _each(lambda t, f, w: b16(mm(t, jnp.concatenate([f["a_bar"], b16(w)], axis=1))), t_inv, fac, akv)
    bt_mw = _each(lambda f, m: _dot_tn(f["b_til"], m), fac, mw)
    rb_mw = _each(mm, r_b, mw)
    h0 = [h_scr[u] for u in range(len(fac))]
    h016 = [b16(h) for h in h0]
    outs = _each(lambda f, rb, rk, h: mm(f["r_bar"] - rb[:, :hd], h) + rk - rb[:, hd:], fac, rb_mw, rk_v, h016)
    for u, (f, bt, kv, h, h16) in enumerate(zip(fac, bt_mw, kt_v, h0, h016)):
        decay_col = jnp.sum(eye * f["p_tot"], axis=1, keepdims=True)
        h_scr[u] = decay_col * h - mm(bt[:, :hd], h16) + kv - bt[:, hd:]
    of_ref[...] = jnp.concatenate(outs[:RWKV_HEADS], axis=1)
    ob_ref[...] = jnp.concatenate(outs[RWKV_HEADS:], axis=1)

    @pl.when(c == per - 1)
    def _store_state():
        sfin_ref[0] = h_scr[...]


def rwkv_chunks(lay, r, v, alpha, lw0, be0, kd0, lw1, be1, kd1, s0):
    n, dim = r.shape
    ch = RWKV_CHUNK
    n_seq = lay.bp + lay.bs

    def fwd(s):
        return (s, 0)

    def bwd(s):
        _, _, c, per = lay.seq_of_tile(s, ch)
        return (s - c + (per - 1 - c), 0)

    def seq(s):
        is_p, q, _, _ = lay.seq_of_tile(s, ch)
        return (jnp.where(is_p, q, lay.bp + q), 0, 0, 0)

    state_block = (1, 2 * RWKV_HEADS, RWKV_HD, RWKV_HD)
    return pl.pallas_call(
        functools.partial(_rwkv_chunk_kernel, lay=lay),
        grid=(n // ch,),
        in_specs=[pl.BlockSpec((ch, dim), fwd)] * 6 + [pl.BlockSpec((ch, dim), bwd)] * 6
        + [pl.BlockSpec(state_block, seq)],
        out_specs=[pl.BlockSpec((ch, dim), fwd), pl.BlockSpec((ch, dim), bwd), pl.BlockSpec(state_block, seq)],
        out_shape=[jax.ShapeDtypeStruct((n, dim), F32), jax.ShapeDtypeStruct((n, dim), F32),
                   jax.ShapeDtypeStruct((n_seq,) + state_block[1:], F32)],
        scratch_shapes=[pltpu.VMEM(state_block[1:], F32)],
        compiler_params=_cparams(("arbitrary",)),
        name="rwkv_chunks",
    )(r, v, alpha, lw0, be0, kd0, r, v, alpha, lw1, be1, kd1, s0)


def _even_out_kernel(x_ref, att_ref, of_ref, ob_ref, bonus_ref, gate_ref, lng_ref, lnb_ref, ones_ref,
                     wa_ref, wr_ref, g1_ref, o_ref):
    ones = ones_ref[...]
    osum = of_ref[...] + ob_ref[...]
    mean = _split_dot(osum, ones) * (1.0 / RWKV_HD)
    cen = osum - mean
    var = _split_dot(cen * cen, ones) * (1.0 / RWKV_HD)
    on = cen * lax.rsqrt(var + GN_EPS) * lng_ref[...] + lnb_ref[...]
    rw = (on + bonus_ref[...]) * gate_ref[...]
    y = _dot(att_ref[...].astype(BF16), wa_ref[...]) + _dot(rw.astype(BF16), wr_ref[...])
    o_ref[...] = x_ref[...] + g1_ref[0] * y


def even_out(lay, x, att, o_f, o_b, bonus, gate, ln_g, ln_b, w_out, mods):
    n, d = x.shape
    tile = TOKEN_TILE
    dim = RWKV_DIM
    const = lambda shape: pl.BlockSpec(shape, lambda i: (0,) * len(shape))
    tok = lambda w: pl.BlockSpec((tile, w), lambda i: (i, 0))
    w16 = w_out.astype(BF16)
    return pl.pallas_call(
        _even_out_kernel,
        grid=(n // tile,),
        in_specs=[tok(d), tok(ATT_Q), tok(dim), tok(dim), tok(dim), tok(dim),
                  const((1, dim)), const((1, dim)), const((dim, dim)),
                  const((ATT_Q, d)), const((dim, d)), _mod_spec(lay, 2, tile)],
        out_specs=tok(d),
        out_shape=jax.ShapeDtypeStruct((n, d), F32),
        compiler_params=_cparams(("arbitrary",)),
        name="even_out",
    )(x, att, o_f, o_b, bonus, gate, ln_g.reshape(1, dim), ln_b.reshape(1, dim), _block_ones(dim, RWKV_HD),
      w16[:ATT_Q], w16[ATT_Q:], mods)


def _gelu_tanh(x):
    return 0.5 * x * (1.0 + jnp.tanh(float(np.sqrt(2.0 / np.pi)) * (x + 0.044715 * (x * x * x))))


def _softplus(x):
    return jnp.maximum(x, 0.0) + jnp.log(1.0 + jnp.exp(-jnp.abs(x)))


def _lru_prep_kernel(x_ref, prev_ref, next_ref, cw_ref, cb_ref, wbd_ref, ba_ref, bx_ref, lam_ref,
                     gate_ref, a0_ref, b0_ref, a1_ref, b1_ref, *, lay):
    i = pl.program_id(0)
    _, _, j, per = lay.seq_of_tile(i, TOKEN_TILE)
    tile = TOKEN_TILE
    gate_ref[...] = _gelu_tanh(x_ref[:, :D_RNN])
    x = x_ref[:, D_RNN:]
    first = j == 0
    last = j == per - 1
    row = lax.broadcasted_iota(I32, x.shape, 0)
    p1 = jnp.where(first, 0.0, prev_ref[HALO - 1:HALO, :])
    p2 = jnp.where(first, 0.0, prev_ref[HALO - 2:HALO - 1, :])
    n1 = jnp.where(last, 0.0, next_ref[0:1, :])
    xm1 = jnp.where(row == 0, p1, pltpu.roll(x, 1, 0))
    xm2 = jnp.where(row == 0, p2, jnp.where(row == 1, p1, pltpu.roll(x, 2, 0)))
    xp1 = jnp.where(row == tile - 1, n1, pltpu.roll(x, tile - 1, 0))
    u = cb_ref[...] + xm2 * cw_ref[0:1, :] + xm1 * cw_ref[1:2, :] + x * cw_ref[2:3, :] + xp1 * cw_ref[3:4, :]

    u16 = u.astype(BF16)
    n_grp = D_RNN // LRU_GROUP
    z = [_dot(u16[:, g * LRU_GROUP:(g + 1) * LRU_GROUP], wbd_ref[g]) for g in range(n_grp)]
    pick = lambda m: jnp.concatenate([zg[:, m * LRU_GROUP:(m + 1) * LRU_GROUP] for zg in z], axis=1)
    for d, (a_ref, b_ref) in enumerate(((a0_ref, b0_ref), (a1_ref, b1_ref))):
        ga = _sigmoid(pick(2 * d) + ba_ref[d:d + 1, :])
        gx = _sigmoid(pick(2 * d + 1) + bx_ref[d:d + 1, :])
        log_a = -LRU_C * ga * _softplus(-lam_ref[d:d + 1, :])
        a_ref[...] = jnp.exp(log_a)
        b_ref[...] = jnp.sqrt(1.0 - jnp.exp(2.0 * log_a)) * gx * u


def lru_prep(lay, proj, conv_w, conv_b, wa, ba, wx, bx, lam):
    n, cols = proj.shape
    tile = TOKEN_TILE
    hb = tile // HALO
    n_halo = n // HALO
    dr = D_RNN
    per = LRU_GROUP // LRU_BS
    n_grp = dr // LRU_GROUP

    def block_diag(w):
        w = w.reshape(n_grp, per, LRU_BS, LRU_BS)
        eye = jnp.eye(per, dtype=w.dtype)
        return jnp.einsum("gpcd,pq->gpcqd", w, eye).reshape(n_grp, LRU_GROUP, LRU_GROUP)

    wbd = jnp.concatenate([block_diag(wa[0]), block_diag(wx[0]), block_diag(wa[1]), block_diag(wx[1])],
                          axis=2).astype(BF16)
    const = lambda shape: pl.BlockSpec(shape, lambda i: (0,) * len(shape))
    out_spec = pl.BlockSpec((tile, dr), lambda i: (i, 0))
    return pl.pallas_call(
        functools.partial(_lru_prep_kernel, lay=lay),
        grid=(n // tile,),
        in_specs=[
            pl.BlockSpec((tile, cols), lambda i: (i, 0)),
            pl.BlockSpec((HALO, dr), lambda i: (jnp.maximum(i * hb - 1, 0), 1)),
            pl.BlockSpec((HALO, dr), lambda i: (jnp.minimum((i + 1) * hb, n_halo - 1), 1)),
            const((CONV_W, dr)), const((1, dr)), const((n_grp, LRU_GROUP, 4 * LRU_GROUP)),
            const((2, dr)), const((2, dr)), const((2, dr)),
        ],
        out_specs=[out_spec] * 5,
        out_shape=[jax.ShapeDtypeStruct((n, dr), F32)] * 5,
        compiler_params=_cparams(("arbitrary",)),
        name="lru_prep",
    )(proj, proj, proj, conv_w, conv_b.reshape(1, dr), wbd, ba, bx, lam)


def _lru_scan_kernel(af_ref, bf_ref, ab_ref, bb_ref, h0_ref, hf_ref, hb_ref, carry, *, lay):
    i = pl.program_id(0)
    _, _, j, _ = lay.seq_of_tile(i, TOKEN_TILE)
    tile = TOKEN_TILE

    @pl.when(j == 0)
    def _load_state():
        carry[...] = h0_ref[0]

    def step(t, hs):
        hf, hb = hs
        tb = tile - 1 - t
        hf = af_ref[pl.ds(t, 1), :] * hf + bf_ref[pl.ds(t, 1), :]
        hb = ab_ref[pl.ds(tb, 1), :] * hb + bb_ref[pl.ds(tb, 1), :]
        hf_ref[pl.ds(t, 1), :] = hf
        hb_ref[pl.ds(tb, 1), :] = hb
        return hf, hb

    hf, hb = lax.fori_loop(0, tile, step, (carry[0:1, :], carry[1:2, :]))
    carry[0:1, :] = hf
    carry[1:2, :] = hb


def lru_scan(lay, a0, b0, a1, b1, h0):
    n, dr = a0.shape
    tile = TOKEN_TILE

    def fwd(i):
        return (i, 0)

    def bwd(i):
        _, _, j, per = lay.seq_of_tile(i, tile)
        return (i - j + (per - 1 - j), 0)

    def seq(i):
        is_p, q, _, _ = lay.seq_of_tile(i, tile)
        return (jnp.where(is_p, q, lay.bp + q), 0, 0)

    return pl.pallas_call(
        functools.partial(_lru_scan_kernel, lay=lay),
        grid=(n // tile,),
        in_specs=[pl.BlockSpec((tile, dr), fwd)] * 2 + [pl.BlockSpec((tile, dr), bwd)] * 2
        + [pl.BlockSpec((1, 2, dr), seq)],
        out_specs=[pl.BlockSpec((tile, dr), fwd), pl.BlockSpec((tile, dr), bwd)],
        out_shape=[jax.ShapeDtypeStruct((n, dr), F32)] * 2,
        scratch_shapes=[pltpu.VMEM((2, dr), F32)],
        compiler_params=_cparams(("arbitrary",)),
        name="lru_scan",
    )(a0, b0, a1, b1, h0)


def _odd_out_kernel(x_ref, gate_ref, hf_ref, hb_ref, w_ref, g1_ref, o_ref):
    y = _dot((gate_ref[...] * (hf_ref[...] + hb_ref[...])).astype(BF16), w_ref[...])
    o_ref[...] = x_ref[...] + g1_ref[0] * y


def odd_out(lay, x, gate, hf, hb, w_out, mods):
    n, d = x.shape
    tile = TOKEN_TILE
    tok = lambda w: pl.BlockSpec((tile, w), lambda i: (i, 0))
    return pl.pallas_call(
        _odd_out_kernel,
        grid=(n // tile,),
        in_specs=[tok(d), tok(D_RNN), tok(D_RNN), tok(D_RNN), pl.BlockSpec((D_RNN, d), lambda i: (0, 0)),
                  _mod_spec(lay, 2, tile)],
        out_specs=tok(d),
        out_shape=jax.ShapeDtypeStruct((n, d), F32),
        compiler_params=_cparams(("arbitrary",)),
        name="odd_out",
    )(x, gate, hf, hb, w_out.astype(BF16), mods)


def _pack_bf16_pairs(x):
    n = x.shape[1] // 2
    hi = pltpu.bitcast(x[:, :n].astype(BF16).astype(F32), jnp.uint32)
    lo = pltpu.bitcast(x[:, n:].astype(BF16).astype(F32), jnp.uint32)
    return hi | (lo >> 16)


def _unpack_bf16_pairs(u):
    a = pltpu.bitcast(u & jnp.uint32(0xFFFF0000), F32)
    b = pltpu.bitcast(u << 16, F32)
    return jnp.concatenate([a, b], axis=1).astype(BF16)


def _router_kernel(x_ref, g_ref, sh_ref, sc_ref, rw_ref, rb_ref, before_ref,
                   h_ref, idx_ref, gate_ref, rank_ref, count_ref, cnt_scr):
    @pl.when(pl.program_id(0) == 0)
    def _reset():
        cnt_scr[...] = jnp.zeros_like(cnt_scr)

    h = _norm_mod(x_ref[...], g_ref[...], sh_ref[0], sc_ref[0])
    h_ref[...] = _pack_bf16_pairs(h)
    logits = lax.dot_general(rw_ref[...], h, (((1,), (1,)), ((), ())), preferred_element_type=F32,
                             precision=HIGHEST) + rb_ref[...]
    e_id = lax.broadcasted_iota(I32, logits.shape, 0)
    vals, ids, hots = [], [], []
    for _ in range(TOP_K):
        m = jnp.max(logits, axis=0, keepdims=True)
        pick = jnp.min(jnp.where(logits == m, e_id, N_EXPERTS), axis=0, keepdims=True)
        hot = e_id == pick
        vals.append(m)
        ids.append(pick)
        hots.append(jnp.where(hot, 1.0, 0.0))
        logits = jnp.where(hot, -jnp.inf, logits)
    top = jnp.concatenate(vals, axis=0)
    p = jnp.exp(top - top[0:1, :])
    gate_ref[...] = p / jnp.sum(p, axis=0, keepdims=True)
    idx_ref[...] = jnp.concatenate(ids, axis=0)
    hot_all = hots[0] + hots[1] + hots[2] + hots[3]
    seen = cnt_scr[:, 0:1] + _dot(hot_all.astype(BF16), before_ref[...])
    rank_ref[...] = jnp.concatenate([jnp.sum(hk * seen, axis=0, keepdims=True) for hk in hots],
                                    axis=0).astype(I32)
    cnt_scr[...] = cnt_scr[...] + jnp.sum(hot_all, axis=1, keepdims=True)
    count_ref[...] = cnt_scr[...]


def router(lay, x, g, mods, rw, rb):
    n, d = x.shape
    tile = TOKEN_TILE
    before = jnp.asarray(np.triu(np.ones((tile, tile), np.float32), 1), dtype=BF16)
    per_tok = pl.BlockSpec((TOP_K, tile), lambda i: (0, i))
    return pl.pallas_call(
        _router_kernel,
        grid=(n // tile,),
        in_specs=[pl.BlockSpec((tile, d), lambda i: (i, 0)), pl.BlockSpec((1, d), lambda i: (0, 0)),
                  _mod_spec(lay, 3, tile), _mod_spec(lay, 4, tile),
                  pl.BlockSpec((N_EXPERTS, d), lambda i: (0, 0)), pl.BlockSpec((N_EXPERTS, 1), lambda i: (0, 0)),
                  pl.BlockSpec((tile, tile), lambda i: (0, 0))],
        out_specs=[pl.BlockSpec((tile, d // 2), lambda i: (i, 0)), per_tok, per_tok, per_tok,
                   pl.BlockSpec((N_EXPERTS, 128), lambda i: (0, 0))],
        out_shape=[jax.ShapeDtypeStruct((n, d // 2), jnp.uint32), jax.ShapeDtypeStruct((TOP_K, n), I32),
                   jax.ShapeDtypeStruct((TOP_K, n), F32), jax.ShapeDtypeStruct((TOP_K, n), I32),
                   jax.ShapeDtypeStruct((N_EXPERTS, 128), F32)],
        scratch_shapes=[pltpu.VMEM((N_EXPERTS, 128), F32)],
        compiler_params=_cparams(("arbitrary",)),
        name="router",
    )(x, g.reshape(1, d), mods, mods, rw.T, rb.reshape(N_EXPERTS, 1), before)


def _row_copy(src_hbm, dst_vmem, sem, src_row, dst_row):
    return pltpu.make_async_copy(src_hbm.at[pl.ds(src_row, 1)], dst_vmem.at[pl.ds(dst_row, 1)], sem)


def _gather_rows_kernel(idx_ref, src_ref, o_ref):
    base = pl.program_id(0) * MOE_TILE

    def copy_row(r, _):
        o_ref[pl.ds(r, 1), :] = src_ref[pl.ds(idx_ref[base + r], 1), :]
        return 0

    lax.fori_loop(0, MOE_TILE, copy_row, 0, unroll=8)


def gather_rows(src, idx):
    n_out = idx.shape[0]
    n, d = src.shape
    return pl.pallas_call(
        _gather_rows_kernel,
        grid_spec=pltpu.PrefetchScalarGridSpec(
            num_scalar_prefetch=1,
            grid=(n_out // MOE_TILE,),
            in_specs=[pl.BlockSpec((n, d), lambda i, idx: (0, 0), pipeline_mode=pl.Buffered(1))],
            out_specs=pl.BlockSpec((MOE_TILE, d), lambda i, idx: (i, 0)),
        ),
        out_shape=jax.ShapeDtypeStruct((n_out, d), src.dtype),
        compiler_params=_cparams(("arbitrary",)),
        name="moe_gather",
    )(idx, src)


def _expert_weight_copies(w1_hbm, w2_hbm, w1_buf, w2_buf, sems, row, slot):
    return (pltpu.make_async_copy(w1_hbm.at[row], w1_buf.at[slot], sems.at[slot, 0]),
            pltpu.make_async_copy(w2_hbm.at[row], w2_buf.at[slot], sems.at[slot, 1]))


def _expert_kernel(blk_e_ref, first_ref, slot_ref, next_e_ref, n_used_ref, x_ref, w1_hbm, b1_ref, w2_hbm, b2_ref,
                   o_ref, w1_buf, w2_buf, w1_scr, w2_scr, sems, *, layer):
    i = pl.program_id(0)
    used = i < n_used_ref[0]
    copies = functools.partial(_expert_weight_copies, w1_hbm, w2_hbm, w1_buf, w2_buf, sems)

    @pl.when(jnp.logical_and(used, first_ref[i] == 1))
    def _switch_expert():
        slot = slot_ref[i]

        @pl.when(i == 0)
        def _fetch_first():
            for cp in copies(layer * N_EXPERTS + blk_e_ref[0], 0):
                cp.start()

        nxt = next_e_ref[i]

        @pl.when(nxt >= 0)
        def _prefetch_next():
            for cp in copies(layer * N_EXPERTS + nxt, 1 - slot):
                cp.start()

        for cp in copies(0, slot):
            cp.wait()
        w1_scr[...] = w1_buf[slot].astype(BF16)
        w2_scr[...] = w2_buf[slot].astype(BF16)

    @pl.when(used)
    def _compute():
        hb = _dot(_unpack_bf16_pairs(x_ref[...]), w1_scr[...]) + b1_ref[0]
        glu = jnp.minimum(hb[:, :D_EXPERT], SWIGLU_LIMIT)
        lin = jnp.clip(hb[:, D_EXPERT:], -SWIGLU_LIMIT, SWIGLU_LIMIT)
        act = glu * _sigmoid(SWIGLU_ALPHA * glu) * (lin + 1.0)
        o_ref[...] = _dot(act.astype(BF16), w2_scr[...]) + b2_ref[0]

    @pl.when(jnp.logical_not(used))
    def _clear():
        o_ref[...] = jnp.zeros_like(o_ref)


def experts(x_sorted, blk_expert, count, n_used, layer, w1, b1, w2, b2):
    n_rows = x_sorted.shape[0]
    n_blk = n_rows // MOE_TILE
    d, de2 = w1.shape[1:]
    de = w2.shape[1]
    first = jnp.concatenate([jnp.ones((1,), I32), (blk_expert[1:] != blk_expert[:-1]).astype(I32)])
    slot = (jnp.cumsum(first) - 1) % 2
    e_id = jnp.arange(N_EXPERTS, dtype=I32)
    later_present = jnp.logical_and(e_id[None, :] > e_id[:, None], count[None, :] > 0)
    next_present = jnp.min(jnp.where(later_present, e_id[None, :], N_EXPERTS), axis=1)
    next_e = jnp.where(next_present < N_EXPERTS, next_present, -1)[blk_expert]
    bmap = lambda i, be, *_: (layer * N_EXPERTS + be[i], 0, 0)
    return pl.pallas_call(
        functools.partial(_expert_kernel, layer=layer),
        grid_spec=pltpu.PrefetchScalarGridSpec(
            num_scalar_prefetch=5,
            grid=(n_blk,),
            in_specs=[pl.BlockSpec((MOE_TILE, d // 2), lambda i, *_: (i, 0)),
                      pl.BlockSpec(memory_space=pl.ANY), pl.BlockSpec((1, 1, de2), bmap),
                      pl.BlockSpec(memory_space=pl.ANY), pl.BlockSpec((1, 1, d), bmap)],
            out_specs=pl.BlockSpec((MOE_TILE, d), lambda i, *_: (i, 0)),
            scratch_shapes=[pltpu.VMEM((2, d, de2), F32), pltpu.VMEM((2, de, d), F32),
                            pltpu.VMEM((d, de2), BF16), pltpu.VMEM((de, d), BF16),
                            pltpu.SemaphoreType.DMA((2, 2))],
        ),
        out_shape=jax.ShapeDtypeStruct((n_rows, d), F32),
        compiler_params=_cparams(("arbitrary",)),
        name="moe_experts",
    )(blk_expert, first, slot.astype(I32), next_e.astype(I32), n_used, x_sorted, w1, b1, w2, b2)


def _combine_kernel(dest_ref, y_hbm, x_ref, gate_ref, g2_ref, o_ref, buf, sems, *, n_tok):
    tile = TOKEN_TILE
    i = pl.program_id(0)
    slot = i % 2

    def issue(step, dst_slot):
        def body(r, _):
            for k in range(TOP_K):
                _row_copy(y_hbm, buf.at[dst_slot, k], sems.at[dst_slot],
                          dest_ref[k * n_tok + step * tile + r], r).start(priority=k % 2)
            return 0

        lax.fori_loop(0, tile, body, 0, unroll=2)

    @pl.when(i == 0)
    def _first_tile():
        issue(0, 0)

    @pl.when(i + 1 < pl.num_programs(0))
    def _next_tile():
        issue(i + 1, 1 - slot)

    def wait(r, _):
        for k in range(TOP_K):
            _row_copy(y_hbm, buf.at[slot, k], sems.at[slot], 0, r).wait()
        return 0

    lax.fori_loop(0, tile, wait, 0, unroll=8)
    gate = gate_ref[...]
    acc = buf[slot, 0] * gate[:, 0:1]
    for k in range(1, TOP_K):
        acc = acc + buf[slot, k] * gate[:, k:k + 1]
    o_ref[...] = x_ref[...] + g2_ref[0] * acc


def combine(lay, x, y_sorted, dest, gate, mods):
    n, d = x.shape
    tile = TOKEN_TILE
    return pl.pallas_call(
        functools.partial(_combine_kernel, n_tok=n),
        grid_spec=pltpu.PrefetchScalarGridSpec(
            num_scalar_prefetch=1,
            grid=(n // tile,),
            in_specs=[pl.BlockSpec(memory_space=pl.ANY),
                      pl.BlockSpec((tile, d), lambda i, dest: (i, 0)),
                      pl.BlockSpec((tile, TOP_K), lambda i, dest: (i, 0)),
                      _mod_spec(lay, 5, tile)],
            out_specs=pl.BlockSpec((tile, d), lambda i, dest: (i, 0)),
            scratch_shapes=[pltpu.VMEM((2, TOP_K, tile, d), F32), pltpu.SemaphoreType.DMA((2,))],
        ),
        out_shape=jax.ShapeDtypeStruct((n, d), F32),
        compiler_params=_cparams(("arbitrary",)),
        name="moe_combine",
    )(dest, y_sorted, x, gate, mods)


def moe_layer(lay, x, g, mods, rw, rb, layer, w1, b1, w2, b2):
    n, d = x.shape
    h, idx_t, gate_t, rank_t, count = router(lay, x, g, mods, rw, rb)
    expert = idx_t.reshape(-1)
    rank = rank_t.reshape(-1)
    n_asg = expert.shape[0]
    count = count[:, 0].astype(I32)
    padded = (count + MOE_TILE - 1) // MOE_TILE * MOE_TILE
    pend = jnp.cumsum(padded)
    dest = ((pend - padded)[expert] + rank).astype(I32)
    n_blk = n_asg // MOE_TILE + N_EXPERTS
    src_tok = jnp.zeros((n_blk * MOE_TILE,), I32).at[dest].set(jnp.arange(n_asg, dtype=I32) % n)
    blk_start = jnp.arange(n_blk, dtype=I32) * MOE_TILE
    blk_expert = jnp.minimum(jnp.sum((pend[None, :] <= blk_start[:, None]).astype(I32), axis=1), N_EXPERTS - 1)
    n_used = (pend[-1] // MOE_TILE).astype(I32).reshape(1)
    x_sorted = gather_rows(h, src_tok)
    y_sorted = experts(x_sorted, blk_expert, count, n_used, layer, w1, b1, w2, b2)
    return combine(lay, x, y_sorted, dest, gate_t.T, mods)


def kernel(x_prompt, x_sample, cache_attn_k, cache_attn_v, state_rwkv, state_lru, c, c_ctx,
           mod_w, mod_b, norm1_g, norm2_g, ev_w_in, ev_w_out, q_norm_g, k_norm_g,
           rwkv_mu, rwkv_w0, rwkv_w_up, rwkv_a0, rwkv_a_up, rwkv_g_up, rwkv_k_k, rwkv_k_a,
           rwkv_r_k, rwkv_ln_g, rwkv_ln_b, od_w_in, od_w_out, conv_w, conv_b,
           lru_wa, lru_ba, lru_wx, lru_bx, lru_lambda,
           router_w, router_b, exp_w1, exp_b1, exp_w2, exp_b2):
    bp, tp, d = x_prompt.shape
    bs, ts, _ = x_sample.shape
    depth = mod_w.shape[0]
    lay = Layout(bp, tp, bs, ts)
    assert bs < MOD_ROWS and tp % TOKEN_TILE == 0 and ts % TOKEN_TILE == 0 and d == D_MODEL
    n_p = lay.n_p

    x = jnp.concatenate([x_prompt.reshape(n_p, d), x_sample.reshape(bs * ts, d)], axis=0)
    cvec = jnp.zeros((MOD_ROWS, d), F32).at[:bs].set(c).at[bs].set(c_ctx)
    mods_all = modulation(cvec, mod_w, mod_b)
    rope = rope_tables(ts)
    n_le = depth * N_EXPERTS
    expert_prm = (exp_w1.reshape(n_le, d, -1), exp_b1.reshape(n_le, 1, -1),
                  exp_w2.reshape(n_le, -1, d), exp_b2.reshape(n_le, 1, d))

    new_k, new_v, new_rw, new_lru = [], [], [], []
    for l in range(depth):
        j = l // 2
        mods = mods_all[l]
        if l % 2 == 0:
            qkv, rw = norm_proj(lay, x, norm1_g[l], mods, 0, ev_w_in[j].astype(BF16),
                                (ATT_Q + 2 * ATT_KV, RWKV_COLS))
            att_p, k_norm = attention(qkv, 0, bp, tp, q_norm_g[j], k_norm_g[j])
            cache = (cache_attn_k[:, j].reshape(bs, -1, ATT_KV), cache_attn_v[:, j].reshape(bs, -1, ATT_KV))
            (att_s,) = attention(qkv, n_p, bs, ts, q_norm_g[j], k_norm_g[j], cache=cache, rope=rope)
            att = jnp.concatenate([att_p, att_s], axis=0)
            prm = (rwkv_mu[j], rwkv_w0[j], rwkv_w_up[j], rwkv_a0[j], rwkv_a_up[j], rwkv_g_up[j],
                   rwkv_k_k[j], rwkv_k_a[j], rwkv_r_k[j].reshape(-1))
            r, v, alpha, lw0, be0, kd0, lw1, be1, kd1, bonus, gate = rwkv_prep(lay, rw, prm)
            s_lat = jnp.swapaxes(state_rwkv[:, j], -1, -2).reshape(bs, 2 * RWKV_HEADS, RWKV_HD, RWKV_HD)
            s0 = jnp.concatenate([jnp.zeros((bp,) + s_lat.shape[1:], F32), s_lat], axis=0)
            o_f, o_b, s_fin = rwkv_chunks(lay, r, v, alpha, lw0, be0, kd0, lw1, be1, kd1, s0)
            x = even_out(lay, x, att, o_f, o_b, bonus, gate, rwkv_ln_g[j], rwkv_ln_b[j], ev_w_out[j], mods)
            new_k.append(k_norm.reshape(bp, tp, ATT_KV_HEADS, HEAD_DIM))
            new_v.append(qkv[:n_p, ATT_Q + ATT_KV:].reshape(bp, tp, ATT_KV_HEADS, HEAD_DIM))
            new_rw.append(jnp.swapaxes(s_fin[:bp].reshape(bp, 2, RWKV_HEADS, RWKV_HD, RWKV_HD), -1, -2))
        else:
            (proj,) = norm_proj(lay, x, norm1_g[l], mods, 0, od_w_in[j].astype(BF16), (2 * D_RNN,))
            gate, a0, b0, a1, b1 = lru_prep(lay, proj, conv_w[j], conv_b[j], lru_wa[j], lru_ba[j], lru_wx[j],
                                            lru_bx[j], lru_lambda[j])
            h0 = jnp.concatenate([jnp.zeros((bp, 2, D_RNN), F32), state_lru[:, j]], axis=0)
            hf, hb = lru_scan(lay, a0, b0, a1, b1, h0)
            x = odd_out(lay, x, gate, hf, hb, od_w_out[j], mods)
            hf_p = hf[:n_p].reshape(bp, tp, D_RNN)
            hb_p = hb[:n_p].reshape(bp, tp, D_RNN)
            new_lru.append(jnp.stack([hf_p[:, -1], hb_p[:, 0]], axis=1))
        x = moe_layer(lay, x, norm2_g[l], mods, router_w[l], router_b[l], l, *expert_prm)

    y_prompt = x[:n_p].reshape(bp, tp, d)
    y_sample = x[n_p:].reshape(bs, ts, d)
    return (y_prompt, y_sample, jnp.stack(new_k, axis=1), jnp.stack(new_v, axis=1), jnp.stack(new_rw, axis=1),
            jnp.stack(new_lru, axis=1))
```

```python
import functools
from typing import NamedTuple

import numpy as np
import jax
import jax.numpy as jnp
from jax import lax
from jax.experimental import pallas as pl
from jax.experimental.pallas import tpu as pltpu

F32 = jnp.float32
BF16 = jnp.bfloat16
I32 = jnp.int32
HIGHEST = lax.Precision.HIGHEST

D_MODEL = 1024
EPS = 1e-6
GRID_W = 64
ATT_HEADS = 8
ATT_KV_HEADS = 2
HEAD_DIM = 64
GQA_GROUP = ATT_HEADS // ATT_KV_HEADS
ATT_Q = ATT_HEADS * HEAD_DIM
ATT_KV = ATT_KV_HEADS * HEAD_DIM
ROPE_THETA = 10000.0
LOG2_E = 1.4426950408889634
RWKV_HEADS = 8
RWKV_HD = 64
RWKV_DIM = RWKV_HEADS * RWKV_HD
DECAY_RANK = 64
ICLR_RANK = 64
GATE_RANK = 128
RWKV_COLS = 3 * RWKV_DIM + 2 * DECAY_RANK + 2 * ICLR_RANK + GATE_RANK
GN_EPS = 64e-5
D_RNN = D_MODEL
LRU_BS = 64
LRU_GROUP = 256
CONV_W = 4
LRU_C = 8.0
N_EXPERTS = 32
TOP_K = 4
D_EXPERT = D_MODEL
SWIGLU_LIMIT = 7.0
SWIGLU_ALPHA = 1.702

TOKEN_TILE = 256
RWKV_CHUNK = 64
RWKV_STEP_CHUNKS = 4
INV_BLOCK = 16
ATT_Q_TILE = 128
MOE_TILE = 256
HALO = 8
MOD_ROWS = 16
VMEM_LIMIT = 56 * 1024 * 1024


class Layout(NamedTuple):
    bp: int
    tp: int
    bs: int
    ts: int

    @property
    def n_p(self):
        return self.bp * self.tp

    @property
    def n(self):
        return self.bp * self.tp + self.bs * self.ts

    def tiles(self, tile):
        return self.n // tile

    def seq_of_tile(self, i, tile):
        npt = self.n_p // tile
        is_p = i < npt
        ii = jnp.where(is_p, i, i - npt)
        per = jnp.where(is_p, self.tp // tile, self.ts // tile)
        return is_p, ii // per, ii % per, per

    def mod_row(self, i, tile):
        is_p, seq, _, _ = self.seq_of_tile(i, tile)
        return jnp.where(is_p, self.bs, seq)


def _cparams(sem):
    return pltpu.CompilerParams(dimension_semantics=sem, vmem_limit_bytes=VMEM_LIMIT)


def _dot(a, b):
    return jnp.dot(a, b, preferred_element_type=F32)


def _dot_nt(a, b):
    return lax.dot_general(a, b, (((1,), (1,)), ((), ())), preferred_element_type=F32)


def _dot_tn(a, b):
    return lax.dot_general(a, b, (((0,), (0,)), ((), ())), preferred_element_type=F32)


def _split_dot(x, m01):
    hi = x.astype(BF16)
    lo = (x - hi.astype(F32)).astype(BF16)
    return _dot(hi, m01) + _dot(lo, m01)


def _split_dot_left(m01, x):
    hi = x.astype(BF16)
    lo = (x - hi.astype(F32)).astype(BF16)
    return _dot(m01, hi) + _dot(m01, lo)


def _sigmoid(x):
    return 1.0 / (1.0 + jnp.exp(-x))


def _block_ones(n, blk):
    idx = np.arange(n) // blk
    return jnp.asarray((idx[:, None] == idx[None, :]).astype(np.float32), dtype=BF16)


def _mod_kernel(c_ref, w_ref, b_ref, o_ref):
    c = c_ref[...]
    s = c * _sigmoid(c)
    o_ref[0] = jnp.dot(s, w_ref[0], preferred_element_type=F32, precision=HIGHEST) + b_ref[0]


def modulation(cvec, mod_w, mod_b):
    depth, d, six_d = mod_w.shape
    nchunk = six_d // d
    out = pl.pallas_call(
        _mod_kernel,
        grid=(depth, nchunk),
        in_specs=[
            pl.BlockSpec((MOD_ROWS, d), lambda l, k: (0, 0)),
            pl.BlockSpec((1, d, d), lambda l, k: (l, 0, k)),
            pl.BlockSpec((1, 1, d), lambda l, k: (l, 0, k)),
        ],
        out_specs=pl.BlockSpec((1, MOD_ROWS, d), lambda l, k: (l, 0, k)),
        out_shape=jax.ShapeDtypeStruct((depth, MOD_ROWS, six_d), F32),
        compiler_params=_cparams(("arbitrary", "arbitrary")),
        name="modulation",
    )(cvec, mod_w, mod_b.reshape(depth, 1, six_d))
    return out.reshape(depth, MOD_ROWS * nchunk, 1, d)


def _mod_spec(lay, k, tile):
    return pl.BlockSpec((1, 1, D_MODEL), lambda i, *_: (lay.mod_row(i, tile) * 6 + k, 0, 0))


def _norm_mod(x, g, shift, scale):
    ms = jnp.mean(x * x, axis=-1, keepdims=True)
    h = x * lax.rsqrt(ms + EPS) * g
    return h * (1.0 + scale) + shift


def _norm_proj_kernel(x_ref, g_ref, sh_ref, sc_ref, w_ref, *o_refs, splits):
    h = _norm_mod(x_ref[...], g_ref[...], sh_ref[0], sc_ref[0])
    y = _dot(h.astype(BF16), w_ref[...])
    off = 0
    for o_ref, n in zip(o_refs, splits):
        o_ref[...] = y[:, off:off + n]
        off += n


def norm_proj(lay, x, g, mods, k_shift, w_bf16, splits):
    n, d = x.shape
    n_out = w_bf16.shape[1]
    tile = TOKEN_TILE
    return pl.pallas_call(
        functools.partial(_norm_proj_kernel, splits=splits),
        grid=(n // tile,),
        in_specs=[
            pl.BlockSpec((tile, d), lambda i: (i, 0)),
            pl.BlockSpec((1, d), lambda i: (0, 0)),
            _mod_spec(lay, k_shift, tile),
            _mod_spec(lay, k_shift + 1, tile),
            pl.BlockSpec((d, n_out), lambda i: (0, 0)),
        ],
        out_specs=[pl.BlockSpec((tile, s), lambda i: (i, 0)) for s in splits],
        out_shape=[jax.ShapeDtypeStruct((n, s), F32) for s in splits],
        compiler_params=_cparams(("arbitrary",)),
        name="norm_proj",
    )(x, g.reshape(1, d), mods, mods, w_bf16)


def _head_norm(x, g, ones):
    ms = _split_dot(x * x, ones) * (1.0 / HEAD_DIM)
    return x * lax.rsqrt(ms + EPS) * g


def _rope(x, cos, sin_signed):
    n = x.shape[1]
    nxt = pltpu.roll(x, n - 1, 1)
    prv = pltpu.roll(x, 1, 1)
    lane = lax.broadcasted_iota(I32, x.shape, 1)
    swapped = jnp.where(lane % 2 == 0, nxt, prv)
    return x * cos + swapped * sin_signed


def _attn_kernel(*refs, t_len, n_ctx, rotary):
    if rotary:
        (qkv_ref, ck_ref, cv_ref, cos_ref, sin_ref, qg_ref, kg_ref, oq_ref, ok_ref,
         att_ref, k_scr, v_scr) = refs
    else:
        qkv_ref, qg_ref, kg_ref, oq_ref, ok_ref, att_ref, kn_ref, k_scr, v_scr = refs
    qi = pl.program_id(1)
    tq = ATT_Q_TILE

    @pl.when(qi == 0)
    def _prepare_keys():
        k = _head_norm(qkv_ref[:, ATT_Q:ATT_Q + ATT_KV], kg_ref[...], ok_ref[...])
        v = qkv_ref[:, ATT_Q + ATT_KV:ATT_Q + 2 * ATT_KV]
        if rotary:
            k = _rope(k, cos_ref[:, :ATT_KV], sin_ref[:, :ATT_KV])
            k_scr[0:n_ctx, :] = ck_ref[0].astype(BF16)
            v_scr[0:n_ctx, :] = cv_ref[0].astype(BF16)
        else:
            kn_ref[...] = k
        k_scr[n_ctx:n_ctx + t_len, :] = k.astype(BF16)
        v_scr[n_ctx:n_ctx + t_len, :] = v.astype(BF16)

    row0 = pl.multiple_of(qi * tq, tq)
    q = _head_norm(qkv_ref[pl.ds(row0, tq), 0:ATT_Q], qg_ref[...], oq_ref[...])
    if rotary:
        q = _rope(q, cos_ref[pl.ds(row0, tq), :], sin_ref[pl.ds(row0, tq), :])
    q = (q * (HEAD_DIM ** -0.5 * LOG2_E)).astype(BF16)
    outs = []
    for j in range(ATT_KV_HEADS):
        kj = k_scr[:, j * HEAD_DIM:(j + 1) * HEAD_DIM]
        vj = v_scr[:, j * HEAD_DIM:(j + 1) * HEAD_DIM]
        qs = jnp.concatenate(
            [q[:, (j * GQA_GROUP + g) * HEAD_DIM:(j * GQA_GROUP + g + 1) * HEAD_DIM] for g in range(GQA_GROUP)],
            axis=0)
        s = _dot_nt(qs, kj)
        p = jnp.exp2(s - jnp.max(s, axis=-1, keepdims=True))
        o = _dot(p.astype(BF16), vj) / jnp.sum(p, axis=-1, keepdims=True)
        outs.extend(o[g * tq:(g + 1) * tq] for g in range(GQA_GROUP))
    att_ref[...] = jnp.concatenate(outs, axis=1)


def attention(qkv, seq0, n_seq, t_len, q_g, k_g, cache=None, rope=None):
    rotary = cache is not None
    n_ctx = cache[0].shape[1] if rotary else 0
    blk0 = seq0 // t_len
    n_q = t_len // ATT_Q_TILE
    width = qkv.shape[1]
    qg = jnp.tile(q_g, ATT_HEADS).reshape(1, ATT_Q)
    kg = jnp.tile(k_g, ATT_KV_HEADS).reshape(1, ATT_KV)
    const = lambda shape: pl.BlockSpec(shape, lambda b, qi: (0,) * len(shape))
    in_specs = [pl.BlockSpec((t_len, width), lambda b, qi: (blk0 + b, 0))]
    args = [qkv]
    if rotary:
        in_specs += [pl.BlockSpec((1, n_ctx, ATT_KV), lambda b, qi: (b, 0, 0))] * 2
        in_specs += [const((t_len, ATT_Q))] * 2
        args += [cache[0], cache[1], rope[0], rope[1]]
    in_specs += [const((1, ATT_Q)), const((1, ATT_KV)), const((ATT_Q, ATT_Q)), const((ATT_KV, ATT_KV))]
    args += [qg, kg, _block_ones(ATT_Q, HEAD_DIM), _block_ones(ATT_KV, HEAD_DIM)]
    out_specs = [pl.BlockSpec((ATT_Q_TILE, ATT_Q), lambda b, qi: (b * n_q + qi, 0))]
    out_shape = [jax.ShapeDtypeStruct((n_seq * t_len, ATT_Q), F32)]
    if not rotary:
        out_specs.append(pl.BlockSpec((t_len, ATT_KV), lambda b, qi: (b, 0)))
        out_shape.append(jax.ShapeDtypeStruct((n_seq * t_len, ATT_KV), F32))
    return pl.pallas_call(
        functools.partial(_attn_kernel, t_len=t_len, n_ctx=n_ctx, rotary=rotary),
        grid=(n_seq, n_q),
        in_specs=in_specs,
        out_specs=out_specs,
        out_shape=out_shape,
        scratch_shapes=[pltpu.VMEM((n_ctx + t_len, ATT_KV), BF16), pltpu.VMEM((n_ctx + t_len, ATT_KV), BF16)],
        compiler_params=_cparams(("arbitrary", "arbitrary")),
        name="attention_latent" if rotary else "attention_context",
    )(*args)


def rope_tables(t_len):
    t = jnp.arange(t_len)
    pos = jnp.stack([t // GRID_W, t % GRID_W], axis=-1).astype(F32)
    n_freq = HEAD_DIM // 4
    inv = ROPE_THETA ** (-jnp.arange(n_freq, dtype=F32) / n_freq)
    ang = (pos[:, :, None] * inv).reshape(t_len, 2 * n_freq)
    cos = jnp.repeat(jnp.cos(ang), 2, axis=1)
    sin = jnp.repeat(jnp.sin(ang), 2, axis=1) * jnp.tile(jnp.asarray([-1.0, 1.0], F32), HEAD_DIM // 2)
    return jnp.tile(cos, (1, ATT_HEADS)), jnp.tile(sin, (1, ATT_HEADS))


def _shifted_rows(x, prev_row, next_row):
    m = x.shape[0]
    row = lax.broadcasted_iota(I32, x.shape, 0)
    prv = jnp.where(row == 0, prev_row, pltpu.roll(x, 1, 0))
    nxt = jnp.where(row == m - 1, next_row, pltpu.roll(x, m - 1, 0))
    return prv, nxt


def _rwkv_prep_kernel(x_ref, prev_ref, next_ref, mu_ref, kk_ref, ka_ref, rk_ref, w0_ref, wup_ref, a0_ref,
                      aup_ref, gup_ref, ones_ref,
                      r_ref, v_ref, al_ref, lw0_ref, be0_ref, kd0_ref, lw1_ref, be1_ref, kd1_ref, bonus_ref,
                      gate_ref, *, lay):
    i = pl.program_id(0)
    _, _, j, per = lay.seq_of_tile(i, TOKEN_TILE)
    x = x_ref[...]
    prev_row = jnp.where(j == 0, 0.0, prev_ref[HALO - 1:HALO, :])
    next_row = jnp.where(j == per - 1, 0.0, next_ref[0:1, :])
    prv, nxt = _shifted_rows(x, prev_row, next_row)
    u = x + mu_ref[...] * (0.5 * (prv + nxt) - x)

    dim = RWKV_DIM
    r, k, v = u[:, :dim], u[:, dim:2 * dim], u[:, 2 * dim:3 * dim]
    o = 3 * dim
    wd = u[:, o:o + 2 * DECAY_RANK]
    o += 2 * DECAY_RANK
    ad = u[:, o:o + 2 * ICLR_RANK]
    o += 2 * ICLR_RANK
    gd = u[:, o:o + GATE_RANK]

    ones = ones_ref[...]
    kk = k * kk_ref[...]
    norm = jnp.sqrt(_split_dot(kk * kk, ones))
    alpha = kk / jnp.maximum(norm, 1e-12)
    r_ref[...] = r
    v_ref[...] = v
    al_ref[...] = alpha

    tanh_wd = jnp.tanh(wd).astype(BF16)
    ad16 = ad.astype(BF16)
    kd_sum = None
    for d, (lw_ref, be_ref, kd_ref) in enumerate(((lw0_ref, be0_ref, kd0_ref), (lw1_ref, be1_ref, kd1_ref))):
        w_log = w0_ref[d:d + 1, :] + _dot(tanh_wd[:, d * DECAY_RANK:(d + 1) * DECAY_RANK], wup_ref[d])
        lw_ref[...] = -_sigmoid(w_log) * float(np.exp(-0.5))
        a = _sigmoid(a0_ref[d:d + 1, :] + _dot(ad16[:, d * ICLR_RANK:(d + 1) * ICLR_RANK], aup_ref[d]))
        kd = k * (1.0 + (a - 1.0) * ka_ref[...])
        be_ref[...] = alpha * a
        kd_ref[...] = kd
        kd_sum = kd if kd_sum is None else kd_sum + kd
    bonus_ref[...] = _split_dot(r * kd_sum * rk_ref[...], ones) * v
    gate_ref[...] = _dot(_sigmoid(gd).astype(BF16), gup_ref[...])


def rwkv_prep(lay, rw, prm):
    mu, w0, w_up, a0, a_up, g_up, k_k, k_a, r_k = prm
    n, cols = rw.shape
    tile = TOKEN_TILE
    hb = tile // HALO
    n_halo = n // HALO
    dim = RWKV_DIM
    const = lambda shape: pl.BlockSpec(shape, lambda i: (0,) * len(shape))
    row = lambda a: a.reshape(1, -1)
    out_spec = pl.BlockSpec((tile, dim), lambda i: (i, 0))
    return pl.pallas_call(
        functools.partial(_rwkv_prep_kernel, lay=lay),
        grid=(n // tile,),
        in_specs=[
            pl.BlockSpec((tile, cols), lambda i: (i, 0)),
            pl.BlockSpec((HALO, cols), lambda i: (jnp.maximum(i * hb - 1, 0), 0)),
            pl.BlockSpec((HALO, cols), lambda i: (jnp.minimum((i + 1) * hb, n_halo - 1), 0)),
            const((1, cols)), const((1, dim)), const((1, dim)), const((1, dim)),
            const((2, dim)), const((2, DECAY_RANK, dim)), const((2, dim)), const((2, ICLR_RANK, dim)),
            const((GATE_RANK, dim)), const((dim, dim)),
        ],
        out_specs=[out_spec] * 11,
        out_shape=[jax.ShapeDtypeStruct((n, dim), F32)] * 11,
        compiler_params=_cparams(("arbitrary",)),
        name="rwkv_prep",
    )(rw, rw, rw, row(mu), row(k_k), row(k_a), row(r_k), w0, w_up.astype(BF16), a0, a_up.astype(BF16),
      g_up.astype(BF16), _block_ones(dim, RWKV_HD))


def _each(fn, *lists):
    return [fn(*args) for args in zip(*lists)]


def _unit_triangular_inverses(l_mats, eye, diag_blocks):
    mm = lambda a, b: _dot(a.astype(BF16), b.astype(BF16))
    ld = [jnp.where(diag_blocks, l, 0.0) for l in l_mats]
    lo = _each(lambda l, d: l - d, l_mats, ld)
    x = [eye - d for d in ld]
    p = ld
    for _ in range(int(np.log2(INV_BLOCK)) - 1):
        p = _each(mm, p, p)
        x = _each(lambda xi, pi: xi + mm(xi, pi), x, p)
    nb = _each(mm, x, lo)
    y = [eye - m for m in nb]
    p = nb
    for _ in range(int(np.log2(RWKV_CHUNK // INV_BLOCK)) - 1):
        p = _each(mm, p, p)
        y = _each(lambda yi, pi: yi + mm(yi, pi), y, p)
    return _each(mm, y, x)


def _chunk_factors(r, alpha, lw, beta, kd, incl01):
    c_incl = _split_dot_left(incl01, lw)
    c_tot = jnp.sum(lw, axis=0, keepdims=True)
    inv_p = jnp.exp(-c_incl)
    to_end = jnp.exp(c_tot - c_incl)
    r_bar = r * jnp.exp(c_incl)
    b16 = lambda m: m.astype(BF16)
    return dict(a_bar=b16(alpha * jnp.exp(c_incl - lw)), r_bar=r_bar, r_bar16=b16(r_bar),
                b_bar=b16(beta * inv_p), k_bar=b16(kd * inv_p), b_til=b16(beta * to_end),
                k_til=b16(kd * to_end), p_tot=jnp.exp(c_tot))


def _rwkv_chunk_kernel(rf_ref, vf_ref, af_ref, lwf_ref, bef_ref, kdf_ref,
                       rb_ref, vb_ref, ab_ref, lwb_ref, beb_ref, kdb_ref, s0_ref,
                       of_ref, ob_ref, sfin_ref, h_scr, *, lay):
    s = pl.program_id(0)
    _, _, c, per = lay.seq_of_tile(s, RWKV_STEP_CHUNKS * RWKV_CHUNK)

    @pl.when(c == 0)
    def _load_state():
        h_scr[...] = s0_ref[0]

    n = RWKV_CHUNK
    hd = RWKV_HD
    row = lax.broadcasted_iota(I32, (n, n), 0)
    col = lax.broadcasted_iota(I32, (n, n), 1)
    eye = (row == col).astype(F32)
    diag_blocks = (row // INV_BLOCK) == (col // INV_BLOCK)
    b16 = lambda m: m.astype(BF16)
    mm = lambda a, b: _dot(b16(a), b16(b))

    fac, strict_m, incl_m, v16, unit_key = [], [], [], [], []
    for d, refs in enumerate(((rf_ref, vf_ref, af_ref, lwf_ref, bef_ref, kdf_ref),
                              (rb_ref, vb_ref, ab_ref, lwb_ref, beb_ref, kdb_ref))):
        r_ref, v_ref, a_ref, lw_ref, be_ref, kd_ref = refs
        strict = (col > row) if d else (col < row)
        incl = (col >= row) if d else (col <= row)
        incl01 = jnp.where(incl, 1.0, 0.0).astype(BF16)
        for j in range(RWKV_STEP_CHUNKS):
            rows = pl.ds(j * n, n)
            f = _chunk_factors(r_ref[rows, :], a_ref[rows, :], lw_ref[rows, :], be_ref[rows, :], kd_ref[rows, :],
                               incl01)
            v = b16(v_ref[rows, :])
            for h in range(RWKV_HEADS):
                sl = slice(h * hd, (h + 1) * hd)
                fac.append({k: a[:, sl] for k, a in f.items()})
                strict_m.append(strict)
                incl_m.append(incl)
                v16.append(v[:, sl])
                unit_key.append((d, j, h))

    gram = [_dot_nt(jnp.concatenate([f["a_bar"], f["r_bar16"]], axis=0),
                    jnp.concatenate([f["b_bar"], f["k_bar"]], axis=0)) for f in fac]
    l_mat = _each(lambda g, m: jnp.where(m, g[:n, :n], 0.0), gram, strict_m)
    a_k = _each(lambda g, m: jnp.where(m, g[:n, n:], 0.0), gram, strict_m)
    r_b = _each(lambda g, m: jnp.where(m, g[n:, :n], 0.0), gram, incl_m)
    r_k = _each(lambda g, m: jnp.where(m, g[n:, n:], 0.0), gram, incl_m)
    akv = _each(mm, a_k, v16)
    kt_v = _each(lambda f, v: _dot_tn(f["k_til"], v), fac, v16)
    rk_v = _each(mm, r_k, v16)
    t_inv = _unit_triangular_inverses(l_mat, eye, diag_blocks)
    mw = _each(lambda t, f, w: b16(mm(t, jnp.concatenate([f["a_bar"], b16(w)], axis=1))), t_inv, fac, akv)
    bt_mw = _each(lambda f, m: _dot_tn(f["b_til"], m), fac, mw)
    rb_mw = _each(mm, r_b, mw)
    q_eff = _each(lambda f, rb: b16(f["r_bar"] - rb[:, :hd]), fac, rb_mw)
    o_intra = _each(lambda rk, rb: rk - rb[:, hd:], rk_v, rb_mw)
    d_mat = _each(lambda kv, bt: kv - bt[:, hd:], kt_v, bt_mw)
    decay_col = [jnp.sum(eye * f["p_tot"], axis=1, keepdims=True) for f in fac]
    unit = {key: u for u, key in enumerate(unit_key)}

    for d, o_ref in enumerate((of_ref, ob_ref)):
        h_cur = [h_scr[d * RWKV_HEADS + h] for h in range(RWKV_HEADS)]
        order = range(RWKV_STEP_CHUNKS - 1, -1, -1) if d else range(RWKV_STEP_CHUNKS)
        for j in order:
            outs = []
            for h in range(RWKV_HEADS):
                u = unit[(d, j, h)]
                h16 = b16(h_cur[h])
                outs.append(_dot(q_eff[u], h16) + o_intra[u])
                h_cur[h] = decay_col[u] * h_cur[h] - mm(bt_mw[u][:, :hd], h16) + d_mat[u]
            o_ref[pl.ds(j * n, n), :] = jnp.concatenate(outs, axis=1)
        for h in range(RWKV_HEADS):
            h_scr[d * RWKV_HEADS + h] = h_cur[h]

    @pl.when(c == per - 1)
    def _store_state():
        sfin_ref[0] = h_scr[...]


def rwkv_chunks(lay, r, v, alpha, lw0, be0, kd0, lw1, be1, kd1, s0):
    n, dim = r.shape
    rows = RWKV_STEP_CHUNKS * RWKV_CHUNK
    n_seq = lay.bp + lay.bs
    assert lay.tp % rows == 0 and lay.ts % rows == 0

    def fwd(s):
        return (s, 0)

    def bwd(s):
        _, _, c, per = lay.seq_of_tile(s, rows)
        return (s - c + (per - 1 - c), 0)

    def seq(s):
        is_p, q, _, _ = lay.seq_of_tile(s, rows)
        return (jnp.where(is_p, q, lay.bp + q), 0, 0, 0)

    state_block = (1, 2 * RWKV_HEADS, RWKV_HD, RWKV_HD)
    return pl.pallas_call(
        functools.partial(_rwkv_chunk_kernel, lay=lay),
        grid=(n // rows,),
        in_specs=[pl.BlockSpec((rows, dim), fwd)] * 6 + [pl.BlockSpec((rows, dim), bwd)] * 6
        + [pl.BlockSpec(state_block, seq)],
        out_specs=[pl.BlockSpec((rows, dim), fwd), pl.BlockSpec((rows, dim), bwd), pl.BlockSpec(state_block, seq)],
        out_shape=[jax.ShapeDtypeStruct((n, dim), F32), jax.ShapeDtypeStruct((n, dim), F32),
                   jax.ShapeDtypeStruct((n_seq,) + state_block[1:], F32)],
        scratch_shapes=[pltpu.VMEM(state_block[1:], F32)],
        compiler_params=_cparams(("arbitrary",)),
        name="rwkv_chunks",
    )(r, v, alpha, lw0, be0, kd0, r, v, alpha, lw1, be1, kd1, s0)


def _even_out_kernel(x_ref, att_ref, of_ref, ob_ref, bonus_ref, gate_ref, lng_ref, lnb_ref, ones_ref,
                     wa_ref, wr_ref, g1_ref, o_ref):
    ones = ones_ref[...]
    osum = of_ref[...] + ob_ref[...]
    mean = _split_dot(osum, ones) * (1.0 / RWKV_HD)
    cen = osum - mean
    var = _split_dot(cen * cen, ones) * (1.0 / RWKV_HD)
    on = cen * lax.rsqrt(var + GN_EPS) * lng_ref[...] + lnb_ref[...]
    rw = (on + bonus_ref[...]) * gate_ref[...]
    y = _dot(att_ref[...].astype(BF16), wa_ref[...]) + _dot(rw.astype(BF16), wr_ref[...])
    o_ref[...] = x_ref[...] + g1_ref[0] * y


def even_out(lay, x, att, o_f, o_b, bonus, gate, ln_g, ln_b, w_out, mods):
    n, d = x.shape
    tile = TOKEN_TILE
    dim = RWKV_DIM
    const = lambda shape: pl.BlockSpec(shape, lambda i: (0,) * len(shape))
    tok = lambda w: pl.BlockSpec((tile, w), lambda i: (i, 0))
    w16 = w_out.astype(BF16)
    return pl.pallas_call(
        _even_out_kernel,
        grid=(n // tile,),
        in_specs=[tok(d), tok(ATT_Q), tok(dim), tok(dim), tok(dim), tok(dim),
                  const((1, dim)), const((1, dim)), const((dim, dim)),
                  const((ATT_Q, d)), const((dim, d)), _mod_spec(lay, 2, tile)],
        out_specs=tok(d),
        out_shape=jax.ShapeDtypeStruct((n, d), F32),
        compiler_params=_cparams(("arbitrary",)),
        name="even_out",
    )(x, att, o_f, o_b, bonus, gate, ln_g.reshape(1, dim), ln_b.reshape(1, dim), _block_ones(dim, RWKV_HD),
      w16[:ATT_Q], w16[ATT_Q:], mods)


def _gelu_tanh(x):
    return 0.5 * x * (1.0 + jnp.tanh(float(np.sqrt(2.0 / np.pi)) * (x + 0.044715 * (x * x * x))))


def _softplus(x):
    return jnp.maximum(x, 0.0) + jnp.log(1.0 + jnp.exp(-jnp.abs(x)))


def _lru_prep_kernel(x_ref, prev_ref, next_ref, cw_ref, cb_ref, wbd_ref, ba_ref, bx_ref, lam_ref,
                     gate_ref, a0_ref, b0_ref, a1_ref, b1_ref, *, lay):
    i = pl.program_id(0)
    _, _, j, per = lay.seq_of_tile(i, TOKEN_TILE)
    tile = TOKEN_TILE
    gate_ref[...] = _gelu_tanh(x_ref[:, :D_RNN])
    x = x_ref[:, D_RNN:]
    first = j == 0
    last = j == per - 1
    row = lax.broadcasted_iota(I32, x.shape, 0)
    p1 = jnp.where(first, 0.0, prev_ref[HALO - 1:HALO, :])
    p2 = jnp.where(first, 0.0, prev_ref[HALO - 2:HALO - 1, :])
    n1 = jnp.where(last, 0.0, next_ref[0:1, :])
    xm1 = jnp.where(row == 0, p1, pltpu.roll(x, 1, 0))
    xm2 = jnp.where(row == 0, p2, jnp.where(row == 1, p1, pltpu.roll(x, 2, 0)))
    xp1 = jnp.where(row == tile - 1, n1, pltpu.roll(x, tile - 1, 0))
    u = cb_ref[...] + xm2 * cw_ref[0:1, :] + xm1 * cw_ref[1:2, :] + x * cw_ref[2:3, :] + xp1 * cw_ref[3:4, :]

    u16 = u.astype(BF16)
    n_grp = D_RNN // LRU_GROUP
    z = [_dot(u16[:, g * LRU_GROUP:(g + 1) * LRU_GROUP], wbd_ref[g]) for g in range(n_grp)]
    pick = lambda m: jnp.concatenate([zg[:, m * LRU_GROUP:(m + 1) * LRU_GROUP] for zg in z], axis=1)
    for d, (a_ref, b_ref) in enumerate(((a0_ref, b0_ref), (a1_ref, b1_ref))):
        ga = _sigmoid(pick(2 * d) + ba_ref[d:d + 1, :])
        gx = _sigmoid(pick(2 * d + 1) + bx_ref[d:d + 1, :])
        log_a = -LRU_C * ga * _softplus(-lam_ref[d:d + 1, :])
        a_ref[...] = jnp.exp(log_a)
        b_ref[...] = jnp.sqrt(1.0 - jnp.exp(2.0 * log_a)) * gx * u


def lru_prep(lay, proj, conv_w, conv_b, wa, ba, wx, bx, lam):
    n, cols = proj.shape
    tile = TOKEN_TILE
    hb = tile // HALO
    n_halo = n // HALO
    dr = D_RNN
    per = LRU_GROUP // LRU_BS
    n_grp = dr // LRU_GROUP

    def block_diag(w):
        w = w.reshape(n_grp, per, LRU_BS, LRU_BS)
        eye = jnp.eye(per, dtype=w.dtype)
        return jnp.einsum("gpcd,pq->gpcqd", w, eye).reshape(n_grp, LRU_GROUP, LRU_GROUP)

    wbd = jnp.concatenate([block_diag(wa[0]), block_diag(wx[0]), block_diag(wa[1]), block_diag(wx[1])],
                          axis=2).astype(BF16)
    const = lambda shape: pl.BlockSpec(shape, lambda i: (0,) * len(shape))
    out_spec = pl.BlockSpec((tile, dr), lambda i: (i, 0))
    return pl.pallas_call(
        functools.partial(_lru_prep_kernel, lay=lay),
        grid=(n // tile,),
        in_specs=[
            pl.BlockSpec((tile, cols), lambda i: (i, 0)),
            pl.BlockSpec((HALO, dr), lambda i: (jnp.maximum(i * hb - 1, 0), 1)),
            pl.BlockSpec((HALO, dr), lambda i: (jnp.minimum((i + 1) * hb, n_halo - 1), 1)),
            const((CONV_W, dr)), const((1, dr)), const((n_grp, LRU_GROUP, 4 * LRU_GROUP)),
            const((2, dr)), const((2, dr)), const((2, dr)),
        ],
        out_specs=[out_spec] * 5,
        out_shape=[jax.ShapeDtypeStruct((n, dr), F32)] * 5,
        compiler_params=_cparams(("arbitrary",)),
        name="lru_prep",
    )(proj, proj, proj, conv_w, conv_b.reshape(1, dr), wbd, ba, bx, lam)


def _lru_scan_kernel(af_ref, bf_ref, ab_ref, bb_ref, h0_ref, hf_ref, hb_ref, carry, *, lay):
    i = pl.program_id(0)
    _, _, j, _ = lay.seq_of_tile(i, TOKEN_TILE)
    tile = TOKEN_TILE

    @pl.when(j == 0)
    def _load_state():
        carry[...] = h0_ref[0]

    def step(t, hs):
        hf, hb = hs
        tb = tile - 1 - t
        hf = af_ref[pl.ds(t, 1), :] * hf + bf_ref[pl.ds(t, 1), :]
        hb = ab_ref[pl.ds(tb, 1), :] * hb + bb_ref[pl.ds(tb, 1), :]
        hf_ref[pl.ds(t, 1), :] = hf
        hb_ref[pl.ds(tb, 1), :] = hb
        return hf, hb

    hf, hb = lax.fori_loop(0, tile, step, (carry[0:1, :], carry[1:2, :]))
    carry[0:1, :] = hf
    carry[1:2, :] = hb


def lru_scan(lay, a0, b0, a1, b1, h0):
    n, dr = a0.shape
    tile = TOKEN_TILE

    def fwd(i):
        return (i, 0)

    def bwd(i):
        _, _, j, per = lay.seq_of_tile(i, tile)
        return (i - j + (per - 1 - j), 0)

    def seq(i):
        is_p, q, _, _ = lay.seq_of_tile(i, tile)
        return (jnp.where(is_p, q, lay.bp + q), 0, 0)

    return pl.pallas_call(
        functools.partial(_lru_scan_kernel, lay=lay),
        grid=(n // tile,),
        in_specs=[pl.BlockSpec((tile, dr), fwd)] * 2 + [pl.BlockSpec((tile, dr), bwd)] * 2
        + [pl.BlockSpec((1, 2, dr), seq)],
        out_specs=[pl.BlockSpec((tile, dr), fwd), pl.BlockSpec((tile, dr), bwd)],
        out_shape=[jax.ShapeDtypeStruct((n, dr), F32)] * 2,
        scratch_shapes=[pltpu.VMEM((2, dr), F32)],
        compiler_params=_cparams(("arbitrary",)),
        name="lru_scan",
    )(a0, b0, a1, b1, h0)


def _odd_out_kernel(x_ref, gate_ref, hf_ref, hb_ref, w_ref, g1_ref, o_ref):
    y = _dot((gate_ref[...] * (hf_ref[...] + hb_ref[...])).astype(BF16), w_ref[...])
    o_ref[...] = x_ref[...] + g1_ref[0] * y


def odd_out(lay, x, gate, hf, hb, w_out, mods):
    n, d = x.shape
    tile = TOKEN_TILE
    tok = lambda w: pl.BlockSpec((tile, w), lambda i: (i, 0))
    return pl.pallas_call(
        _odd_out_kernel,
        grid=(n // tile,),
        in_specs=[tok(d), tok(D_RNN), tok(D_RNN), tok(D_RNN), pl.BlockSpec((D_RNN, d), lambda i: (0, 0)),
                  _mod_spec(lay, 2, tile)],
        out_specs=tok(d),
        out_shape=jax.ShapeDtypeStruct((n, d), F32),
        compiler_params=_cparams(("arbitrary",)),
        name="odd_out",
    )(x, gate, hf, hb, w_out.astype(BF16), mods)


def _pack_bf16_pairs(x):
    n = x.shape[1] // 2
    hi = pltpu.bitcast(x[:, :n].astype(BF16).astype(F32), jnp.uint32)
    lo = pltpu.bitcast(x[:, n:].astype(BF16).astype(F32), jnp.uint32)
    return hi | (lo >> 16)


def _unpack_bf16_pairs(u):
    a = pltpu.bitcast(u & jnp.uint32(0xFFFF0000), F32)
    b = pltpu.bitcast(u << 16, F32)
    return jnp.concatenate([a, b], axis=1).astype(BF16)


def _router_kernel(x_ref, g_ref, sh_ref, sc_ref, rw_ref, rb_ref, before_ref,
                   h_ref, idx_ref, gate_ref, rank_ref, count_ref, cnt_scr):
    @pl.when(pl.program_id(0) == 0)
    def _reset():
        cnt_scr[...] = jnp.zeros_like(cnt_scr)

    h = _norm_mod(x_ref[...], g_ref[...], sh_ref[0], sc_ref[0])
    h_ref[...] = _pack_bf16_pairs(h)
    logits = lax.dot_general(rw_ref[...], h, (((1,), (1,)), ((), ())), preferred_element_type=F32,
                             precision=HIGHEST) + rb_ref[...]
    e_id = lax.broadcasted_iota(I32, logits.shape, 0)
    vals, ids, hots = [], [], []
    for _ in range(TOP_K):
        m = jnp.max(logits, axis=0, keepdims=True)
        pick = jnp.min(jnp.where(logits == m, e_id, N_EXPERTS), axis=0, keepdims=True)
        hot = e_id == pick
        vals.append(m)
        ids.append(pick)
        hots.append(jnp.where(hot, 1.0, 0.0))
        logits = jnp.where(hot, -jnp.inf, logits)
    top = jnp.concatenate(vals, axis=0)
    p = jnp.exp(top - top[0:1, :])
    gate_ref[...] = p / jnp.sum(p, axis=0, keepdims=True)
    idx_ref[...] = jnp.concatenate(ids, axis=0)
    hot_all = hots[0] + hots[1] + hots[2] + hots[3]
    seen = cnt_scr[:, 0:1] + _dot(hot_all.astype(BF16), before_ref[...])
    rank_ref[...] = jnp.concatenate([jnp.sum(hk * seen, axis=0, keepdims=True) for hk in hots],
                                    axis=0).astype(I32)
    cnt_scr[...] = cnt_scr[...] + jnp.sum(hot_all, axis=1, keepdims=True)
    count_ref[...] = cnt_scr[...]


def router(lay, x, g, mods, rw, rb):
    n, d = x.shape
    tile = TOKEN_TILE
    before = jnp.asarray(np.triu(np.ones((tile, tile), np.float32), 1), dtype=BF16)
    per_tok = pl.BlockSpec((TOP_K, tile), lambda i: (0, i))
    return pl.pallas_call(
        _router_kernel,
        grid=(n // tile,),
        in_specs=[pl.BlockSpec((tile, d), lambda i: (i, 0)), pl.BlockSpec((1, d), lambda i: (0, 0)),
                  _mod_spec(lay, 3, tile), _mod_spec(lay, 4, tile),
                  pl.BlockSpec((N_EXPERTS, d), lambda i: (0, 0)), pl.BlockSpec((N_EXPERTS, 1), lambda i: (0, 0)),
                  pl.BlockSpec((tile, tile), lambda i: (0, 0))],
        out_specs=[pl.BlockSpec((tile, d // 2), lambda i: (i, 0)), per_tok, per_tok, per_tok,
                   pl.BlockSpec((N_EXPERTS, 128), lambda i: (0, 0))],
        out_shape=[jax.ShapeDtypeStruct((n, d // 2), jnp.uint32), jax.ShapeDtypeStruct((TOP_K, n), I32),
                   jax.ShapeDtypeStruct((TOP_K, n), F32), jax.ShapeDtypeStruct((TOP_K, n), I32),
                   jax.ShapeDtypeStruct((N_EXPERTS, 128), F32)],
        scratch_shapes=[pltpu.VMEM((N_EXPERTS, 128), F32)],
        compiler_params=_cparams(("arbitrary",)),
        name="router",
    )(x, g.reshape(1, d), mods, mods, rw.T, rb.reshape(N_EXPERTS, 1), before)


def _row_copy(src_hbm, dst_vmem, sem, src_row, dst_row):
    return pltpu.make_async_copy(src_hbm.at[pl.ds(src_row, 1)], dst_vmem.at[pl.ds(dst_row, 1)], sem)


def _dispatch_kernel(dest_ref, src_ref, o_ref, row_tok, *, n_tok):
    @pl.when(pl.program_id(0) == 0)
    def _invert():
        def clear(i, _):
            row_tok[i] = 0
            return 0

        lax.fori_loop(0, row_tok.shape[0], clear, 0, unroll=8)
        for k in range(TOP_K):
            def put(t, _, k=k):
                row_tok[dest_ref[k * n_tok + t]] = t
                return 0

            lax.fori_loop(0, n_tok, put, 0, unroll=8)

    base = pl.program_id(0) * MOE_TILE

    def copy_row(r, _):
        o_ref[pl.ds(r, 1), :] = src_ref[pl.ds(row_tok[base + r], 1), :]
        return 0

    lax.fori_loop(0, MOE_TILE, copy_row, 0, unroll=8)


def dispatch_rows(src, dest, n_out):
    n, d = src.shape
    return pl.pallas_call(
        functools.partial(_dispatch_kernel, n_tok=n),
        grid_spec=pltpu.PrefetchScalarGridSpec(
            num_scalar_prefetch=1,
            grid=(n_out // MOE_TILE,),
            in_specs=[pl.BlockSpec((n, d), lambda i, dest: (0, 0), pipeline_mode=pl.Buffered(1))],
            out_specs=pl.BlockSpec((MOE_TILE, d), lambda i, dest: (i, 0)),
            scratch_shapes=[pltpu.SMEM((n_out,), I32)],
        ),
        out_shape=jax.ShapeDtypeStruct((n_out, d), src.dtype),
        compiler_params=_cparams(("arbitrary",)),
        name="moe_dispatch",
    )(dest, src)


def _expert_weight_copies(w1_hbm, w2_hbm, w1_buf, w2_buf, sems, row, slot):
    return (pltpu.make_async_copy(w1_hbm.at[row], w1_buf.at[slot], sems.at[slot, 0]),
            pltpu.make_async_copy(w2_hbm.at[row], w2_buf.at[slot], sems.at[slot, 1]))


def _expert_kernel(blk_e_ref, first_ref, slot_ref, next_e_ref, n_used_ref, x_ref, w1_hbm, b1_ref, w2_hbm, b2_ref,
                   o_ref, w1_buf, w2_buf, w1_scr, w2_scr, sems, *, layer):
    i = pl.program_id(0)
    used = i < n_used_ref[0]
    copies = functools.partial(_expert_weight_copies, w1_hbm, w2_hbm, w1_buf, w2_buf, sems)

    @pl.when(jnp.logical_and(used, first_ref[i] == 1))
    def _switch_expert():
        slot = slot_ref[i]

        @pl.when(i == 0)
        def _fetch_first():
            for cp in copies(layer * N_EXPERTS + blk_e_ref[0], 0):
                cp.start()

        nxt = next_e_ref[i]

        @pl.when(nxt >= 0)
        def _prefetch_next():
            for cp in copies(layer * N_EXPERTS + nxt, 1 - slot):
                cp.start(priority=1)

        for cp in copies(0, slot):
            cp.wait()
        w1_scr[...] = w1_buf[slot].astype(BF16)
        w2_scr[...] = w2_buf[slot].astype(BF16)

    @pl.when(used)
    def _compute():
        hb = _dot(_unpack_bf16_pairs(x_ref[...]), w1_scr[...]) + b1_ref[0]
        glu = jnp.minimum(hb[:, :D_EXPERT], SWIGLU_LIMIT)
        lin = jnp.clip(hb[:, D_EXPERT:], -SWIGLU_LIMIT, SWIGLU_LIMIT)
        act = glu * _sigmoid(SWIGLU_ALPHA * glu) * (lin + 1.0)
        o_ref[...] = _dot(act.astype(BF16), w2_scr[...]) + b2_ref[0]

    @pl.when(jnp.logical_not(used))
    def _clear():
        o_ref[...] = jnp.zeros_like(o_ref)


def experts(x_sorted, blk_expert, count, n_used, layer, w1, b1, w2, b2):
    n_rows = x_sorted.shape[0]
    n_blk = n_rows // MOE_TILE
    d, de2 = w1.shape[1:]
    de = w2.shape[1]
    first = jnp.concatenate([jnp.ones((1,), I32), (blk_expert[1:] != blk_expert[:-1]).astype(I32)])
    slot = (jnp.cumsum(first) - 1) % 2
    e_id = jnp.arange(N_EXPERTS, dtype=I32)
    later_present = jnp.logical_and(e_id[None, :] > e_id[:, None], count[None, :] > 0)
    next_present = jnp.min(jnp.where(later_present, e_id[None, :], N_EXPERTS), axis=1)
    next_e = jnp.where(next_present < N_EXPERTS, next_present, -1)[blk_expert]
    bmap = lambda i, be, *_: (layer * N_EXPERTS + be[i], 0, 0)
    return pl.pallas_call(
        functools.partial(_expert_kernel, layer=layer),
        grid_spec=pltpu.PrefetchScalarGridSpec(
            num_scalar_prefetch=5,
            grid=(n_blk,),
            in_specs=[pl.BlockSpec((MOE_TILE, d // 2), lambda i, *_: (i, 0)),
                      pl.BlockSpec(memory_space=pl.ANY), pl.BlockSpec((1, 1, de2), bmap),
                      pl.BlockSpec(memory_space=pl.ANY), pl.BlockSpec((1, 1, d), bmap)],
            out_specs=pl.BlockSpec((MOE_TILE, d), lambda i, *_: (i, 0)),
            scratch_shapes=[pltpu.VMEM((2, d, de2), F32), pltpu.VMEM((2, de, d), F32),
                            pltpu.VMEM((d, de2), BF16), pltpu.VMEM((de, d), BF16),
                            pltpu.SemaphoreType.DMA((2, 2))],
        ),
        out_shape=jax.ShapeDtypeStruct((n_rows, d), F32),
        compiler_params=_cparams(("arbitrary",)),
        name="moe_experts",
    )(blk_expert, first, slot.astype(I32), next_e.astype(I32), n_used, x_sorted, w1, b1, w2, b2)


def _combine_kernel(dest_ref, y_hbm, x_ref, gate_ref, g2_ref, o_ref, buf, sems, *, n_tok):
    tile = TOKEN_TILE
    i = pl.program_id(0)
    slot = i % 2

    def issue(step, dst_slot):
        def body(r, _):
            for k in range(TOP_K):
                _row_copy(y_hbm, buf.at[dst_slot, k], sems.at[dst_slot],
                          dest_ref[k * n_tok + step * tile + r], r).start(priority=k % 2)
            return 0

        lax.fori_loop(0, tile, body, 0, unroll=2)

    @pl.when(i == 0)
    def _first_tile():
        issue(0, 0)

    @pl.when(i + 1 < pl.num_programs(0))
    def _next_tile():
        issue(i + 1, 1 - slot)

    def wait(r, _):
        for k in range(TOP_K):
            _row_copy(y_hbm, buf.at[slot, k], sems.at[slot], 0, r).wait()
        return 0

    lax.fori_loop(0, tile, wait, 0, unroll=8)
    gate = gate_ref[...]
    acc = buf[slot, 0] * gate[:, 0:1]
    for k in range(1, TOP_K):
        acc = acc + buf[slot, k] * gate[:, k:k + 1]
    o_ref[...] = x_ref[...] + g2_ref[0] * acc


def combine(lay, x, y_sorted, dest, gate, mods):
    n, d = x.shape
    tile = TOKEN_TILE
    return pl.pallas_call(
        functools.partial(_combine_kernel, n_tok=n),
        grid_spec=pltpu.PrefetchScalarGridSpec(
            num_scalar_prefetch=1,
            grid=(n // tile,),
            in_specs=[pl.BlockSpec(memory_space=pl.ANY),
                      pl.BlockSpec((tile, d), lambda i, dest: (i, 0)),
                      pl.BlockSpec((tile, TOP_K), lambda i, dest: (i, 0)),
                      _mod_spec(lay, 5, tile)],
            out_specs=pl.BlockSpec((tile, d), lambda i, dest: (i, 0)),
            scratch_shapes=[pltpu.VMEM((2, TOP_K, tile, d), F32), pltpu.SemaphoreType.DMA((2,))],
        ),
        out_shape=jax.ShapeDtypeStruct((n, d), F32),
        compiler_params=_cparams(("arbitrary",)),
        name="moe_combine",
    )(dest, y_sorted, x, gate, mods)


def moe_layer(lay, x, g, mods, rw, rb, layer, w1, b1, w2, b2):
    n, d = x.shape
    h, idx_t, gate_t, rank_t, count = router(lay, x, g, mods, rw, rb)
    expert = idx_t.reshape(-1)
    rank = rank_t.reshape(-1)
    n_asg = expert.shape[0]
    count = count[:, 0].astype(I32)
    padded = (count + MOE_TILE - 1) // MOE_TILE * MOE_TILE
    pend = jnp.cumsum(padded)
    dest = ((pend - padded)[expert] + rank).astype(I32)
    n_blk = n_asg // MOE_TILE + N_EXPERTS
    blk_start = jnp.arange(n_blk, dtype=I32) * MOE_TILE
    blk_expert = jnp.minimum(jnp.sum((pend[None, :] <= blk_start[:, None]).astype(I32), axis=1), N_EXPERTS - 1)
    n_used = (pend[-1] // MOE_TILE).astype(I32).reshape(1)
    x_sorted = dispatch_rows(h, dest, n_blk * MOE_TILE)
    y_sorted = experts(x_sorted, blk_expert, count, n_used, layer, w1, b1, w2, b2)
    return combine(lay, x, y_sorted, dest, gate_t.T, mods)


def kernel(x_prompt, x_sample, cache_attn_k, cache_attn_v, state_rwkv, state_lru, c, c_ctx,
           mod_w, mod_b, norm1_g, norm2_g, ev_w_in, ev_w_out, q_norm_g, k_norm_g,
           rwkv_mu, rwkv_w0, rwkv_w_up, rwkv_a0, rwkv_a_up, rwkv_g_up, rwkv_k_k, rwkv_k_a,
           rwkv_r_k, rwkv_ln_g, rwkv_ln_b, od_w_in, od_w_out, conv_w, conv_b,
           lru_wa, lru_ba, lru_wx, lru_bx, lru_lambda,
           router_w, router_b, exp_w1, exp_b1, exp_w2, exp_b2):
    bp, tp, d = x_prompt.shape
    bs, ts, _ = x_sample.shape
    depth = mod_w.shape[0]
    lay = Layout(bp, tp, bs, ts)
    assert bs < MOD_ROWS and tp % TOKEN_TILE == 0 and ts % TOKEN_TILE == 0 and d == D_MODEL
    n_p = lay.n_p

    x = jnp.concatenate([x_prompt.reshape(n_p, d), x_sample.reshape(bs * ts, d)], axis=0)
    cvec = jnp.zeros((MOD_ROWS, d), F32).at[:bs].set(c).at[bs].set(c_ctx)
    mods_all = modulation(cvec, mod_w, mod_b)
    rope = rope_tables(ts)
    n_le = depth * N_EXPERTS
    expert_prm = (exp_w1.reshape(n_le, d, -1), exp_b1.reshape(n_le, 1, -1),
                  exp_w2.reshape(n_le, -1, d), exp_b2.reshape(n_le, 1, d))

    new_k, new_v, new_rw, new_lru = [], [], [], []
    for l in range(depth):
        j = l // 2
        mods = mods_all[l]
        if l % 2 == 0:
            qkv, rw = norm_proj(lay, x, norm1_g[l], mods, 0, ev_w_in[j].astype(BF16),
                                (ATT_Q + 2 * ATT_KV, RWKV_COLS))
            att_p, k_norm = attention(qkv, 0, bp, tp, q_norm_g[j], k_norm_g[j])
            cache = (cache_attn_k[:, j].reshape(bs, -1, ATT_KV), cache_attn_v[:, j].reshape(bs, -1, ATT_KV))
            (att_s,) = attention(qkv, n_p, bs, ts, q_norm_g[j], k_norm_g[j], cache=cache, rope=rope)
            att = jnp.concatenate([att_p, att_s], axis=0)
            prm = (rwkv_mu[j], rwkv_w0[j], rwkv_w_up[j], rwkv_a0[j], rwkv_a_up[j], rwkv_g_up[j],
                   rwkv_k_k[j], rwkv_k_a[j], rwkv_r_k[j].reshape(-1))
            r, v, alpha, lw0, be0, kd0, lw1, be1, kd1, bonus, gate = rwkv_prep(lay, rw, prm)
            s_lat = jnp.swapaxes(state_rwkv[:, j], -1, -2).reshape(bs, 2 * RWKV_HEADS, RWKV_HD, RWKV_HD)
            s0 = jnp.concatenate([jnp.zeros((bp,) + s_lat.shape[1:], F32), s_lat], axis=0)
            o_f, o_b, s_fin = rwkv_chunks(lay, r, v, alpha, lw0, be0, kd0, lw1, be1, kd1, s0)
            x = even_out(lay, x, att, o_f, o_b, bonus, gate, rwkv_ln_g[j], rwkv_ln_b[j], ev_w_out[j], mods)
            new_k.append(k_norm.reshape(bp, tp, ATT_KV_HEADS, HEAD_DIM))
            new_v.append(qkv[:n_p, ATT_Q + ATT_KV:].reshape(bp, tp, ATT_KV_HEADS, HEAD_DIM))
            new_rw.append(jnp.swapaxes(s_fin[:bp].reshape(bp, 2, RWKV_HEADS, RWKV_HD, RWKV_HD), -1, -2))
        else:
            (proj,) = norm_proj(lay, x, norm1_g[l], mods, 0, od_w_in[j].astype(BF16), (2 * D_RNN,))
            gate, a0, b0, a1, b1 = lru_prep(lay, proj, conv_w[j], conv_b[j], lru_wa[j], lru_ba[j], lru_wx[j],
                                            lru_bx[j], lru_lambda[j])
            h0 = jnp.concatenate([jnp.zeros((bp, 2, D_RNN), F32), state_lru[:, j]], axis=0)
            hf, hb = lru_scan(lay, a0, b0, a1, b1, h0)
            x = odd_out(lay, x, gate, hf, hb, od_w_out[j], mods)
            hf_p = hf[:n_p].reshape(bp, tp, D_RNN)
            hb_p = hb[:n_p].reshape(bp, tp, D_RNN)
            new_lru.append(jnp.stack([hf_p[:, -1], hb_p[:, 0]], axis=1))
        x = moe_layer(lay, x, norm2_g[l], mods, router_w[l], router_b[l], l, *expert_prm)

    y_prompt = x[:n_p].reshape(bp, tp, d)
    y_sample = x[n_p:].reshape(bs, ts, d)
    return (y_prompt, y_sample, jnp.stack(new_k, axis=1), jnp.stack(new_v, axis=1), jnp.stack(new_rw, axis=1),
            jnp.stack(new_lru, axis=1))
```

```python
import functools
from typing import NamedTuple

import numpy as np
import jax
import jax.numpy as jnp
from jax import lax
from jax.experimental import pallas as pl
from jax.experimental.pallas import tpu as pltpu

F32 = jnp.float32
BF16 = jnp.bfloat16
I32 = jnp.int32
HIGHEST = lax.Precision.HIGHEST

D_MODEL = 1024
EPS = 1e-6
GRID_W = 64
ATT_HEADS = 8
ATT_KV_HEADS = 2
HEAD_DIM = 64
GQA_GROUP = ATT_HEADS // ATT_KV_HEADS
ATT_Q = ATT_HEADS * HEAD_DIM
ATT_KV = ATT_KV_HEADS * HEAD_DIM
ROPE_THETA = 10000.0
LOG2_E = 1.4426950408889634
RWKV_HEADS = 8
RWKV_HD = 64
RWKV_DIM = RWKV_HEADS * RWKV_HD
DECAY_RANK = 64
ICLR_RANK = 64
GATE_RANK = 128
RWKV_COLS = 3 * RWKV_DIM + 2 * DECAY_RANK + 2 * ICLR_RANK + GATE_RANK
GN_EPS = 64e-5
D_RNN = D_MODEL
LRU_BS = 64
LRU_GROUP = 256
CONV_W = 4
LRU_C = 8.0
N_EXPERTS = 32
TOP_K = 4
D_EXPERT = D_MODEL
SWIGLU_LIMIT = 7.0
SWIGLU_ALPHA = 1.702

TOKEN_TILE = 256
RWKV_CHUNK = 64
RWKV_STEP_CHUNKS = 4
INV_BLOCK = 16
ATT_Q_TILE = 128
MOE_TILE = 256
HALO = 8
MOD_ROWS = 16
VMEM_LIMIT = 56 * 1024 * 1024


class Layout(NamedTuple):
    bp: int
    tp: int
    bs: int
    ts: int

    @property
    def n_p(self):
        return self.bp * self.tp

    @property
    def n(self):
        return self.bp * self.tp + self.bs * self.ts

    def tiles(self, tile):
        return self.n // tile

    def seq_of_tile(self, i, tile):
        npt = self.n_p // tile
        is_p = i < npt
        ii = jnp.where(is_p, i, i - npt)
        per = jnp.where(is_p, self.tp // tile, self.ts // tile)
        return is_p, ii // per, ii % per, per

    def mod_row(self, i, tile):
        is_p, seq, _, _ = self.seq_of_tile(i, tile)
        return jnp.where(is_p, self.bs, seq)


def _cparams(sem):
    return pltpu.CompilerParams(dimension_semantics=sem, vmem_limit_bytes=VMEM_LIMIT)


def _dot(a, b):
    return jnp.dot(a, b, preferred_element_type=F32)


def _dot_nt(a, b):
    return lax.dot_general(a, b, (((1,), (1,)), ((), ())), preferred_element_type=F32)


def _dot_tn(a, b):
    return lax.dot_general(a, b, (((0,), (0,)), ((), ())), preferred_element_type=F32)


def _split_dot(x, m01):
    hi = x.astype(BF16)
    lo = (x - hi.astype(F32)).astype(BF16)
    return _dot(hi, m01) + _dot(lo, m01)


def _split_dot_left(m01, x):
    hi = x.astype(BF16)
    lo = (x - hi.astype(F32)).astype(BF16)
    return _dot(m01, hi) + _dot(m01, lo)


def _sigmoid(x):
    return 1.0 / (1.0 + jnp.exp(-x))


def _block_ones(n, blk):
    idx = np.arange(n) // blk
    return jnp.asarray((idx[:, None] == idx[None, :]).astype(np.float32), dtype=BF16)


def _mod_kernel(c_ref, w_ref, b_ref, o_ref):
    c = c_ref[...]
    s = c * _sigmoid(c)
    o_ref[0] = jnp.dot(s, w_ref[0], preferred_element_type=F32, precision=HIGHEST) + b_ref[0]


def modulation(cvec, mod_w, mod_b):
    depth, d, six_d = mod_w.shape
    nchunk = six_d // d
    out = pl.pallas_call(
        _mod_kernel,
        grid=(depth, nchunk),
        in_specs=[
            pl.BlockSpec((MOD_ROWS, d), lambda l, k: (0, 0)),
            pl.BlockSpec((1, d, d), lambda l, k: (l, 0, k)),
            pl.BlockSpec((1, 1, d), lambda l, k: (l, 0, k)),
        ],
        out_specs=pl.BlockSpec((1, MOD_ROWS, d), lambda l, k: (l, 0, k)),
        out_shape=jax.ShapeDtypeStruct((depth, MOD_ROWS, six_d), F32),
        compiler_params=_cparams(("arbitrary", "arbitrary")),
        name="modulation",
    )(cvec, mod_w, mod_b.reshape(depth, 1, six_d))
    return out.reshape(depth, MOD_ROWS * nchunk, 1, d)


def _mod_spec(lay, k, tile):
    return pl.BlockSpec((1, 1, D_MODEL), lambda i, *_: (lay.mod_row(i, tile) * 6 + k, 0, 0))


def _norm_mod(x, g, shift, scale):
    ms = jnp.mean(x * x, axis=-1, keepdims=True)
    h = x * lax.rsqrt(ms + EPS) * g
    return h * (1.0 + scale) + shift


def _head_norm(x, g, ones):
    ms = _split_dot(x * x, ones) * (1.0 / HEAD_DIM)
    return x * lax.rsqrt(ms + EPS) * g


def _rope(x, cos, sin_signed):
    n = x.shape[1]
    nxt = pltpu.roll(x, n - 1, 1)
    prv = pltpu.roll(x, 1, 1)
    lane = lax.broadcasted_iota(I32, x.shape, 1)
    swapped = jnp.where(lane % 2 == 0, nxt, prv)
    return x * cos + swapped * sin_signed


def _attn_kernel(*refs, t_len, n_ctx, rotary):
    if rotary:
        (qkv_ref, ck_ref, cv_ref, cos_ref, sin_ref, qg_ref, kg_ref, oq_ref, ok_ref,
         att_ref, k_scr, v_scr) = refs
    else:
        qkv_ref, qg_ref, kg_ref, oq_ref, ok_ref, att_ref, kn_ref, k_scr, v_scr = refs
    qi = pl.program_id(1)
    tq = ATT_Q_TILE

    @pl.when(qi == 0)
    def _prepare_keys():
        k = _head_norm(qkv_ref[:, ATT_Q:ATT_Q + ATT_KV], kg_ref[...], ok_ref[...])
        v = qkv_ref[:, ATT_Q + ATT_KV:ATT_Q + 2 * ATT_KV]
        if rotary:
            k = _rope(k, cos_ref[:, :ATT_KV], sin_ref[:, :ATT_KV])
            k_scr[0:n_ctx, :] = ck_ref[0].astype(BF16)
            v_scr[0:n_ctx, :] = cv_ref[0].astype(BF16)
        else:
            kn_ref[...] = k
        k_scr[n_ctx:n_ctx + t_len, :] = k.astype(BF16)
        v_scr[n_ctx:n_ctx + t_len, :] = v.astype(BF16)

    row0 = pl.multiple_of(qi * tq, tq)
    q = _head_norm(qkv_ref[pl.ds(row0, tq), 0:ATT_Q], qg_ref[...], oq_ref[...])
    if rotary:
        q = _rope(q, cos_ref[pl.ds(row0, tq), :], sin_ref[pl.ds(row0, tq), :])
    q = (q * (HEAD_DIM ** -0.5 * LOG2_E)).astype(BF16)
    outs = []
    for j in range(ATT_KV_HEADS):
        kj = k_scr[:, j * HEAD_DIM:(j + 1) * HEAD_DIM]
        vj = v_scr[:, j * HEAD_DIM:(j + 1) * HEAD_DIM]
        qs = jnp.concatenate(
            [q[:, (j * GQA_GROUP + g) * HEAD_DIM:(j * GQA_GROUP + g + 1) * HEAD_DIM] for g in range(GQA_GROUP)],
            axis=0)
        s = _dot_nt(qs, kj)
        p = jnp.exp2(s - jnp.max(s, axis=-1, keepdims=True))
        o = _dot(p.astype(BF16), vj) / jnp.sum(p, axis=-1, keepdims=True)
        outs.extend(o[g * tq:(g + 1) * tq] for g in range(GQA_GROUP))
    att_ref[...] = jnp.concatenate(outs, axis=1)


def attention(qkv, seq0, n_seq, t_len, q_g, k_g, cache=None, rope=None):
    rotary = cache is not None
    n_ctx = cache[0].shape[1] if rotary else 0
    blk0 = seq0 // t_len
    n_q = t_len // ATT_Q_TILE
    width = qkv.shape[1]
    qg = jnp.tile(q_g, ATT_HEADS).reshape(1, ATT_Q)
    kg = jnp.tile(k_g, ATT_KV_HEADS).reshape(1, ATT_KV)
    const = lambda shape: pl.BlockSpec(shape, lambda b, qi: (0,) * len(shape))
    in_specs = [pl.BlockSpec((t_len, width), lambda b, qi: (blk0 + b, 0))]
    args = [qkv]
    if rotary:
        in_specs += [pl.BlockSpec((1, n_ctx, ATT_KV), lambda b, qi: (b, 0, 0))] * 2
        in_specs += [const((t_len, ATT_Q))] * 2
        args += [cache[0], cache[1], rope[0], rope[1]]
    in_specs += [const((1, ATT_Q)), const((1, ATT_KV)), const((ATT_Q, ATT_Q)), const((ATT_KV, ATT_KV))]
    args += [qg, kg, _block_ones(ATT_Q, HEAD_DIM), _block_ones(ATT_KV, HEAD_DIM)]
    out_specs = [pl.BlockSpec((ATT_Q_TILE, ATT_Q), lambda b, qi: (b * n_q + qi, 0))]
    out_shape = [jax.ShapeDtypeStruct((n_seq * t_len, ATT_Q), F32)]
    if not rotary:
        out_specs.append(pl.BlockSpec((t_len, ATT_KV), lambda b, qi: (b, 0)))
        out_shape.append(jax.ShapeDtypeStruct((n_seq * t_len, ATT_KV), F32))
    return pl.pallas_call(
        functools.partial(_attn_kernel, t_len=t_len, n_ctx=n_ctx, rotary=rotary),
        grid=(n_seq, n_q),
        in_specs=in_specs,
        out_specs=out_specs,
        out_shape=out_shape,
        scratch_shapes=[pltpu.VMEM((n_ctx + t_len, ATT_KV), BF16), pltpu.VMEM((n_ctx + t_len, ATT_KV), BF16)],
        compiler_params=_cparams(("arbitrary", "arbitrary")),
        name="attention_latent" if rotary else "attention_context",
    )(*args)


def rope_tables(t_len):
    t = jnp.arange(t_len)
    pos = jnp.stack([t // GRID_W, t % GRID_W], axis=-1).astype(F32)
    n_freq = HEAD_DIM // 4
    inv = ROPE_THETA ** (-jnp.arange(n_freq, dtype=F32) / n_freq)
    ang = (pos[:, :, None] * inv).reshape(t_len, 2 * n_freq)
    cos = jnp.repeat(jnp.cos(ang), 2, axis=1)
    sin = jnp.repeat(jnp.sin(ang), 2, axis=1) * jnp.tile(jnp.asarray([-1.0, 1.0], F32), HEAD_DIM // 2)
    return jnp.tile(cos, (1, ATT_HEADS)), jnp.tile(sin, (1, ATT_HEADS))


def _shifted_rows(x, prev_row, next_row):
    m = x.shape[0]
    row = lax.broadcasted_iota(I32, x.shape, 0)
    prv = jnp.where(row == 0, prev_row, pltpu.roll(x, 1, 0))
    nxt = jnp.where(row == m - 1, next_row, pltpu.roll(x, m - 1, 0))
    return prv, nxt


def _halo_proj(xprev_ref, xnext_ref, g_ref, sh_ref, sc_ref, w_cols):
    halo = jnp.concatenate([xprev_ref[...], xnext_ref[...]], axis=0)
    return _dot(_norm_mod(halo, g_ref[...], sh_ref[0], sc_ref[0]).astype(BF16), w_cols)


def _even_in_kernel(xm_ref, xprev_ref, xnext_ref, g_ref, sh_ref, sc_ref, w_ref,
                    mu_ref, kk_ref, ka_ref, rk_ref, w0_ref, wup_ref, a0_ref, aup_ref, gup_ref, ones_ref,
                    qkv_ref, r_ref, v_ref, al_ref, lw0_ref, be0_ref, kd0_ref, lw1_ref, be1_ref, kd1_ref,
                    bonus_ref, gate_ref, *, lay):
    i = pl.program_id(0)
    _, _, j, per = lay.seq_of_tile(i, TOKEN_TILE)
    n_qkv = ATT_Q + 2 * ATT_KV
    y = _dot(_norm_mod(xm_ref[...], g_ref[...], sh_ref[0], sc_ref[0]).astype(BF16), w_ref[...])
    qkv_ref[...] = y[:, :n_qkv]
    x = y[:, n_qkv:]
    yh = _halo_proj(xprev_ref, xnext_ref, g_ref, sh_ref, sc_ref, w_ref[:, n_qkv:])
    prev_row = jnp.where(j == 0, 0.0, yh[HALO - 1:HALO, :])
    next_row = jnp.where(j == per - 1, 0.0, yh[HALO:HALO + 1, :])
    prv, nxt = _shifted_rows(x, prev_row, next_row)
    u = x + mu_ref[...] * (0.5 * (prv + nxt) - x)

    dim = RWKV_DIM
    r, k, v = u[:, :dim], u[:, dim:2 * dim], u[:, 2 * dim:3 * dim]
    o = 3 * dim
    wd = u[:, o:o + 2 * DECAY_RANK]
    o += 2 * DECAY_RANK
    ad = u[:, o:o + 2 * ICLR_RANK]
    o += 2 * ICLR_RANK
    gd = u[:, o:o + GATE_RANK]

    ones = ones_ref[...]
    kk = k * kk_ref[...]
    norm = jnp.sqrt(_split_dot(kk * kk, ones))
    alpha = kk / jnp.maximum(norm, 1e-12)
    r_ref[...] = r
    v_ref[...] = v
    al_ref[...] = alpha

    tanh_wd = jnp.tanh(wd).astype(BF16)
    ad16 = ad.astype(BF16)
    kd_sum = None
    for d, (lw_ref, be_ref, kd_ref) in enumerate(((lw0_ref, be0_ref, kd0_ref), (lw1_ref, be1_ref, kd1_ref))):
        w_log = w0_ref[d:d + 1, :] + _dot(tanh_wd[:, d * DECAY_RANK:(d + 1) * DECAY_RANK], wup_ref[d])
        lw_ref[...] = -_sigmoid(w_log) * float(np.exp(-0.5))
        a = _sigmoid(a0_ref[d:d + 1, :] + _dot(ad16[:, d * ICLR_RANK:(d + 1) * ICLR_RANK], aup_ref[d]))
        kd = k * (1.0 + (a - 1.0) * ka_ref[...])
        be_ref[...] = alpha * a
        kd_ref[...] = kd
        kd_sum = kd if kd_sum is None else kd_sum + kd
    bonus_ref[...] = _split_dot(r * kd_sum * rk_ref[...], ones) * v
    gate_ref[...] = _dot(_sigmoid(gd).astype(BF16), gup_ref[...])


def _halo_specs(n, d, tile):
    hb = tile // HALO
    n_halo = n // HALO
    return [pl.BlockSpec((HALO, d), lambda i: (jnp.maximum(i * hb - 1, 0), 0)),
            pl.BlockSpec((HALO, d), lambda i: (jnp.minimum((i + 1) * hb, n_halo - 1), 0))]


def even_in(lay, x, g, mods, w_in, prm):
    mu, w0, w_up, a0, a_up, g_up, k_k, k_a, r_k = prm
    n, d = x.shape
    cols = RWKV_COLS
    n_qkv = ATT_Q + 2 * ATT_KV
    tile = TOKEN_TILE
    dim = RWKV_DIM
    const = lambda shape: pl.BlockSpec(shape, lambda i: (0,) * len(shape))
    row = lambda a: a.reshape(1, -1)
    tok = lambda w: pl.BlockSpec((tile, w), lambda i: (i, 0))
    return pl.pallas_call(
        functools.partial(_even_in_kernel, lay=lay),
        grid=(n // tile,),
        in_specs=[tok(d)] + _halo_specs(n, d, tile) + [
            const((1, d)), _mod_spec(lay, 0, tile), _mod_spec(lay, 1, tile), const((d, n_qkv + cols)),
            const((1, cols)), const((1, dim)), const((1, dim)), const((1, dim)),
            const((2, dim)), const((2, DECAY_RANK, dim)), const((2, dim)), const((2, ICLR_RANK, dim)),
            const((GATE_RANK, dim)), const((dim, dim)),
        ],
        out_specs=[tok(n_qkv)] + [tok(dim)] * 11,
        out_shape=[jax.ShapeDtypeStruct((n, n_qkv), F32)] + [jax.ShapeDtypeStruct((n, dim), F32)] * 11,
        compiler_params=_cparams(("arbitrary",)),
        name="even_in",
    )(x, x, x, g.reshape(1, d), mods, mods, w_in.astype(BF16),
      row(mu), row(k_k), row(k_a), row(r_k), w0, w_up.astype(BF16), a0, a_up.astype(BF16),
      g_up.astype(BF16), _block_ones(dim, RWKV_HD))


def _each(fn, *lists):
    return [fn(*args) for args in zip(*lists)]


def _unit_triangular_inverses(l_mats, eye, diag_blocks):
    mm = lambda a, b: _dot(a.astype(BF16), b.astype(BF16))
    ld = [jnp.where(diag_blocks, l, 0.0) for l in l_mats]
    lo = _each(lambda l, d: l - d, l_mats, ld)
    x = [eye - d for d in ld]
    p = ld
    for _ in range(int(np.log2(INV_BLOCK)) - 1):
        p = _each(mm, p, p)
        x = _each(lambda xi, pi: xi + mm(xi, pi), x, p)
    nb = _each(mm, x, lo)
    y = [eye - m for m in nb]
    p = nb
    for _ in range(int(np.log2(RWKV_CHUNK // INV_BLOCK)) - 1):
        p = _each(mm, p, p)
        y = _each(lambda yi, pi: yi + mm(yi, pi), y, p)
    return _each(mm, y, x)


def _chunk_factors(r, alpha, lw, beta, kd, incl01):
    c_incl = _split_dot_left(incl01, lw)
    c_tot = jnp.sum(lw, axis=0, keepdims=True)
    inv_p = jnp.exp(-c_incl)
    to_end = jnp.exp(c_tot - c_incl)
    r_bar = r * jnp.exp(c_incl)
    b16 = lambda m: m.astype(BF16)
    return dict(a_bar=b16(alpha * jnp.exp(c_incl - lw)), r_bar=r_bar, r_bar16=b16(r_bar),
                b_bar=b16(beta * inv_p), k_bar=b16(kd * inv_p), b_til=b16(beta * to_end),
                k_til=b16(kd * to_end), p_tot=jnp.exp(c_tot))


def _rwkv_chunk_kernel(rf_ref, vf_ref, af_ref, lwf_ref, bef_ref, kdf_ref,
                       rb_ref, vb_ref, ab_ref, lwb_ref, beb_ref, kdb_ref, s0_ref,
                       of_ref, ob_ref, sfin_ref, h_scr, *, lay):
    s = pl.program_id(0)
    _, _, c, per = lay.seq_of_tile(s, RWKV_STEP_CHUNKS * RWKV_CHUNK)

    @pl.when(c == 0)
    def _load_state():
        h_scr[...] = s0_ref[0]

    n = RWKV_CHUNK
    hd = RWKV_HD
    row = lax.broadcasted_iota(I32, (n, n), 0)
    col = lax.broadcasted_iota(I32, (n, n), 1)
    eye = (row == col).astype(F32)
    diag_blocks = (row // INV_BLOCK) == (col // INV_BLOCK)
    b16 = lambda m: m.astype(BF16)
    mm = lambda a, b: _dot(b16(a), b16(b))

    fac, strict_m, incl_m, v16, unit_key = [], [], [], [], []
    for d, refs in enumerate(((rf_ref, vf_ref, af_ref, lwf_ref, bef_ref, kdf_ref),
                              (rb_ref, vb_ref, ab_ref, lwb_ref, beb_ref, kdb_ref))):
        r_ref, v_ref, a_ref, lw_ref, be_ref, kd_ref = refs
        strict = (col > row) if d else (col < row)
        incl = (col >= row) if d else (col <= row)
        incl01 = jnp.where(incl, 1.0, 0.0).astype(BF16)
        for j in range(RWKV_STEP_CHUNKS):
            rows = pl.ds(j * n, n)
            f = _chunk_factors(r_ref[rows, :], a_ref[rows, :], lw_ref[rows, :], be_ref[rows, :], kd_ref[rows, :],
                               incl01)
            v = b16(v_ref[rows, :])
            for h in range(RWKV_HEADS):
                sl = slice(h * hd, (h + 1) * hd)
                fac.append({k: a[:, sl] for k, a in f.items()})
                strict_m.append(strict)
                incl_m.append(incl)
                v16.append(v[:, sl])
                unit_key.append((d, j, h))

    gram = [_dot_nt(jnp.concatenate([f["a_bar"], f["r_bar16"]], axis=0),
                    jnp.concatenate([f["b_bar"], f["k_bar"]], axis=0)) for f in fac]
    l_mat = _each(lambda g, m: jnp.where(m, g[:n, :n], 0.0), gram, strict_m)
    a_k = _each(lambda g, m: jnp.where(m, g[:n, n:], 0.0), gram, strict_m)
    r_b = _each(lambda g, m: jnp.where(m, g[n:, :n], 0.0), gram, incl_m)
    r_k = _each(lambda g, m: jnp.where(m, g[n:, n:], 0.0), gram, incl_m)
    akv = _each(mm, a_k, v16)
    kt_v = _each(lambda f, v: _dot_tn(f["k_til"], v), fac, v16)
    rk_v = _each(mm, r_k, v16)
    t_inv = _unit_triangular_inverses(l_mat, eye, diag_blocks)
    mw = _each(lambda t, f, w: b16(mm(t, jnp.concatenate([f["a_bar"], b16(w)], axis=1))), t_inv, fac, akv)
    bt_mw = _each(lambda f, m: _dot_tn(f["b_til"], m), fac, mw)
    rb_mw = _each(mm, r_b, mw)
    q_eff = _each(lambda f, rb: b16(f["r_bar"] - rb[:, :hd]), fac, rb_mw)
    o_intra = _each(lambda rk, rb: rk - rb[:, hd:], rk_v, rb_mw)
    d_mat = _each(lambda kv, bt: kv - bt[:, hd:], kt_v, bt_mw)
    decay_col = [jnp.sum(eye * f["p_tot"], axis=1, keepdims=True) for f in fac]
    unit = {key: u for u, key in enumerate(unit_key)}

    for d, o_ref in enumerate((of_ref, ob_ref)):
        h_cur = [h_scr[d * RWKV_HEADS + h] for h in range(RWKV_HEADS)]
        order = range(RWKV_STEP_CHUNKS - 1, -1, -1) if d else range(RWKV_STEP_CHUNKS)
        for j in order:
            outs = []
            for h in range(RWKV_HEADS):
                u = unit[(d, j, h)]
                h16 = b16(h_cur[h])
                outs.append(_dot(q_eff[u], h16) + o_intra[u])
                h_cur[h] = decay_col[u] * h_cur[h] - mm(bt_mw[u][:, :hd], h16) + d_mat[u]
            o_ref[pl.ds(j * n, n), :] = jnp.concatenate(outs, axis=1)
        for h in range(RWKV_HEADS):
            h_scr[d * RWKV_HEADS + h] = h_cur[h]

    @pl.when(c == per - 1)
    def _store_state():
        sfin_ref[0] = h_scr[...]


def rwkv_chunks(lay, r, v, alpha, lw0, be0, kd0, lw1, be1, kd1, s0):
    n, dim = r.shape
    rows = RWKV_STEP_CHUNKS * RWKV_CHUNK
    n_seq = lay.bp + lay.bs
    assert lay.tp % rows == 0 and lay.ts % rows == 0

    def fwd(s):
        return (s, 0)

    def bwd(s):
        _, _, c, per = lay.seq_of_tile(s, rows)
        return (s - c + (per - 1 - c), 0)

    def seq(s):
        is_p, q, _, _ = lay.seq_of_tile(s, rows)
        return (jnp.where(is_p, q, lay.bp + q), 0, 0, 0)

    state_block = (1, 2 * RWKV_HEADS, RWKV_HD, RWKV_HD)
    return pl.pallas_call(
        functools.partial(_rwkv_chunk_kernel, lay=lay),
        grid=(n // rows,),
        in_specs=[pl.BlockSpec((rows, dim), fwd)] * 6 + [pl.BlockSpec((rows, dim), bwd)] * 6
        + [pl.BlockSpec(state_block, seq)],
        out_specs=[pl.BlockSpec((rows, dim), fwd), pl.BlockSpec((rows, dim), bwd), pl.BlockSpec(state_block, seq)],
        out_shape=[jax.ShapeDtypeStruct((n, dim), F32), jax.ShapeDtypeStruct((n, dim), F32),
                   jax.ShapeDtypeStruct((n_seq,) + state_block[1:], F32)],
        scratch_shapes=[pltpu.VMEM(state_block[1:], F32)],
        compiler_params=_cparams(("arbitrary",)),
        name="rwkv_chunks",
    )(r, v, alpha, lw0, be0, kd0, r, v, alpha, lw1, be1, kd1, s0)


def _even_out_kernel(x_ref, att_ref, of_ref, ob_ref, bonus_ref, gate_ref, lng_ref, lnb_ref, ones_ref,
                     wa_ref, wr_ref, g1_ref, o_ref):
    ones = ones_ref[...]
    osum = of_ref[...] + ob_ref[...]
    mean = _split_dot(osum, ones) * (1.0 / RWKV_HD)
    cen = osum - mean
    var = _split_dot(cen * cen, ones) * (1.0 / RWKV_HD)
    on = cen * lax.rsqrt(var + GN_EPS) * lng_ref[...] + lnb_ref[...]
    rw = (on + bonus_ref[...]) * gate_ref[...]
    y = _dot(att_ref[...].astype(BF16), wa_ref[...]) + _dot(rw.astype(BF16), wr_ref[...])
    o_ref[...] = x_ref[...] + g1_ref[0] * y


def even_out(lay, x, att, o_f, o_b, bonus, gate, ln_g, ln_b, w_out, mods):
    n, d = x.shape
    tile = TOKEN_TILE
    dim = RWKV_DIM
    const = lambda shape: pl.BlockSpec(shape, lambda i: (0,) * len(shape))
    tok = lambda w: pl.BlockSpec((tile, w), lambda i: (i, 0))
    w16 = w_out.astype(BF16)
    return pl.pallas_call(
        _even_out_kernel,
        grid=(n // tile,),
        in_specs=[tok(d), tok(ATT_Q), tok(dim), tok(dim), tok(dim), tok(dim),
                  const((1, dim)), const((1, dim)), const((dim, dim)),
                  const((ATT_Q, d)), const((dim, d)), _mod_spec(lay, 2, tile)],
        out_specs=tok(d),
        out_shape=jax.ShapeDtypeStruct((n, d), F32),
        compiler_params=_cparams(("arbitrary",)),
        name="even_out",
    )(x, att, o_f, o_b, bonus, gate, ln_g.reshape(1, dim), ln_b.reshape(1, dim), _block_ones(dim, RWKV_HD),
      w16[:ATT_Q], w16[ATT_Q:], mods)


def _gelu_tanh(x):
    return 0.5 * x * (1.0 + jnp.tanh(float(np.sqrt(2.0 / np.pi)) * (x + 0.044715 * (x * x * x))))


def _softplus(x):
    return jnp.maximum(x, 0.0) + jnp.log(1.0 + jnp.exp(-jnp.abs(x)))


def _odd_in_kernel(xm_ref, xprev_ref, xnext_ref, g_ref, sh_ref, sc_ref, w_ref, cw_ref, cb_ref,
                   gate_ref, u_ref, *, lay):
    i = pl.program_id(0)
    _, _, j, per = lay.seq_of_tile(i, TOKEN_TILE)
    tile = TOKEN_TILE
    y = _dot(_norm_mod(xm_ref[...], g_ref[...], sh_ref[0], sc_ref[0]).astype(BF16), w_ref[...])
    gate_ref[...] = _gelu_tanh(y[:, :D_RNN])
    x = y[:, D_RNN:]
    yh = _halo_proj(xprev_ref, xnext_ref, g_ref, sh_ref, sc_ref, w_ref[:, D_RNN:])
    first = j == 0
    last = j == per - 1
    row = lax.broadcasted_iota(I32, x.shape, 0)
    p1 = jnp.where(first, 0.0, yh[HALO - 1:HALO, :])
    p2 = jnp.where(first, 0.0, yh[HALO - 2:HALO - 1, :])
    n1 = jnp.where(last, 0.0, yh[HALO:HALO + 1, :])
    xm1 = jnp.where(row == 0, p1, pltpu.roll(x, 1, 0))
    xm2 = jnp.where(row == 0, p2, jnp.where(row == 1, p1, pltpu.roll(x, 2, 0)))
    xp1 = jnp.where(row == tile - 1, n1, pltpu.roll(x, tile - 1, 0))
    u_ref[...] = (cb_ref[...] + xm2 * cw_ref[0:1, :] + xm1 * cw_ref[1:2, :] + x * cw_ref[2:3, :]
                  + xp1 * cw_ref[3:4, :])


def odd_in(lay, x, g, mods, w_in, conv_w, conv_b):
    n, d = x.shape
    tile = TOKEN_TILE
    dr = D_RNN
    const = lambda shape: pl.BlockSpec(shape, lambda i: (0,) * len(shape))
    tok = lambda w: pl.BlockSpec((tile, w), lambda i: (i, 0))
    return pl.pallas_call(
        functools.partial(_odd_in_kernel, lay=lay),
        grid=(n // tile,),
        in_specs=[tok(d)] + _halo_specs(n, d, tile) + [
            const((1, d)), _mod_spec(lay, 0, tile), _mod_spec(lay, 1, tile), const((d, 2 * dr)),
            const((CONV_W, dr)), const((1, dr))],
        out_specs=[tok(dr)] * 2,
        out_shape=[jax.ShapeDtypeStruct((n, dr), F32)] * 2,
        compiler_params=_cparams(("arbitrary",)),
        name="odd_in",
    )(x, x, x, g.reshape(1, d), mods, mods, w_in.astype(BF16), conv_w, conv_b.reshape(1, dr))


def _lru_coefficients(u, d, wbd_ref, ba_ref, bx_ref, lam_ref):
    u16 = u.astype(BF16)
    n_grp = D_RNN // LRU_GROUP
    z = [_dot(u16[:, g * LRU_GROUP:(g + 1) * LRU_GROUP], wbd_ref[d, g]) for g in range(n_grp)]
    pick = lambda m: jnp.concatenate([zg[:, m * LRU_GROUP:(m + 1) * LRU_GROUP] for zg in z], axis=1)
    ga = _sigmoid(pick(0) + ba_ref[d:d + 1, :])
    gx = _sigmoid(pick(1) + bx_ref[d:d + 1, :])
    log_a = -LRU_C * ga * _softplus(-lam_ref[d:d + 1, :])
    return jnp.exp(log_a), jnp.sqrt(1.0 - jnp.exp(2.0 * log_a)) * gx * u


def _lru_scan_kernel(uf_ref, ub_ref, wbd_ref, ba_ref, bx_ref, lam_ref, h0_ref, hf_ref, hb_ref,
                     af_ref, bf_ref, ab_ref, bb_ref, carry, *, lay):
    i = pl.program_id(0)
    _, _, j, _ = lay.seq_of_tile(i, TOKEN_TILE)
    tile = TOKEN_TILE

    @pl.when(j == 0)
    def _load_state():
        carry[...] = h0_ref[0]

    af_ref[...], bf_ref[...] = _lru_coefficients(uf_ref[...], 0, wbd_ref, ba_ref, bx_ref, lam_ref)
    ab_ref[...], bb_ref[...] = _lru_coefficients(ub_ref[...], 1, wbd_ref, ba_ref, bx_ref, lam_ref)

    def step(t, hs):
        hf, hb = hs
        tb = tile - 1 - t
        hf = af_ref[pl.ds(t, 1), :] * hf + bf_ref[pl.ds(t, 1), :]
        hb = ab_ref[pl.ds(tb, 1), :] * hb + bb_ref[pl.ds(tb, 1), :]
        hf_ref[pl.ds(t, 1), :] = hf
        hb_ref[pl.ds(tb, 1), :] = hb
        return hf, hb

    hf, hb = lax.fori_loop(0, tile, step, (carry[0:1, :], carry[1:2, :]))
    carry[0:1, :] = hf
    carry[1:2, :] = hb


def lru_scan(lay, u, wa, ba, wx, bx, lam, h0):
    n, dr = u.shape
    tile = TOKEN_TILE
    per = LRU_GROUP // LRU_BS
    n_grp = dr // LRU_GROUP

    def block_diag(w):
        w = w.reshape(n_grp, per, LRU_BS, LRU_BS)
        eye = jnp.eye(per, dtype=w.dtype)
        return jnp.einsum("gpcd,pq->gpcqd", w, eye).reshape(n_grp, LRU_GROUP, LRU_GROUP)

    wbd = jnp.stack([jnp.concatenate([block_diag(wa[d]), block_diag(wx[d])], axis=2) for d in range(2)]).astype(BF16)
    const = lambda shape: pl.BlockSpec(shape, lambda i: (0,) * len(shape))

    def fwd(i):
        return (i, 0)

    def bwd(i):
        _, _, j, per = lay.seq_of_tile(i, tile)
        return (i - j + (per - 1 - j), 0)

    def seq(i):
        is_p, q, _, _ = lay.seq_of_tile(i, tile)
        return (jnp.where(is_p, q, lay.bp + q), 0, 0)

    return pl.pallas_call(
        functools.partial(_lru_scan_kernel, lay=lay),
        grid=(n // tile,),
        in_specs=[pl.BlockSpec((tile, dr), fwd), pl.BlockSpec((tile, dr), bwd),
                  const((2, n_grp, LRU_GROUP, 2 * LRU_GROUP)), const((2, dr)), const((2, dr)), const((2, dr)),
                  pl.BlockSpec((1, 2, dr), seq)],
        out_specs=[pl.BlockSpec((tile, dr), fwd), pl.BlockSpec((tile, dr), bwd)],
        out_shape=[jax.ShapeDtypeStruct((n, dr), F32)] * 2,
        scratch_shapes=[pltpu.VMEM((tile, dr), F32)] * 4 + [pltpu.VMEM((2, dr), F32)],
        compiler_params=_cparams(("arbitrary",)),
        name="lru_scan",
    )(u, u, wbd, ba, bx, lam, h0)


def _odd_out_kernel(x_ref, gate_ref, hf_ref, hb_ref, w_ref, g1_ref, o_ref):
    y = _dot((gate_ref[...] * (hf_ref[...] + hb_ref[...])).astype(BF16), w_ref[...])
    o_ref[...] = x_ref[...] + g1_ref[0] * y


def odd_out(lay, x, gate, hf, hb, w_out, mods):
    n, d = x.shape
    tile = TOKEN_TILE
    tok = lambda w: pl.BlockSpec((tile, w), lambda i: (i, 0))
    return pl.pallas_call(
        _odd_out_kernel,
        grid=(n // tile,),
        in_specs=[tok(d), tok(D_RNN), tok(D_RNN), tok(D_RNN), pl.BlockSpec((D_RNN, d), lambda i: (0, 0)),
                  _mod_spec(lay, 2, tile)],
        out_specs=tok(d),
        out_shape=jax.ShapeDtypeStruct((n, d), F32),
        compiler_params=_cparams(("arbitrary",)),
        name="odd_out",
    )(x, gate, hf, hb, w_out.astype(BF16), mods)


def _pack_bf16_pairs(x):
    n = x.shape[1] // 2
    hi = pltpu.bitcast(x[:, :n].astype(BF16).astype(F32), jnp.uint32)
    lo = pltpu.bitcast(x[:, n:].astype(BF16).astype(F32), jnp.uint32)
    return hi | (lo >> 16)


def _unpack_bf16_pairs(u):
    a = pltpu.bitcast(u & jnp.uint32(0xFFFF0000), F32)
    b = pltpu.bitcast(u << 16, F32)
    return jnp.concatenate([a, b], axis=1).astype(BF16)


def _router_kernel(x_ref, g_ref, sh_ref, sc_ref, rw_ref, rb_ref, before_ref,
                   h_ref, idx_ref, gate_ref, rank_ref, count_ref, cnt_scr):
    @pl.when(pl.program_id(0) == 0)
    def _reset():
        cnt_scr[...] = jnp.zeros_like(cnt_scr)

    h = _norm_mod(x_ref[...], g_ref[...], sh_ref[0], sc_ref[0])
    h_ref[...] = _pack_bf16_pairs(h)
    logits = lax.dot_general(rw_ref[...], h, (((1,), (1,)), ((), ())), preferred_element_type=F32,
                             precision=HIGHEST) + rb_ref[...]
    e_id = lax.broadcasted_iota(I32, logits.shape, 0)
    vals, ids, hots = [], [], []
    for _ in range(TOP_K):
        m = jnp.max(logits, axis=0, keepdims=True)
        pick = jnp.min(jnp.where(logits == m, e_id, N_EXPERTS), axis=0, keepdims=True)
        hot = e_id == pick
        vals.append(m)
        ids.append(pick)
        hots.append(jnp.where(hot, 1.0, 0.0))
        logits = jnp.where(hot, -jnp.inf, logits)
    top = jnp.concatenate(vals, axis=0)
    p = jnp.exp(top - top[0:1, :])
    gate_ref[...] = p / jnp.sum(p, axis=0, keepdims=True)
    idx_ref[...] = jnp.concatenate(ids, axis=0)
    hot_all = hots[0] + hots[1] + hots[2] + hots[3]
    seen = cnt_scr[:, 0:1] + _dot(hot_all.astype(BF16), before_ref[...])
    rank_ref[...] = jnp.concatenate([jnp.sum(hk * seen, axis=0, keepdims=True) for hk in hots],
                                    axis=0).astype(I32)
    cnt_scr[...] = cnt_scr[...] + jnp.sum(hot_all, axis=1, keepdims=True)
    count_ref[...] = cnt_scr[...]


def router(lay, x, g, mods, rw, rb):
    n, d = x.shape
    tile = TOKEN_TILE
    before = jnp.asarray(np.triu(np.ones((tile, tile), np.float32), 1), dtype=BF16)
    per_tok = pl.BlockSpec((TOP_K, tile), lambda i: (0, i))
    return pl.pallas_call(
        _router_kernel,
        grid=(n // tile,),
        in_specs=[pl.BlockSpec((tile, d), lambda i: (i, 0)), pl.BlockSpec((1, d), lambda i: (0, 0)),
                  _mod_spec(lay, 3, tile), _mod_spec(lay, 4, tile),
                  pl.BlockSpec((N_EXPERTS, d), lambda i: (0, 0)), pl.BlockSpec((N_EXPERTS, 1), lambda i: (0, 0)),
                  pl.BlockSpec((tile, tile), lambda i: (0, 0))],
        out_specs=[pl.BlockSpec((tile, d // 2), lambda i: (i, 0)), per_tok, per_tok, per_tok,
                   pl.BlockSpec((N_EXPERTS, 128), lambda i: (0, 0))],
        out_shape=[jax.ShapeDtypeStruct((n, d // 2), jnp.uint32), jax.ShapeDtypeStruct((TOP_K, n), I32),
                   jax.ShapeDtypeStruct((TOP_K, n), F32), jax.ShapeDtypeStruct((TOP_K, n), I32),
                   jax.ShapeDtypeStruct((N_EXPERTS, 128), F32)],
        scratch_shapes=[pltpu.VMEM((N_EXPERTS, 128), F32)],
        compiler_params=_cparams(("arbitrary",)),
        name="router",
    )(x, g.reshape(1, d), mods, mods, rw.T, rb.reshape(N_EXPERTS, 1), before)


def _row_copy(src_hbm, dst_vmem, sem, src_row, dst_row):
    return pltpu.make_async_copy(src_hbm.at[pl.ds(src_row, 1)], dst_vmem.at[pl.ds(dst_row, 1)], sem)


def _dispatch_kernel(dest_ref, src_ref, o_ref, row_tok, *, n_tok):
    @pl.when(pl.program_id(0) == 0)
    def _invert():
        def clear(i, _):
            row_tok[i] = 0
            return 0

        lax.fori_loop(0, row_tok.shape[0], clear, 0, unroll=8)
        for k in range(TOP_K):
            def put(t, _, k=k):
                row_tok[dest_ref[k * n_tok + t]] = t
                return 0

            lax.fori_loop(0, n_tok, put, 0, unroll=8)

    base = pl.program_id(0) * MOE_TILE

    def copy_row(r, _):
        o_ref[pl.ds(r, 1), :] = src_ref[pl.ds(row_tok[base + r], 1), :]
        return 0

    lax.fori_loop(0, MOE_TILE, copy_row, 0, unroll=8)


def dispatch_rows(src, dest, n_out):
    n, d = src.shape
    return pl.pallas_call(
        functools.partial(_dispatch_kernel, n_tok=n),
        grid_spec=pltpu.PrefetchScalarGridSpec(
            num_scalar_prefetch=1,
            grid=(n_out // MOE_TILE,),
            in_specs=[pl.BlockSpec((n, d), lambda i, dest: (0, 0), pipeline_mode=pl.Buffered(1))],
            out_specs=pl.BlockSpec((MOE_TILE, d), lambda i, dest: (i, 0)),
            scratch_shapes=[pltpu.SMEM((n_out,), I32)],
        ),
        out_shape=jax.ShapeDtypeStruct((n_out, d), src.dtype),
        compiler_params=_cparams(("arbitrary",)),
        name="moe_dispatch",
    )(dest, src)


def _expert_weight_copies(w1_hbm, w2_hbm, w1_buf, w2_buf, sems, row, slot):
    return (pltpu.make_async_copy(w1_hbm.at[row], w1_buf.at[slot], sems.at[slot, 0]),
            pltpu.make_async_copy(w2_hbm.at[row], w2_buf.at[slot], sems.at[slot, 1]))


def _expert_kernel(blk_e_ref, first_ref, slot_ref, next_e_ref, n_used_ref, x_ref, w1_hbm, b1_ref, w2_hbm, b2_ref,
                   o_ref, w1_buf, w2_buf, w1_scr, w2_scr, sems, *, layer):
    i = pl.program_id(0)
    used = i < n_used_ref[0]
    copies = functools.partial(_expert_weight_copies, w1_hbm, w2_hbm, w1_buf, w2_buf, sems)

    @pl.when(jnp.logical_and(used, first_ref[i] == 1))
    def _switch_expert():
        slot = slot_ref[i]

        @pl.when(i == 0)
        def _fetch_first():
            for cp in copies(layer * N_EXPERTS + blk_e_ref[0], 0):
                cp.start()

        nxt = next_e_ref[i]

        @pl.when(nxt >= 0)
        def _prefetch_next():
            for cp in copies(layer * N_EXPERTS + nxt, 1 - slot):
                cp.start(priority=1)

        for cp in copies(0, slot):
            cp.wait()
        w1_scr[...] = w1_buf[slot].astype(BF16)
        w2_scr[...] = w2_buf[slot].astype(BF16)

    @pl.when(used)
    def _compute():
        hb = _dot(_unpack_bf16_pairs(x_ref[...]), w1_scr[...]) + b1_ref[0]
        glu = jnp.minimum(hb[:, :D_EXPERT], SWIGLU_LIMIT)
        lin = jnp.clip(hb[:, D_EXPERT:], -SWIGLU_LIMIT, SWIGLU_LIMIT)
        act = glu * _sigmoid(SWIGLU_ALPHA * glu) * (lin + 1.0)
        o_ref[...] = _dot(act.astype(BF16), w2_scr[...]) + b2_ref[0]

    @pl.when(jnp.logical_not(used))
    def _clear():
        o_ref[...] = jnp.zeros_like(o_ref)


def experts(x_sorted, blk_expert, count, n_used, layer, w1, b1, w2, b2):
    n_rows = x_sorted.shape[0]
    n_blk = n_rows // MOE_TILE
    d, de2 = w1.shape[1:]
    de = w2.shape[1]
    first = jnp.concatenate([jnp.ones((1,), I32), (blk_expert[1:] != blk_expert[:-1]).astype(I32)])
    slot = (jnp.cumsum(first) - 1) % 2
    e_id = jnp.arange(N_EXPERTS, dtype=I32)
    later_present = jnp.logical_and(e_id[None, :] > e_id[:, None], count[None, :] > 0)
    next_present = jnp.min(jnp.where(later_present, e_id[None, :], N_EXPERTS), axis=1)
    next_e = jnp.where(next_present < N_EXPERTS, next_present, -1)[blk_expert]
    bmap = lambda i, be, *_: (layer * N_EXPERTS + be[i], 0, 0)
    return pl.pallas_call(
        functools.partial(_expert_kernel, layer=layer),
        grid_spec=pltpu.PrefetchScalarGridSpec(
            num_scalar_prefetch=5,
            grid=(n_blk,),
            in_specs=[pl.BlockSpec((MOE_TILE, d // 2), lambda i, *_: (i, 0)),
                      pl.BlockSpec(memory_space=pl.ANY), pl.BlockSpec((1, 1, de2), bmap),
                      pl.BlockSpec(memory_space=pl.ANY), pl.BlockSpec((1, 1, d), bmap)],
            out_specs=pl.BlockSpec((MOE_TILE, d), lambda i, *_: (i, 0)),
            scratch_shapes=[pltpu.VMEM((2, d, de2), F32), pltpu.VMEM((2, de, d), F32),
                            pltpu.VMEM((d, de2), BF16), pltpu.VMEM((de, d), BF16),
                            pltpu.SemaphoreType.DMA((2, 2))],
        ),
        out_shape=jax.ShapeDtypeStruct((n_rows, d), F32),
        compiler_params=_cparams(("arbitrary",)),
        name="moe_experts",
    )(blk_expert, first, slot.astype(I32), next_e.astype(I32), n_used, x_sorted, w1, b1, w2, b2)


def _combine_kernel(dest_ref, y_hbm, x_ref, gate_ref, g2_ref, o_ref, buf, sems, *, n_tok):
    tile = TOKEN_TILE
    i = pl.program_id(0)
    slot = i % 2

    def issue(step, dst_slot):
        def body(r, _):
            for k in range(TOP_K):
                _row_copy(y_hbm, buf.at[dst_slot, k], sems.at[dst_slot],
                          dest_ref[k * n_tok + step * tile + r], r).start(priority=k % 2)
            return 0

        lax.fori_loop(0, tile, body, 0, unroll=2)

    @pl.when(i == 0)
    def _first_tile():
        issue(0, 0)

    @pl.when(i + 1 < pl.num_programs(0))
    def _next_tile():
        issue(i + 1, 1 - slot)

    def wait(r, _):
        for k in range(TOP_K):
            _row_copy(y_hbm, buf.at[slot, k], sems.at[slot], 0, r).wait()
        return 0

    lax.fori_loop(0, tile, wait, 0, unroll=8)
    gate = gate_ref[...]
    acc = buf[slot, 0] * gate[:, 0:1]
    for k in range(1, TOP_K):
        acc = acc + buf[slot, k] * gate[:, k:k + 1]
    o_ref[...] = x_ref[...] + g2_ref[0] * acc


def combine(lay, x, y_sorted, dest, gate, mods):
    n, d = x.shape
    tile = TOKEN_TILE
    return pl.pallas_call(
        functools.partial(_combine_kernel, n_tok=n),
        grid_spec=pltpu.PrefetchScalarGridSpec(
            num_scalar_prefetch=1,
            grid=(n // tile,),
            in_specs=[pl.BlockSpec(memory_space=pl.ANY),
                      pl.BlockSpec((tile, d), lambda i, dest: (i, 0)),
                      pl.BlockSpec((tile, TOP_K), lambda i, dest: (i, 0)),
                      _mod_spec(lay, 5, tile)],
            out_specs=pl.BlockSpec((tile, d), lambda i, dest: (i, 0)),
            scratch_shapes=[pltpu.VMEM((2, TOP_K, tile, d), F32), pltpu.SemaphoreType.DMA((2,))],
        ),
        out_shape=jax.ShapeDtypeStruct((n, d), F32),
        compiler_params=_cparams(("arbitrary",)),
        name="moe_combine",
    )(dest, y_sorted, x, gate, mods)


def moe_layer(lay, x, g, mods, rw, rb, layer, w1, b1, w2, b2):
    n, d = x.shape
    h, idx_t, gate_t, rank_t, count = router(lay, x, g, mods, rw, rb)
    expert = idx_t.reshape(-1)
    rank = rank_t.reshape(-1)
    n_asg = expert.shape[0]
    count = count[:, 0].astype(I32)
    padded = (count + MOE_TILE - 1) // MOE_TILE * MOE_TILE
    pend = jnp.cumsum(padded)
    dest = ((pend - padded)[expert] + rank).astype(I32)
    n_blk = n_asg // MOE_TILE + N_EXPERTS
    blk_start = jnp.arange(n_blk, dtype=I32) * MOE_TILE
    blk_expert = jnp.minimum(jnp.sum((pend[None, :] <= blk_start[:, None]).astype(I32), axis=1), N_EXPERTS - 1)
    n_used = (pend[-1] // MOE_TILE).astype(I32).reshape(1)
    x_sorted = dispatch_rows(h, dest, n_blk * MOE_TILE)
    y_sorted = experts(x_sorted, blk_expert, count, n_used, layer, w1, b1, w2, b2)
    return combine(lay, x, y_sorted, dest, gate_t.T, mods)


def kernel(x_prompt, x_sample, cache_attn_k, cache_attn_v, state_rwkv, state_lru, c, c_ctx,
           mod_w, mod_b, norm1_g, norm2_g, ev_w_in, ev_w_out, q_norm_g, k_norm_g,
           rwkv_mu, rwkv_w0, rwkv_w_up, rwkv_a0, rwkv_a_up, rwkv_g_up, rwkv_k_k, rwkv_k_a,
           rwkv_r_k, rwkv_ln_g, rwkv_ln_b, od_w_in, od_w_out, conv_w, conv_b,
           lru_wa, lru_ba, lru_wx, lru_bx, lru_lambda,
           router_w, router_b, exp_w1, exp_b1, exp_w2, exp_b2):
    bp, tp, d = x_prompt.shape
    bs, ts, _ = x_sample.shape
    depth = mod_w.shape[0]
    lay = Layout(bp, tp, bs, ts)
    assert bs < MOD_ROWS and tp % TOKEN_TILE == 0 and ts % TOKEN_TILE == 0 and d == D_MODEL
    n_p = lay.n_p

    x = jnp.concatenate([x_prompt.reshape(n_p, d), x_sample.reshape(bs * ts, d)], axis=0)
    cvec = jnp.zeros((MOD_ROWS, d), F32).at[:bs].set(c).at[bs].set(c_ctx)
    mods_all = modulation(cvec, mod_w, mod_b)
    rope = rope_tables(ts)
    n_le = depth * N_EXPERTS
    expert_prm = (exp_w1.reshape(n_le, d, -1), exp_b1.reshape(n_le, 1, -1),
                  exp_w2.reshape(n_le, -1, d), exp_b2.reshape(n_le, 1, d))

    new_k, new_v, new_rw, new_lru = [], [], [], []
    for l in range(depth):
        j = l // 2
        mods = mods_all[l]
        if l % 2 == 0:
            prm = (rwkv_mu[j], rwkv_w0[j], rwkv_w_up[j], rwkv_a0[j], rwkv_a_up[j], rwkv_g_up[j],
                   rwkv_k_k[j], rwkv_k_a[j], rwkv_r_k[j].reshape(-1))
            qkv, r, v, alpha, lw0, be0, kd0, lw1, be1, kd1, bonus, gate = even_in(
                lay, x, norm1_g[l], mods, ev_w_in[j], prm)
            att_p, k_norm = attention(qkv, 0, bp, tp, q_norm_g[j], k_norm_g[j])
            cache = (cache_attn_k[:, j].reshape(bs, -1, ATT_KV), cache_attn_v[:, j].reshape(bs, -1, ATT_KV))
            (att_s,) = attention(qkv, n_p, bs, ts, q_norm_g[j], k_norm_g[j], cache=cache, rope=rope)
            att = jnp.concatenate([att_p, att_s], axis=0)
            s_lat = jnp.swapaxes(state_rwkv[:, j], -1, -2).reshape(bs, 2 * RWKV_HEADS, RWKV_HD, RWKV_HD)
            s0 = jnp.concatenate([jnp.zeros((bp,) + s_lat.shape[1:], F32), s_lat], axis=0)
            o_f, o_b, s_fin = rwkv_chunks(lay, r, v, alpha, lw0, be0, kd0, lw1, be1, kd1, s0)
            x = even_out(lay, x, att, o_f, o_b, bonus, gate, rwkv_ln_g[j], rwkv_ln_b[j], ev_w_out[j], mods)
            new_k.append(k_norm.reshape(bp, tp, ATT_KV_HEADS, HEAD_DIM))
            new_v.append(qkv[:n_p, ATT_Q + ATT_KV:].reshape(bp, tp, ATT_KV_HEADS, HEAD_DIM))
            new_rw.append(jnp.swapaxes(s_fin[:bp].reshape(bp, 2, RWKV_HEADS, RWKV_HD, RWKV_HD), -1, -2))
        else:
            gate, u = odd_in(lay, x, norm1_g[l], mods, od_w_in[j], conv_w[j], conv_b[j])
            h0 = jnp.concatenate([jnp.zeros((bp, 2, D_RNN), F32), state_lru[:, j]], axis=0)
            hf, hb = lru_scan(lay, u, lru_wa[j], lru_ba[j], lru_wx[j], lru_bx[j], lru_lambda[j], h0)
            x = odd_out(lay, x, gate, hf, hb, od_w_out[j], mods)
            hf_p = hf[:n_p].reshape(bp, tp, D_RNN)
            hb_p = hb[:n_p].reshape(bp, tp, D_RNN)
            new_lru.append(jnp.stack([hf_p[:, -1], hb_p[:, 0]], axis=1))
        x = moe_layer(lay, x, norm2_g[l], mods, router_w[l], router_b[l], l, *expert_prm)

    y_prompt = x[:n_p].reshape(bp, tp, d)
    y_sample = x[n_p:].reshape(bs, ts, d)
    return (y_prompt, y_sample, jnp.stack(new_k, axis=1), jnp.stack(new_v, axis=1), jnp.stack(new_rw, axis=1),
            jnp.stack(new_lru, axis=1))
```

```python
import functools
from typing import NamedTuple

import numpy as np
import jax
import jax.numpy as jnp
from jax import lax
from jax.experimental import pallas as pl
from jax.experimental.pallas import tpu as pltpu

F32 = jnp.float32
BF16 = jnp.bfloat16
I32 = jnp.int32
HIGHEST = lax.Precision.HIGHEST

D_MODEL = 1024
EPS = 1e-6
GRID_W = 64
ATT_HEADS = 8
ATT_KV_HEADS = 2
HEAD_DIM = 64
GQA_GROUP = ATT_HEADS // ATT_KV_HEADS
ATT_Q = ATT_HEADS * HEAD_DIM
ATT_KV = ATT_KV_HEADS * HEAD_DIM
ROPE_THETA = 10000.0
LOG2_E = 1.4426950408889634
RWKV_HEADS = 8
RWKV_HD = 64
RWKV_DIM = RWKV_HEADS * RWKV_HD
DECAY_RANK = 64
ICLR_RANK = 64
GATE_RANK = 128
RWKV_COLS = 3 * RWKV_DIM + 2 * DECAY_RANK + 2 * ICLR_RANK + GATE_RANK
GN_EPS = 64e-5
D_RNN = D_MODEL
LRU_BS = 64
LRU_GROUP = 256
CONV_W = 4
LRU_C = 8.0
N_EXPERTS = 32
TOP_K = 4
D_EXPERT = D_MODEL
SWIGLU_LIMIT = 7.0
SWIGLU_ALPHA = 1.702

TOKEN_TILE = 256
RWKV_CHUNK = 64
RWKV_STEP_CHUNKS = 4
INV_BLOCK = 16
ATT_Q_TILE = 128
MOE_TILE = 256
HALO = 8
MOD_ROWS = 16
VMEM_LIMIT = 56 * 1024 * 1024


class Layout(NamedTuple):
    bp: int
    tp: int
    bs: int
    ts: int

    @property
    def n_p(self):
        return self.bp * self.tp

    @property
    def n(self):
        return self.bp * self.tp + self.bs * self.ts

    def tiles(self, tile):
        return self.n // tile

    def seq_of_tile(self, i, tile):
        npt = self.n_p // tile
        is_p = i < npt
        ii = jnp.where(is_p, i, i - npt)
        per = jnp.where(is_p, self.tp // tile, self.ts // tile)
        return is_p, ii // per, ii % per, per

    def mod_row(self, i, tile):
        is_p, seq, _, _ = self.seq_of_tile(i, tile)
        return jnp.where(is_p, self.bs, seq)


def _cparams(sem):
    return pltpu.CompilerParams(dimension_semantics=sem, vmem_limit_bytes=VMEM_LIMIT)


def _dot(a, b):
    return jnp.dot(a, b, preferred_element_type=F32)


def _dot_nt(a, b):
    return lax.dot_general(a, b, (((1,), (1,)), ((), ())), preferred_element_type=F32)


def _dot_tn(a, b):
    return lax.dot_general(a, b, (((0,), (0,)), ((), ())), preferred_element_type=F32)


def _split_dot(x, m01):
    hi = x.astype(BF16)
    lo = (x - hi.astype(F32)).astype(BF16)
    return _dot(hi, m01) + _dot(lo, m01)


def _split_dot_left(m01, x):
    hi = x.astype(BF16)
    lo = (x - hi.astype(F32)).astype(BF16)
    return _dot(m01, hi) + _dot(m01, lo)


def _sigmoid(x):
    return 1.0 / (1.0 + jnp.exp(-x))


def _block_ones(n, blk):
    idx = np.arange(n) // blk
    return jnp.asarray((idx[:, None] == idx[None, :]).astype(np.float32), dtype=BF16)


def _mod_kernel(c_ref, w_ref, b_ref, o_ref):
    c = c_ref[...]
    s = c * _sigmoid(c)
    o_ref[0] = jnp.dot(s, w_ref[0], preferred_element_type=F32, precision=HIGHEST) + b_ref[0]


def modulation(cvec, mod_w, mod_b):
    depth, d, six_d = mod_w.shape
    nchunk = six_d // d
    out = pl.pallas_call(
        _mod_kernel,
        grid=(depth, nchunk),
        in_specs=[
            pl.BlockSpec((MOD_ROWS, d), lambda l, k: (0, 0)),
            pl.BlockSpec((1, d, d), lambda l, k: (l, 0, k)),
            pl.BlockSpec((1, 1, d), lambda l, k: (l, 0, k)),
        ],
        out_specs=pl.BlockSpec((1, MOD_ROWS, d), lambda l, k: (l, 0, k)),
        out_shape=jax.ShapeDtypeStruct((depth, MOD_ROWS, six_d), F32),
        compiler_params=_cparams(("arbitrary", "arbitrary")),
        name="modulation",
    )(cvec, mod_w, mod_b.reshape(depth, 1, six_d))
    return out.reshape(depth, MOD_ROWS * nchunk, 1, d)


def _mod_spec(lay, k, tile):
    return pl.BlockSpec((1, 1, D_MODEL), lambda i, *_: (lay.mod_row(i, tile) * 6 + k, 0, 0))


def _norm_mod(x, g, shift, scale):
    ms = jnp.mean(x * x, axis=-1, keepdims=True)
    h = x * lax.rsqrt(ms + EPS) * g
    return h * (1.0 + scale) + shift


def _head_norm(x, g, ones):
    ms = _split_dot(x * x, ones) * (1.0 / HEAD_DIM)
    return x * lax.rsqrt(ms + EPS) * g


def _rope(x, cos, sin_signed):
    n = x.shape[1]
    nxt = pltpu.roll(x, n - 1, 1)
    prv = pltpu.roll(x, 1, 1)
    lane = lax.broadcasted_iota(I32, x.shape, 1)
    swapped = jnp.where(lane % 2 == 0, nxt, prv)
    return x * cos + swapped * sin_signed


def _attn_kernel(*refs, t_len, n_ctx, rotary):
    if rotary:
        (qkv_ref, ck_ref, cv_ref, cos_ref, sin_ref, qg_ref, kg_ref, oq_ref, ok_ref,
         att_ref, k_scr, v_scr) = refs
    else:
        qkv_ref, qg_ref, kg_ref, oq_ref, ok_ref, att_ref, kn_ref, k_scr, v_scr = refs
    qi = pl.program_id(1)
    tq = ATT_Q_TILE

    @pl.when(qi == 0)
    def _prepare_keys():
        k = _head_norm(qkv_ref[:, ATT_Q:ATT_Q + ATT_KV], kg_ref[...], ok_ref[...])
        v = qkv_ref[:, ATT_Q + ATT_KV:ATT_Q + 2 * ATT_KV]
        if rotary:
            k = _rope(k, cos_ref[:, :ATT_KV], sin_ref[:, :ATT_KV])
            k_scr[0:n_ctx, :] = ck_ref[0].astype(BF16)
            v_scr[0:n_ctx, :] = cv_ref[0].astype(BF16)
        else:
            kn_ref[...] = k
        k_scr[n_ctx:n_ctx + t_len, :] = k.astype(BF16)
        v_scr[n_ctx:n_ctx + t_len, :] = v.astype(BF16)

    row0 = pl.multiple_of(qi * tq, tq)
    q = _head_norm(qkv_ref[pl.ds(row0, tq), 0:ATT_Q], qg_ref[...], oq_ref[...])
    if rotary:
        q = _rope(q, cos_ref[pl.ds(row0, tq), :], sin_ref[pl.ds(row0, tq), :])
    q = (q * (HEAD_DIM ** -0.5 * LOG2_E)).astype(BF16)
    outs = []
    for j in range(ATT_KV_HEADS):
        kj = k_scr[:, j * HEAD_DIM:(j + 1) * HEAD_DIM]
        vj = v_scr[:, j * HEAD_DIM:(j + 1) * HEAD_DIM]
        qs = jnp.concatenate(
            [q[:, (j * GQA_GROUP + g) * HEAD_DIM:(j * GQA_GROUP + g + 1) * HEAD_DIM] for g in range(GQA_GROUP)],
            axis=0)
        s = _dot_nt(qs, kj)
        p = jnp.exp2(s - jnp.max(s, axis=-1, keepdims=True))
        o = _dot(p.astype(BF16), vj) / jnp.sum(p, axis=-1, keepdims=True)
        outs.extend(o[g * tq:(g + 1) * tq] for g in range(GQA_GROUP))
    att_ref[...] = jnp.concatenate(outs, axis=1)


def attention(qkv, seq0, n_seq, t_len, q_g, k_g, cache=None, rope=None):
    rotary = cache is not None
    n_ctx = cache[0].shape[1] if rotary else 0
    blk0 = seq0 // t_len
    n_q = t_len // ATT_Q_TILE
    width = qkv.shape[1]
    qg = jnp.tile(q_g, ATT_HEADS).reshape(1, ATT_Q)
    kg = jnp.tile(k_g, ATT_KV_HEADS).reshape(1, ATT_KV)
    const = lambda shape: pl.BlockSpec(shape, lambda b, qi: (0,) * len(shape))
    in_specs = [pl.BlockSpec((t_len, width), lambda b, qi: (blk0 + b, 0))]
    args = [qkv]
    if rotary:
        in_specs += [pl.BlockSpec((1, n_ctx, ATT_KV), lambda b, qi: (b, 0, 0))] * 2
        in_specs += [const((t_len, ATT_Q))] * 2
        args += [cache[0], cache[1], rope[0], rope[1]]
    in_specs += [const((1, ATT_Q)), const((1, ATT_KV)), const((ATT_Q, ATT_Q)), const((ATT_KV, ATT_KV))]
    args += [qg, kg, _block_ones(ATT_Q, HEAD_DIM), _block_ones(ATT_KV, HEAD_DIM)]
    out_specs = [pl.BlockSpec((ATT_Q_TILE, ATT_Q), lambda b, qi: (b * n_q + qi, 0))]
    out_shape = [jax.ShapeDtypeStruct((n_seq * t_len, ATT_Q), F32)]
    if not rotary:
        out_specs.append(pl.BlockSpec((t_len, ATT_KV), lambda b, qi: (b, 0)))
        out_shape.append(jax.ShapeDtypeStruct((n_seq * t_len, ATT_KV), F32))
    return pl.pallas_call(
        functools.partial(_attn_kernel, t_len=t_len, n_ctx=n_ctx, rotary=rotary),
        grid=(n_seq, n_q),
        in_specs=in_specs,
        out_specs=out_specs,
        out_shape=out_shape,
        scratch_shapes=[pltpu.VMEM((n_ctx + t_len, ATT_KV), BF16), pltpu.VMEM((n_ctx + t_len, ATT_KV), BF16)],
        compiler_params=_cparams(("arbitrary", "arbitrary")),
        name="attention_latent" if rotary else "attention_context",
    )(*args)


def rope_tables(t_len):
    t = jnp.arange(t_len)
    pos = jnp.stack([t // GRID_W, t % GRID_W], axis=-1).astype(F32)
    n_freq = HEAD_DIM // 4
    inv = ROPE_THETA ** (-jnp.arange(n_freq, dtype=F32) / n_freq)
    ang = (pos[:, :, None] * inv).reshape(t_len, 2 * n_freq)
    cos = jnp.repeat(jnp.cos(ang), 2, axis=1)
    sin = jnp.repeat(jnp.sin(ang), 2, axis=1) * jnp.tile(jnp.asarray([-1.0, 1.0], F32), HEAD_DIM // 2)
    return jnp.tile(cos, (1, ATT_HEADS)), jnp.tile(sin, (1, ATT_HEADS))


def _shifted_rows(x, prev_row, next_row):
    m = x.shape[0]
    row = lax.broadcasted_iota(I32, x.shape, 0)
    prv = jnp.where(row == 0, prev_row, pltpu.roll(x, 1, 0))
    nxt = jnp.where(row == m - 1, next_row, pltpu.roll(x, m - 1, 0))
    return prv, nxt


def _halo_proj(xprev_ref, xnext_ref, g_ref, sh_ref, sc_ref, w_cols):
    halo = jnp.concatenate([xprev_ref[...], xnext_ref[...]], axis=0)
    return _dot(_norm_mod(halo, g_ref[...], sh_ref[0], sc_ref[0]).astype(BF16), w_cols)


def _even_in_kernel(xm_ref, xprev_ref, xnext_ref, g_ref, sh_ref, sc_ref, w_ref,
                    mu_ref, kk_ref, ka_ref, rk_ref, w0_ref, wup_ref, a0_ref, aup_ref, gup_ref, ones_ref,
                    qkv_ref, r_ref, v_ref, al_ref, lw0_ref, be0_ref, kd0_ref, lw1_ref, be1_ref, kd1_ref,
                    bonus_ref, gate_ref, *, lay):
    i = pl.program_id(0)
    _, _, j, per = lay.seq_of_tile(i, TOKEN_TILE)
    n_qkv = ATT_Q + 2 * ATT_KV
    y = _dot(_norm_mod(xm_ref[...], g_ref[...], sh_ref[0], sc_ref[0]).astype(BF16), w_ref[...])
    qkv_ref[...] = y[:, :n_qkv]
    x = y[:, n_qkv:]
    yh = _halo_proj(xprev_ref, xnext_ref, g_ref, sh_ref, sc_ref, w_ref[:, n_qkv:])
    prev_row = jnp.where(j == 0, 0.0, yh[HALO - 1:HALO, :])
    next_row = jnp.where(j == per - 1, 0.0, yh[HALO:HALO + 1, :])
    prv, nxt = _shifted_rows(x, prev_row, next_row)
    u = x + mu_ref[...] * (0.5 * (prv + nxt) - x)

    dim = RWKV_DIM
    r, k, v = u[:, :dim], u[:, dim:2 * dim], u[:, 2 * dim:3 * dim]
    o = 3 * dim
    wd = u[:, o:o + 2 * DECAY_RANK]
    o += 2 * DECAY_RANK
    ad = u[:, o:o + 2 * ICLR_RANK]
    o += 2 * ICLR_RANK
    gd = u[:, o:o + GATE_RANK]

    ones = ones_ref[...]
    kk = k * kk_ref[...]
    norm = jnp.sqrt(_split_dot(kk * kk, ones))
    alpha = kk / jnp.maximum(norm, 1e-12)
    r_ref[...] = r
    v_ref[...] = v
    al_ref[...] = alpha

    tanh_wd = jnp.tanh(wd).astype(BF16)
    ad16 = ad.astype(BF16)
    kd_sum = None
    for d, (lw_ref, be_ref, kd_ref) in enumerate(((lw0_ref, be0_ref, kd0_ref), (lw1_ref, be1_ref, kd1_ref))):
        w_log = w0_ref[d:d + 1, :] + _dot(tanh_wd[:, d * DECAY_RANK:(d + 1) * DECAY_RANK], wup_ref[d])
        lw_ref[...] = -_sigmoid(w_log) * float(np.exp(-0.5))
        a = _sigmoid(a0_ref[d:d + 1, :] + _dot(ad16[:, d * ICLR_RANK:(d + 1) * ICLR_RANK], aup_ref[d]))
        kd = k * (1.0 + (a - 1.0) * ka_ref[...])
        be_ref[...] = alpha * a
        kd_ref[...] = kd
        kd_sum = kd if kd_sum is None else kd_sum + kd
    bonus_ref[...] = _split_dot(r * kd_sum * rk_ref[...], ones) * v
    gate_ref[...] = _dot(_sigmoid(gd).astype(BF16), gup_ref[...])


def _halo_specs(n, d, tile):
    hb = tile // HALO
    n_halo = n // HALO
    return [pl.BlockSpec((HALO, d), lambda i: (jnp.maximum(i * hb - 1, 0), 0)),
            pl.BlockSpec((HALO, d), lambda i: (jnp.minimum((i + 1) * hb, n_halo - 1), 0))]


def even_in(lay, x, g, mods, w_in, prm):
    mu, w0, w_up, a0, a_up, g_up, k_k, k_a, r_k = prm
    n, d = x.shape
    cols = RWKV_COLS
    n_qkv = ATT_Q + 2 * ATT_KV
    tile = TOKEN_TILE
    dim = RWKV_DIM
    const = lambda shape: pl.BlockSpec(shape, lambda i: (0,) * len(shape))
    row = lambda a: a.reshape(1, -1)
    tok = lambda w: pl.BlockSpec((tile, w), lambda i: (i, 0))
    return pl.pallas_call(
        functools.partial(_even_in_kernel, lay=lay),
        grid=(n // tile,),
        in_specs=[tok(d)] + _halo_specs(n, d, tile) + [
            const((1, d)), _mod_spec(lay, 0, tile), _mod_spec(lay, 1, tile), const((d, n_qkv + cols)),
            const((1, cols)), const((1, dim)), const((1, dim)), const((1, dim)),
            const((2, dim)), const((2, DECAY_RANK, dim)), const((2, dim)), const((2, ICLR_RANK, dim)),
            const((GATE_RANK, dim)), const((dim, dim)),
        ],
        out_specs=[tok(n_qkv)] + [tok(dim)] * 11,
        out_shape=[jax.ShapeDtypeStruct((n, n_qkv), F32)] + [jax.ShapeDtypeStruct((n, dim), F32)] * 11,
        compiler_params=_cparams(("arbitrary",)),
        name="even_in",
    )(x, x, x, g.reshape(1, d), mods, mods, w_in.astype(BF16),
      row(mu), row(k_k), row(k_a), row(r_k), w0, w_up.astype(BF16), a0, a_up.astype(BF16),
      g_up.astype(BF16), _block_ones(dim, RWKV_HD))


def _each(fn, *lists):
    return [fn(*args) for args in zip(*lists)]


def _unit_triangular_inverses(l_mats, eye, diag_blocks):
    mm = lambda a, b: _dot(a.astype(BF16), b.astype(BF16))
    ld = [jnp.where(diag_blocks, l, 0.0) for l in l_mats]
    lo = _each(lambda l, d: l - d, l_mats, ld)
    x = [eye - d for d in ld]
    p = ld
    for _ in range(int(np.log2(INV_BLOCK)) - 1):
        p = _each(mm, p, p)
        x = _each(lambda xi, pi: xi + mm(xi, pi), x, p)
    nb = _each(mm, x, lo)
    y = [eye - m for m in nb]
    p = nb
    for _ in range(int(np.log2(RWKV_CHUNK // INV_BLOCK)) - 1):
        p = _each(mm, p, p)
        y = _each(lambda yi, pi: yi + mm(yi, pi), y, p)
    return _each(mm, y, x)


def _chunk_factors(r, alpha, lw, beta, kd, incl01):
    c_incl = _split_dot_left(incl01, lw)
    c_tot = jnp.sum(lw, axis=0, keepdims=True)
    inv_p = jnp.exp(-c_incl)
    to_end = jnp.exp(c_tot - c_incl)
    r_bar = r * jnp.exp(c_incl)
    b16 = lambda m: m.astype(BF16)
    return dict(a_bar=b16(alpha * jnp.exp(c_incl - lw)), r_bar=r_bar, r_bar16=b16(r_bar),
                b_bar=b16(beta * inv_p), k_bar=b16(kd * inv_p), b_til=b16(beta * to_end),
                k_til=b16(kd * to_end), p_tot=jnp.exp(c_tot))


def _rwkv_chunk_kernel(rf_ref, vf_ref, af_ref, lwf_ref, bef_ref, kdf_ref,
                       rb_ref, vb_ref, ab_ref, lwb_ref, beb_ref, kdb_ref, s0_ref,
                       of_ref, ob_ref, sfin_ref, h_scr, *, lay):
    s = pl.program_id(0)
    _, _, c, per = lay.seq_of_tile(s, RWKV_STEP_CHUNKS * RWKV_CHUNK)

    @pl.when(c == 0)
    def _load_state():
        h_scr[...] = s0_ref[0]

    n = RWKV_CHUNK
    hd = RWKV_HD
    row = lax.broadcasted_iota(I32, (n, n), 0)
    col = lax.broadcasted_iota(I32, (n, n), 1)
    eye = (row == col).astype(F32)
    diag_blocks = (row // INV_BLOCK) == (col // INV_BLOCK)
    b16 = lambda m: m.astype(BF16)
    mm = lambda a, b: _dot(b16(a), b16(b))

    fac, strict_m, incl_m, v16, unit_key = [], [], [], [], []
    for d, refs in enumerate(((rf_ref, vf_ref, af_ref, lwf_ref, bef_ref, kdf_ref),
                              (rb_ref, vb_ref, ab_ref, lwb_ref, beb_ref, kdb_ref))):
        r_ref, v_ref, a_ref, lw_ref, be_ref, kd_ref = refs
        strict = (col > row) if d else (col < row)
        incl = (col >= row) if d else (col <= row)
        incl01 = jnp.where(incl, 1.0, 0.0).astype(BF16)
        for j in range(RWKV_STEP_CHUNKS):
            rows = pl.ds(j * n, n)
            f = _chunk_factors(r_ref[rows, :], a_ref[rows, :], lw_ref[rows, :], be_ref[rows, :], kd_ref[rows, :],
                               incl01)
            v = b16(v_ref[rows, :])
            for h in range(RWKV_HEADS):
                sl = slice(h * hd, (h + 1) * hd)
                fac.append({k: a[:, sl] for k, a in f.items()})
                strict_m.append(strict)
                incl_m.append(incl)
                v16.append(v[:, sl])
                unit_key.append((d, j, h))

    gram = [_dot_nt(jnp.concatenate([f["a_bar"], f["r_bar16"]], axis=0),
                    jnp.concatenate([f["b_bar"], f["k_bar"]], axis=0)) for f in fac]
    l_mat = _each(lambda g, m: jnp.where(m, g[:n, :n], 0.0), gram, strict_m)
    a_k = _each(lambda g, m: jnp.where(m, g[:n, n:], 0.0), gram, strict_m)
    r_b = _each(lambda g, m: jnp.where(m, g[n:, :n], 0.0), gram, incl_m)
    r_k = _each(lambda g, m: jnp.where(m, g[n:, n:], 0.0), gram, incl_m)
    akv = _each(mm, a_k, v16)
    kt_v = _each(lambda f, v: _dot_tn(f["k_til"], v), fac, v16)
    rk_v = _each(mm, r_k, v16)
    t_inv = _unit_triangular_inverses(l_mat, eye, diag_blocks)
    mw = _each(lambda t, f, w: b16(mm(t, jnp.concatenate([f["a_bar"], b16(w)], axis=1))), t_inv, fac, akv)
    bt_mw = _each(lambda f, m: _dot_tn(f["b_til"], m), fac, mw)
    rb_mw = _each(mm, r_b, mw)
    q_eff = _each(lambda f, rb: b16(f["r_bar"] - rb[:, :hd]), fac, rb_mw)
    o_intra = _each(lambda rk, rb: rk - rb[:, hd:], rk_v, rb_mw)
    d_mat = _each(lambda kv, bt: kv - bt[:, hd:], kt_v, bt_mw)
    decay_col = [jnp.sum(eye * f["p_tot"], axis=1, keepdims=True) for f in fac]
    unit = {key: u for u, key in enumerate(unit_key)}

    for d, o_ref in enumerate((of_ref, ob_ref)):
        h_cur = [h_scr[d * RWKV_HEADS + h] for h in range(RWKV_HEADS)]
        order = range(RWKV_STEP_CHUNKS - 1, -1, -1) if d else range(RWKV_STEP_CHUNKS)
        for j in order:
            outs = []
            for h in range(RWKV_HEADS):
                u = unit[(d, j, h)]
                h16 = b16(h_cur[h])
                outs.append(_dot(q_eff[u], h16) + o_intra[u])
                h_cur[h] = decay_col[u] * h_cur[h] - mm(bt_mw[u][:, :hd], h16) + d_mat[u]
            o_ref[pl.ds(j * n, n), :] = jnp.concatenate(outs, axis=1)
        for h in range(RWKV_HEADS):
            h_scr[d * RWKV_HEADS + h] = h_cur[h]

    @pl.when(c == per - 1)
    def _store_state():
        sfin_ref[0] = h_scr[...]


def rwkv_chunks(lay, r, v, alpha, lw0, be0, kd0, lw1, be1, kd1, s0):
    n, dim = r.shape
    rows = RWKV_STEP_CHUNKS * RWKV_CHUNK
    n_seq = lay.bp + lay.bs
    assert lay.tp % rows == 0 and lay.ts % rows == 0

    def fwd(s):
        return (s, 0)

    def bwd(s):
        _, _, c, per = lay.seq_of_tile(s, rows)
        return (s - c + (per - 1 - c), 0)

    def seq(s):
        is_p, q, _, _ = lay.seq_of_tile(s, rows)
        return (jnp.where(is_p, q, lay.bp + q), 0, 0, 0)

    state_block = (1, 2 * RWKV_HEADS, RWKV_HD, RWKV_HD)
    return pl.pallas_call(
        functools.partial(_rwkv_chunk_kernel, lay=lay),
        grid=(n // rows,),
        in_specs=[pl.BlockSpec((rows, dim), fwd)] * 6 + [pl.BlockSpec((rows, dim), bwd)] * 6
        + [pl.BlockSpec(state_block, seq)],
        out_specs=[pl.BlockSpec((rows, dim), fwd), pl.BlockSpec((rows, dim), bwd), pl.BlockSpec(state_block, seq)],
        out_shape=[jax.ShapeDtypeStruct((n, dim), F32), jax.ShapeDtypeStruct((n, dim), F32),
                   jax.ShapeDtypeStruct((n_seq,) + state_block[1:], F32)],
        scratch_shapes=[pltpu.VMEM(state_block[1:], F32)],
        compiler_params=_cparams(("arbitrary",)),
        name="rwkv_chunks",
    )(r, v, alpha, lw0, be0, kd0, r, v, alpha, lw1, be1, kd1, s0)


def _even_out_kernel(x_ref, att_ref, of_ref, ob_ref, bonus_ref, gate_ref, lng_ref, lnb_ref, ones_ref,
                     wa_ref, wr_ref, g1_ref, o_ref):
    ones = ones_ref[...]
    osum = of_ref[...] + ob_ref[...]
    mean = _split_dot(osum, ones) * (1.0 / RWKV_HD)
    cen = osum - mean
    var = _split_dot(cen * cen, ones) * (1.0 / RWKV_HD)
    on = cen * lax.rsqrt(var + GN_EPS) * lng_ref[...] + lnb_ref[...]
    rw = (on + bonus_ref[...]) * gate_ref[...]
    y = _dot(att_ref[...].astype(BF16), wa_ref[...]) + _dot(rw.astype(BF16), wr_ref[...])
    o_ref[...] = x_ref[...] + g1_ref[0] * y


def even_out(lay, x, att, o_f, o_b, bonus, gate, ln_g, ln_b, w_out, mods):
    n, d = x.shape
    tile = TOKEN_TILE
    dim = RWKV_DIM
    const = lambda shape: pl.BlockSpec(shape, lambda i: (0,) * len(shape))
    tok = lambda w: pl.BlockSpec((tile, w), lambda i: (i, 0))
    w16 = w_out.astype(BF16)
    return pl.pallas_call(
        _even_out_kernel,
        grid=(n // tile,),
        in_specs=[tok(d), tok(ATT_Q), tok(dim), tok(dim), tok(dim), tok(dim),
                  const((1, dim)), const((1, dim)), const((dim, dim)),
                  const((ATT_Q, d)), const((dim, d)), _mod_spec(lay, 2, tile)],
        out_specs=tok(d),
        out_shape=jax.ShapeDtypeStruct((n, d), F32),
        compiler_params=_cparams(("arbitrary",)),
        name="even_out",
    )(x, att, o_f, o_b, bonus, gate, ln_g.reshape(1, dim), ln_b.reshape(1, dim), _block_ones(dim, RWKV_HD),
      w16[:ATT_Q], w16[ATT_Q:], mods)


def _gelu_tanh(x):
    return 0.5 * x * (1.0 + jnp.tanh(float(np.sqrt(2.0 / np.pi)) * (x + 0.044715 * (x * x * x))))


def _softplus(x):
    return jnp.maximum(x, 0.0) + jnp.log(1.0 + jnp.exp(-jnp.abs(x)))


def _odd_in_kernel(xm_ref, xprev_ref, xnext_ref, g_ref, sh_ref, sc_ref, w_ref, cw_ref, cb_ref,
                   gate_ref, u_ref, *, lay):
    i = pl.program_id(0)
    _, _, j, per = lay.seq_of_tile(i, TOKEN_TILE)
    tile = TOKEN_TILE
    y = _dot(_norm_mod(xm_ref[...], g_ref[...], sh_ref[0], sc_ref[0]).astype(BF16), w_ref[...])
    gate_ref[...] = _gelu_tanh(y[:, :D_RNN])
    x = y[:, D_RNN:]
    yh = _halo_proj(xprev_ref, xnext_ref, g_ref, sh_ref, sc_ref, w_ref[:, D_RNN:])
    first = j == 0
    last = j == per - 1
    row = lax.broadcasted_iota(I32, x.shape, 0)
    p1 = jnp.where(first, 0.0, yh[HALO - 1:HALO, :])
    p2 = jnp.where(first, 0.0, yh[HALO - 2:HALO - 1, :])
    n1 = jnp.where(last, 0.0, yh[HALO:HALO + 1, :])
    xm1 = jnp.where(row == 0, p1, pltpu.roll(x, 1, 0))
    xm2 = jnp.where(row == 0, p2, jnp.where(row == 1, p1, pltpu.roll(x, 2, 0)))
    xp1 = jnp.where(row == tile - 1, n1, pltpu.roll(x, tile - 1, 0))
    u_ref[...] = (cb_ref[...] + xm2 * cw_ref[0:1, :] + xm1 * cw_ref[1:2, :] + x * cw_ref[2:3, :]
                  + xp1 * cw_ref[3:4, :])


def odd_in(lay, x, g, mods, w_in, conv_w, conv_b):
    n, d = x.shape
    tile = TOKEN_TILE
    dr = D_RNN
    const = lambda shape: pl.BlockSpec(shape, lambda i: (0,) * len(shape))
    tok = lambda w: pl.BlockSpec((tile, w), lambda i: (i, 0))
    return pl.pallas_call(
        functools.partial(_odd_in_kernel, lay=lay),
        grid=(n // tile,),
        in_specs=[tok(d)] + _halo_specs(n, d, tile) + [
            const((1, d)), _mod_spec(lay, 0, tile), _mod_spec(lay, 1, tile), const((d, 2 * dr)),
            const((CONV_W, dr)), const((1, dr))],
        out_specs=[tok(dr)] * 2,
        out_shape=[jax.ShapeDtypeStruct((n, dr), F32)] * 2,
        compiler_params=_cparams(("arbitrary",)),
        name="odd_in",
    )(x, x, x, g.reshape(1, d), mods, mods, w_in.astype(BF16), conv_w, conv_b.reshape(1, dr))


def _lru_coefficients(u, d, wbd_ref, ba_ref, bx_ref, lam_ref):
    u16 = u.astype(BF16)
    n_grp = D_RNN // LRU_GROUP
    z = [_dot(u16[:, g * LRU_GROUP:(g + 1) * LRU_GROUP], wbd_ref[d, g]) for g in range(n_grp)]
    pick = lambda m: jnp.concatenate([zg[:, m * LRU_GROUP:(m + 1) * LRU_GROUP] for zg in z], axis=1)
    ga = _sigmoid(pick(0) + ba_ref[d:d + 1, :])
    gx = _sigmoid(pick(1) + bx_ref[d:d + 1, :])
    log_a = -LRU_C * ga * _softplus(-lam_ref[d:d + 1, :])
    return jnp.exp(log_a), jnp.sqrt(1.0 - jnp.exp(2.0 * log_a)) * gx * u


def _lru_scan_kernel(uf_ref, ub_ref, wbd_ref, ba_ref, bx_ref, lam_ref, h0_ref, hf_ref, hb_ref,
                     af_ref, bf_ref, ab_ref, bb_ref, carry, *, lay):
    i = pl.program_id(0)
    _, _, j, _ = lay.seq_of_tile(i, TOKEN_TILE)
    tile = TOKEN_TILE

    @pl.when(j == 0)
    def _load_state():
        carry[...] = h0_ref[0]

    af_ref[...], bf_ref[...] = _lru_coefficients(uf_ref[...], 0, wbd_ref, ba_ref, bx_ref, lam_ref)
    ab_ref[...], bb_ref[...] = _lru_coefficients(ub_ref[...], 1, wbd_ref, ba_ref, bx_ref, lam_ref)

    def step(t, hs):
        hf, hb = hs
        tb = tile - 1 - t
        hf = af_ref[pl.ds(t, 1), :] * hf + bf_ref[pl.ds(t, 1), :]
        hb = ab_ref[pl.ds(tb, 1), :] * hb + bb_ref[pl.ds(tb, 1), :]
        hf_ref[pl.ds(t, 1), :] = hf
        hb_ref[pl.ds(tb, 1), :] = hb
        return hf, hb

    hf, hb = lax.fori_loop(0, tile, step, (carry[0:1, :], carry[1:2, :]))
    carry[0:1, :] = hf
    carry[1:2, :] = hb


def lru_scan(lay, u, wa, ba, wx, bx, lam, h0):
    n, dr = u.shape
    tile = TOKEN_TILE
    per = LRU_GROUP // LRU_BS
    n_grp = dr // LRU_GROUP

    def block_diag(w):
        w = w.reshape(n_grp, per, LRU_BS, LRU_BS)
        eye = jnp.eye(per, dtype=w.dtype)
        return jnp.einsum("gpcd,pq->gpcqd", w, eye).reshape(n_grp, LRU_GROUP, LRU_GROUP)

    wbd = jnp.stack([jnp.concatenate([block_diag(wa[d]), block_diag(wx[d])], axis=2) for d in range(2)]).astype(BF16)
    const = lambda shape: pl.BlockSpec(shape, lambda i: (0,) * len(shape))

    def fwd(i):
        return (i, 0)

    def bwd(i):
        _, _, j, per = lay.seq_of_tile(i, tile)
        return (i - j + (per - 1 - j), 0)

    def seq(i):
        is_p, q, _, _ = lay.seq_of_tile(i, tile)
        return (jnp.where(is_p, q, lay.bp + q), 0, 0)

    return pl.pallas_call(
        functools.partial(_lru_scan_kernel, lay=lay),
        grid=(n // tile,),
        in_specs=[pl.BlockSpec((tile, dr), fwd), pl.BlockSpec((tile, dr), bwd),
                  const((2, n_grp, LRU_GROUP, 2 * LRU_GROUP)), const((2, dr)), const((2, dr)), const((2, dr)),
                  pl.BlockSpec((1, 2, dr), seq)],
        out_specs=[pl.BlockSpec((tile, dr), fwd), pl.BlockSpec((tile, dr), bwd)],
        out_shape=[jax.ShapeDtypeStruct((n, dr), F32)] * 2,
        scratch_shapes=[pltpu.VMEM((tile, dr), F32)] * 4 + [pltpu.VMEM((2, dr), F32)],
        compiler_params=_cparams(("arbitrary",)),
        name="lru_scan",
    )(u, u, wbd, ba, bx, lam, h0)


def _odd_out_kernel(x_ref, gate_ref, hf_ref, hb_ref, w_ref, g1_ref, o_ref):
    y = _dot((gate_ref[...] * (hf_ref[...] + hb_ref[...])).astype(BF16), w_ref[...])
    o_ref[...] = x_ref[...] + g1_ref[0] * y


def odd_out(lay, x, gate, hf, hb, w_out, mods):
    n, d = x.shape
    tile = TOKEN_TILE
    tok = lambda w: pl.BlockSpec((tile, w), lambda i: (i, 0))
    return pl.pallas_call(
        _odd_out_kernel,
        grid=(n // tile,),
        in_specs=[tok(d), tok(D_RNN), tok(D_RNN), tok(D_RNN), pl.BlockSpec((D_RNN, d), lambda i: (0, 0)),
                  _mod_spec(lay, 2, tile)],
        out_specs=tok(d),
        out_shape=jax.ShapeDtypeStruct((n, d), F32),
        compiler_params=_cparams(("arbitrary",)),
        name="odd_out",
    )(x, gate, hf, hb, w_out.astype(BF16), mods)


def _pack_bf16_pairs(x):
    n = x.shape[1] // 2
    hi = pltpu.bitcast(x[:, :n].astype(BF16).astype(F32), jnp.uint32)
    lo = pltpu.bitcast(x[:, n:].astype(BF16).astype(F32), jnp.uint32)
    return hi | (lo >> 16)


def _unpack_bf16_pairs(u):
    a = pltpu.bitcast(u & jnp.uint32(0xFFFF0000), F32)
    b = pltpu.bitcast(u << 16, F32)
    return jnp.concatenate([a, b], axis=1).astype(BF16)


def _router_kernel(x_ref, g_ref, sh_ref, sc_ref, rw_ref, rb_ref, before_ref,
                   h_ref, idx_ref, gate_ref, rank_ref, count_ref, cnt_scr):
    @pl.when(pl.program_id(0) == 0)
    def _reset():
        cnt_scr[...] = jnp.zeros_like(cnt_scr)

    h = _norm_mod(x_ref[...], g_ref[...], sh_ref[0], sc_ref[0])
    h_ref[...] = _pack_bf16_pairs(h)
    logits = lax.dot_general(rw_ref[...], h, (((1,), (1,)), ((), ())), preferred_element_type=F32,
                             precision=HIGHEST) + rb_ref[...]
    e_id = lax.broadcasted_iota(I32, logits.shape, 0)
    vals, ids, hots = [], [], []
    for _ in range(TOP_K):
        m = jnp.max(logits, axis=0, keepdims=True)
        pick = jnp.min(jnp.where(logits == m, e_id, N_EXPERTS), axis=0, keepdims=True)
        hot = e_id == pick
        vals.append(m)
        ids.append(pick)
        hots.append(jnp.where(hot, 1.0, 0.0))
        logits = jnp.where(hot, -jnp.inf, logits)
    top = jnp.concatenate(vals, axis=0)
    p = jnp.exp(top - top[0:1, :])
    gate_ref[...] = p / jnp.sum(p, axis=0, keepdims=True)
    idx_ref[...] = jnp.concatenate(ids, axis=0)
    hot_all = hots[0] + hots[1] + hots[2] + hots[3]
    seen = cnt_scr[:, 0:1] + _dot(hot_all.astype(BF16), before_ref[...])
    rank_ref[...] = jnp.concatenate([jnp.sum(hk * seen, axis=0, keepdims=True) for hk in hots],
                                    axis=0).astype(I32)
    cnt_scr[...] = cnt_scr[...] + jnp.sum(hot_all, axis=1, keepdims=True)
    count_ref[...] = cnt_scr[...]


def router(lay, x, g, mods, rw, rb):
    n, d = x.shape
    tile = TOKEN_TILE
    before = jnp.asarray(np.triu(np.ones((tile, tile), np.float32), 1), dtype=BF16)
    per_tok = pl.BlockSpec((TOP_K, tile), lambda i: (0, i))
    return pl.pallas_call(
        _router_kernel,
        grid=(n // tile,),
        in_specs=[pl.BlockSpec((tile, d), lambda i: (i, 0)), pl.BlockSpec((1, d), lambda i: (0, 0)),
                  _mod_spec(lay, 3, tile), _mod_spec(lay, 4, tile),
                  pl.BlockSpec((N_EXPERTS, d), lambda i: (0, 0)), pl.BlockSpec((N_EXPERTS, 1), lambda i: (0, 0)),
                  pl.BlockSpec((tile, tile), lambda i: (0, 0))],
        out_specs=[pl.BlockSpec((tile, d // 2), lambda i: (i, 0)), per_tok, per_tok, per_tok,
                   pl.BlockSpec((N_EXPERTS, 128), lambda i: (0, 0))],
        out_shape=[jax.ShapeDtypeStruct((n, d // 2), jnp.uint32), jax.ShapeDtypeStruct((TOP_K, n), I32),
                   jax.ShapeDtypeStruct((TOP_K, n), F32), jax.ShapeDtypeStruct((TOP_K, n), I32),
                   jax.ShapeDtypeStruct((N_EXPERTS, 128), F32)],
        scratch_shapes=[pltpu.VMEM((N_EXPERTS, 128), F32)],
        compiler_params=_cparams(("arbitrary",)),
        name="router",
    )(x, g.reshape(1, d), mods, mods, rw.T, rb.reshape(N_EXPERTS, 1), before)


def _dispatch_kernel(dest_ref, src_ref, o_ref, slot_ref, *, n_tok):
    @pl.when(pl.program_id(0) == 0)
    def _invert():
        def clear(i, _):
            slot_ref[i] = TOP_K * n_tok + i % MOE_TILE
            return 0

        lax.fori_loop(0, slot_ref.shape[0], clear, 0, unroll=8)
        for k in range(TOP_K):
            def put(t, _, k=k):
                slot_ref[dest_ref[k * n_tok + t]] = t * TOP_K + k
                return 0

            lax.fori_loop(0, n_tok, put, 0, unroll=8)

    base = pl.program_id(0) * MOE_TILE

    def copy_row(r, _):
        tok = jnp.minimum(slot_ref[base + r] // TOP_K, n_tok - 1)
        o_ref[pl.ds(r, 1), :] = src_ref[pl.ds(tok, 1), :]
        return 0

    lax.fori_loop(0, MOE_TILE, copy_row, 0, unroll=8)


def dispatch_rows(src, dest, n_out):
    n, d = src.shape
    return pl.pallas_call(
        functools.partial(_dispatch_kernel, n_tok=n),
        grid_spec=pltpu.PrefetchScalarGridSpec(
            num_scalar_prefetch=1,
            grid=(n_out // MOE_TILE,),
            in_specs=[pl.BlockSpec((n, d), lambda i, dest: (0, 0), pipeline_mode=pl.Buffered(1))],
            out_specs=[pl.BlockSpec((MOE_TILE, d), lambda i, dest: (i, 0)),
                       pl.BlockSpec(memory_space=pltpu.SMEM)],
        ),
        out_shape=[jax.ShapeDtypeStruct((n_out, d), src.dtype), jax.ShapeDtypeStruct((n_out,), I32)],
        compiler_params=_cparams(("arbitrary",)),
        name="moe_dispatch",
    )(dest, src)


def _expert_weight_copies(w1_hbm, w2_hbm, w1_buf, w2_buf, sems, row, slot):
    return (pltpu.make_async_copy(w1_hbm.at[row], w1_buf.at[slot], sems.at[slot, 0]),
            pltpu.make_async_copy(w2_hbm.at[row], w2_buf.at[slot], sems.at[slot, 1]))


def _expert_kernel(blk_e_ref, first_ref, slot_ref, next_e_ref, n_used_ref, row_slot_ref,
                   x_ref, w1_hbm, b1_ref, w2_hbm, b2_ref, y_hbm,
                   w1_buf, w2_buf, w1_scr, w2_scr, y_buf, sems, y_sems, *, layer):
    i = pl.program_id(0)
    n_used = n_used_ref[0]
    used = i < n_used
    copies = functools.partial(_expert_weight_copies, w1_hbm, w2_hbm, w1_buf, w2_buf, sems)

    def row_copy(step, r):
        s = step % 2
        return pltpu.make_async_copy(y_buf.at[s, pl.ds(r, 1)], y_hbm.at[pl.ds(row_slot_ref[step * MOE_TILE + r], 1)],
                                     y_sems.at[s])

    def start_rows(step):
        for r in range(MOE_TILE):
            row_copy(step, r).start(priority=r % 2)

    def wait_rows(step):
        s = step % 2
        for r in range(MOE_TILE):
            pltpu.make_async_copy(y_buf.at[s, pl.ds(r, 1)], y_hbm.at[pl.ds(0, 1)], y_sems.at[s]).wait()

    def compute():
        hb = _dot(_unpack_bf16_pairs(x_ref[...]), w1_scr[...]) + b1_ref[0]
        glu = jnp.minimum(hb[:, :D_EXPERT], SWIGLU_LIMIT)
        lin = jnp.clip(hb[:, D_EXPERT:], -SWIGLU_LIMIT, SWIGLU_LIMIT)
        act = glu * _sigmoid(SWIGLU_ALPHA * glu) * (lin + 1.0)
        y_buf[i % 2] = _dot(act.astype(BF16), w2_scr[...]) + b2_ref[0]

    @pl.when(jnp.logical_and(used, first_ref[i] == 1))
    def _switch_expert():
        slot = slot_ref[i]

        @pl.when(i == 0)
        def _fetch_first():
            for cp in copies(layer * N_EXPERTS + blk_e_ref[0], 0):
                cp.start()

        nxt = next_e_ref[i]

        @pl.when(nxt >= 0)
        def _prefetch_next():
            for cp in copies(layer * N_EXPERTS + nxt, 1 - slot):
                cp.start(priority=1)

        for cp in copies(0, slot):
            cp.wait()
        w1_scr[...] = w1_buf[slot].astype(BF16)
        w2_scr[...] = w2_buf[slot].astype(BF16)

    @pl.when(jnp.logical_and(used, i >= 2))
    def _free_buffer():
        wait_rows(i - 2)

    @pl.when(i == 0)
    def _clear_spare_rows():
        y_buf[1] = jnp.zeros(y_buf.shape[1:], F32)
        spare = pltpu.make_async_copy(y_buf.at[1], y_hbm.at[pl.ds(y_hbm.shape[0] - MOE_TILE, MOE_TILE)], y_sems.at[1])
        spare.start()
        spare.wait()

    @pl.when(jnp.logical_and(used, i == 0))
    def _first_block():
        compute()

    @pl.when(jnp.logical_and(used, i > 0))
    def _block():
        start_rows(i - 1)
        compute()

    @pl.when(i == n_used)
    def _last_block_rows():
        @pl.when(i >= 2)
        def _():
            wait_rows(i - 2)

        start_rows(i - 1)

    @pl.when(i == pl.num_programs(0) - 1)
    def _drain():
        wait_rows(n_used - 1)


def experts(x_sorted, row_slot, n_slots, blk_expert, count, n_used, layer, w1, b1, w2, b2):
    n_rows = x_sorted.shape[0]
    n_blk = n_rows // MOE_TILE
    d, de2 = w1.shape[1:]
    de = w2.shape[1]
    first = jnp.concatenate([jnp.ones((1,), I32), (blk_expert[1:] != blk_expert[:-1]).astype(I32)])
    slot = (jnp.cumsum(first) - 1) % 2
    e_id = jnp.arange(N_EXPERTS, dtype=I32)
    later_present = jnp.logical_and(e_id[None, :] > e_id[:, None], count[None, :] > 0)
    next_present = jnp.min(jnp.where(later_present, e_id[None, :], N_EXPERTS), axis=1)
    next_e = jnp.where(next_present < N_EXPERTS, next_present, -1)[blk_expert]
    bmap = lambda i, be, *_: (layer * N_EXPERTS + be[i], 0, 0)
    return pl.pallas_call(
        functools.partial(_expert_kernel, layer=layer),
        grid_spec=pltpu.PrefetchScalarGridSpec(
            num_scalar_prefetch=6,
            grid=(n_blk + 1,),
            in_specs=[pl.BlockSpec((MOE_TILE, d // 2), lambda i, *_: (jnp.minimum(i, n_blk - 1), 0)),
                      pl.BlockSpec(memory_space=pl.ANY), pl.BlockSpec((1, 1, de2), bmap),
                      pl.BlockSpec(memory_space=pl.ANY), pl.BlockSpec((1, 1, d), bmap)],
            out_specs=pl.BlockSpec(memory_space=pl.ANY),
            scratch_shapes=[pltpu.VMEM((2, d, de2), F32), pltpu.VMEM((2, de, d), F32),
                            pltpu.VMEM((d, de2), BF16), pltpu.VMEM((de, d), BF16),
                            pltpu.VMEM((2, MOE_TILE, d), F32),
                            pltpu.SemaphoreType.DMA((2, 2)), pltpu.SemaphoreType.DMA((2,))],
        ),
        out_shape=jax.ShapeDtypeStruct((n_slots + MOE_TILE, d), F32),
        compiler_params=_cparams(("arbitrary",)),
        name="moe_experts",
    )(blk_expert, first, slot.astype(I32), next_e.astype(I32), n_used, row_slot, x_sorted, w1, b1, w2, b2)


def _combine_kernel(y_ref, x_ref, gate_ref, g2_ref, o_ref):
    d = x_ref.shape[1]
    gate = gate_ref[...]
    acc = y_ref[:, 0:d] * gate[:, 0:1]
    for k in range(1, TOP_K):
        acc = acc + y_ref[:, k * d:(k + 1) * d] * gate[:, k:k + 1]
    o_ref[...] = x_ref[...] + g2_ref[0] * acc


def combine(lay, x, y_slots, gate, mods):
    n, d = x.shape
    tile = TOKEN_TILE
    y4 = y_slots.reshape(y_slots.shape[0] // TOP_K, TOP_K * d)
    return pl.pallas_call(
        _combine_kernel,
        grid=(n // tile,),
        in_specs=[pl.BlockSpec((tile, TOP_K * d), lambda i: (i, 0)),
                  pl.BlockSpec((tile, d), lambda i: (i, 0)),
                  pl.BlockSpec((tile, TOP_K), lambda i: (i, 0)),
                  _mod_spec(lay, 5, tile)],
        out_specs=pl.BlockSpec((tile, d), lambda i: (i, 0)),
        out_shape=jax.ShapeDtypeStruct((n, d), F32),
        compiler_params=_cparams(("arbitrary",)),
        name="moe_combine",
    )(y4, x, gate, mods)


def moe_layer(lay, x, g, mods, rw, rb, layer, w1, b1, w2, b2):
    n, d = x.shape
    h, idx_t, gate_t, rank_t, count = router(lay, x, g, mods, rw, rb)
    expert = idx_t.reshape(-1)
    rank = rank_t.reshape(-1)
    n_asg = expert.shape[0]
    count = count[:, 0].astype(I32)
    padded = (count + MOE_TILE - 1) // MOE_TILE * MOE_TILE
    pend = jnp.cumsum(padded)
    dest = ((pend - padded)[expert] + rank).astype(I32)
    n_blk = n_asg // MOE_TILE + N_EXPERTS
    blk_start = jnp.arange(n_blk + 1, dtype=I32) * MOE_TILE
    blk_expert = jnp.minimum(jnp.sum((pend[None, :] <= blk_start[:, None]).astype(I32), axis=1), N_EXPERTS - 1)
    n_used = (pend[-1] // MOE_TILE).astype(I32).reshape(1)
    x_sorted, row_slot = dispatch_rows(h, dest, n_blk * MOE_TILE)
    y_slots = experts(x_sorted, row_slot, n_asg, blk_expert, count, n_used, layer, w1, b1, w2, b2)
    return combine(lay, x, y_slots, gate_t.T, mods)


def kernel(x_prompt, x_sample, cache_attn_k, cache_attn_v, state_rwkv, state_lru, c, c_ctx,
           mod_w, mod_b, norm1_g, norm2_g, ev_w_in, ev_w_out, q_norm_g, k_norm_g,
           rwkv_mu, rwkv_w0, rwkv_w_up, rwkv_a0, rwkv_a_up, rwkv_g_up, rwkv_k_k, rwkv_k_a,
           rwkv_r_k, rwkv_ln_g, rwkv_ln_b, od_w_in, od_w_out, conv_w, conv_b,
           lru_wa, lru_ba, lru_wx, lru_bx, lru_lambda,
           router_w, router_b, exp_w1, exp_b1, exp_w2, exp_b2):
    bp, tp, d = x_prompt.shape
    bs, ts, _ = x_sample.shape
    depth = mod_w.shape[0]
    lay = Layout(bp, tp, bs, ts)
    assert bs < MOD_ROWS and tp % TOKEN_TILE == 0 and ts % TOKEN_TILE == 0 and d == D_MODEL
    n_p = lay.n_p

    x = jnp.concatenate([x_prompt.reshape(n_p, d), x_sample.reshape(bs * ts, d)], axis=0)
    cvec = jnp.zeros((MOD_ROWS, d), F32).at[:bs].set(c).at[bs].set(c_ctx)
    mods_all = modulation(cvec, mod_w, mod_b)
    rope = rope_tables(ts)
    n_le = depth * N_EXPERTS
    expert_prm = (exp_w1.reshape(n_le, d, -1), exp_b1.reshape(n_le, 1, -1),
                  exp_w2.reshape(n_le, -1, d), exp_b2.reshape(n_le, 1, d))

    new_k, new_v, new_rw, new_lru = [], [], [], []
    for l in range(depth):
        j = l // 2
        mods = mods_all[l]
        if l % 2 == 0:
            prm = (rwkv_mu[j], rwkv_w0[j], rwkv_w_up[j], rwkv_a0[j], rwkv_a_up[j], rwkv_g_up[j],
                   rwkv_k_k[j], rwkv_k_a[j], rwkv_r_k[j].reshape(-1))
            qkv, r, v, alpha, lw0, be0, kd0, lw1, be1, kd1, bonus, gate = even_in(
                lay, x, norm1_g[l], mods, ev_w_in[j], prm)
            att_p, k_norm = attention(qkv, 0, bp, tp, q_norm_g[j], k_norm_g[j])
            cache = (cache_attn_k[:, j].reshape(bs, -1, ATT_KV), cache_attn_v[:, j].reshape(bs, -1, ATT_KV))
            (att_s,) = attention(qkv, n_p, bs, ts, q_norm_g[j], k_norm_g[j], cache=cache, rope=rope)
            att = jnp.concatenate([att_p, att_s], axis=0)
            s_lat = jnp.swapaxes(state_rwkv[:, j], -1, -2).reshape(bs, 2 * RWKV_HEADS, RWKV_HD, RWKV_HD)
            s0 = jnp.concatenate([jnp.zeros((bp,) + s_lat.shape[1:], F32), s_lat], axis=0)
            o_f, o_b, s_fin = rwkv_chunks(lay, r, v, alpha, lw0, be0, kd0, lw1, be1, kd1, s0)
            x = even_out(lay, x, att, o_f, o_b, bonus, gate, rwkv_ln_g[j], rwkv_ln_b[j], ev_w_out[j], mods)
            new_k.append(k_norm.reshape(bp, tp, ATT_KV_HEADS, HEAD_DIM))
            new_v.append(qkv[:n_p, ATT_Q + ATT_KV:].reshape(bp, tp, ATT_KV_HEADS, HEAD_DIM))
            new_rw.append(jnp.swapaxes(s_fin[:bp].reshape(bp, 2, RWKV_HEADS, RWKV_HD, RWKV_HD), -1, -2))
        else:
            gate, u = odd_in(lay, x, norm1_g[l], mods, od_w_in[j], conv_w[j], conv_b[j])
            h0 = jnp.concatenate([jnp.zeros((bp, 2, D_RNN), F32), state_lru[:, j]], axis=0)
            hf, hb = lru_scan(lay, u, lru_wa[j], lru_ba[j], lru_wx[j], lru_bx[j], lru_lambda[j], h0)
            x = odd_out(lay, x, gate, hf, hb, od_w_out[j], mods)
            hf_p = hf[:n_p].reshape(bp, tp, D_RNN)
            hb_p = hb[:n_p].reshape(bp, tp, D_RNN)
            new_lru.append(jnp.stack([hf_p[:, -1], hb_p[:, 0]], axis=1))
        x = moe_layer(lay, x, norm2_g[l], mods, router_w[l], router_b[l], l, *expert_prm)

    y_prompt = x[:n_p].reshape(bp, tp, d)
    y_sample = x[n_p:].reshape(bs, ts, d)
    return (y_prompt, y_sample, jnp.stack(new_k, axis=1), jnp.stack(new_v, axis=1), jnp.stack(new_rw, axis=1),
            jnp.stack(new_lru, axis=1))
```

```python
import functools
from typing import NamedTuple

import numpy as np
import jax
import jax.numpy as jnp
from jax import lax
from jax.experimental import pallas as pl
from jax.experimental.pallas import tpu as pltpu

F32 = jnp.float32
BF16 = jnp.bfloat16
I32 = jnp.int32
HIGHEST = lax.Precision.HIGHEST

D_MODEL = 1024
EPS = 1e-6
GRID_W = 64
ATT_HEADS = 8
ATT_KV_HEADS = 2
HEAD_DIM = 64
GQA_GROUP = ATT_HEADS // ATT_KV_HEADS
ATT_Q = ATT_HEADS * HEAD_DIM
ATT_KV = ATT_KV_HEADS * HEAD_DIM
ROPE_THETA = 10000.0
LOG2_E = 1.4426950408889634
RWKV_HEADS = 8
RWKV_HD = 64
RWKV_DIM = RWKV_HEADS * RWKV_HD
DECAY_RANK = 64
ICLR_RANK = 64
GATE_RANK = 128
RWKV_COLS = 3 * RWKV_DIM + 2 * DECAY_RANK + 2 * ICLR_RANK + GATE_RANK
GN_EPS = 64e-5
D_RNN = D_MODEL
LRU_BS = 64
LRU_GROUP = 256
CONV_W = 4
LRU_C = 8.0
N_EXPERTS = 32
TOP_K = 4
D_EXPERT = D_MODEL
SWIGLU_LIMIT = 7.0
SWIGLU_ALPHA = 1.702

TOKEN_TILE = 256
RWKV_CHUNK = 64
RWKV_STEP_CHUNKS = 4
INV_BLOCK = 16
ATT_Q_TILE = 256
MOE_TILE = 256
HALO = 8
MOD_ROWS = 16
VMEM_LIMIT = 56 * 1024 * 1024


class Layout(NamedTuple):
    bp: int
    tp: int
    bs: int
    ts: int

    @property
    def n_p(self):
        return self.bp * self.tp

    @property
    def n(self):
        return self.bp * self.tp + self.bs * self.ts

    def tiles(self, tile):
        return self.n // tile

    def seq_of_tile(self, i, tile):
        npt = self.n_p // tile
        is_p = i < npt
        ii = jnp.where(is_p, i, i - npt)
        per = jnp.where(is_p, self.tp // tile, self.ts // tile)
        return is_p, ii // per, ii % per, per

    def mod_row(self, i, tile):
        is_p, seq, _, _ = self.seq_of_tile(i, tile)
        return jnp.where(is_p, self.bs, seq)


def _cparams(sem):
    return pltpu.CompilerParams(dimension_semantics=sem, vmem_limit_bytes=VMEM_LIMIT)


def _dot(a, b):
    return jnp.dot(a, b, preferred_element_type=F32)


def _dot_nt(a, b):
    return lax.dot_general(a, b, (((1,), (1,)), ((), ())), preferred_element_type=F32)


def _dot_tn(a, b):
    return lax.dot_general(a, b, (((0,), (0,)), ((), ())), preferred_element_type=F32)


def _split_dot(x, m01):
    hi = x.astype(BF16)
    lo = (x - hi.astype(F32)).astype(BF16)
    return _dot(hi, m01) + _dot(lo, m01)


def _split_dot_left(m01, x):
    hi = x.astype(BF16)
    lo = (x - hi.astype(F32)).astype(BF16)
    return _dot(m01, hi) + _dot(m01, lo)


def _sigmoid(x):
    return 1.0 / (1.0 + jnp.exp(-x))


def _block_ones(n, blk):
    idx = np.arange(n) // blk
    return jnp.asarray((idx[:, None] == idx[None, :]).astype(np.float32), dtype=BF16)


def _mod_kernel(c_ref, w_ref, b_ref, o_ref):
    c = c_ref[...]
    s = c * _sigmoid(c)
    o_ref[0] = jnp.dot(s, w_ref[0], preferred_element_type=F32, precision=HIGHEST) + b_ref[0]


def modulation(cvec, mod_w, mod_b):
    depth, d, six_d = mod_w.shape
    nchunk = six_d // d
    out = pl.pallas_call(
        _mod_kernel,
        grid=(depth, nchunk),
        in_specs=[
            pl.BlockSpec((MOD_ROWS, d), lambda l, k: (0, 0)),
            pl.BlockSpec((1, d, d), lambda l, k: (l, 0, k)),
            pl.BlockSpec((1, 1, d), lambda l, k: (l, 0, k)),
        ],
        out_specs=pl.BlockSpec((1, MOD_ROWS, d), lambda l, k: (l, 0, k)),
        out_shape=jax.ShapeDtypeStruct((depth, MOD_ROWS, six_d), F32),
        compiler_params=_cparams(("arbitrary", "arbitrary")),
        name="modulation",
    )(cvec, mod_w, mod_b.reshape(depth, 1, six_d))
    return out.reshape(depth, MOD_ROWS * nchunk, 1, d)


def _mod_spec(lay, k, tile):
    return pl.BlockSpec((1, 1, D_MODEL), lambda i, *_: (lay.mod_row(i, tile) * 6 + k, 0, 0))


def _norm_mod(x, g, shift, scale):
    ms = jnp.mean(x * x, axis=-1, keepdims=True)
    h = x * lax.rsqrt(ms + EPS) * g
    return h * (1.0 + scale) + shift


def _head_norm(x, g, ones):
    ms = _split_dot(x * x, ones) * (1.0 / HEAD_DIM)
    return x * lax.rsqrt(ms + EPS) * g


def _rope(x, cos, sin_signed):
    n = x.shape[1]
    nxt = pltpu.roll(x, n - 1, 1)
    prv = pltpu.roll(x, 1, 1)
    lane = lax.broadcasted_iota(I32, x.shape, 1)
    swapped = jnp.where(lane % 2 == 0, nxt, prv)
    return x * cos + swapped * sin_signed


def _attn_kernel(*refs, t_len, n_ctx, rotary):
    if rotary:
        (qkv_ref, ck_ref, cv_ref, cos_ref, sin_ref, qg_ref, kg_ref, oq_ref, ok_ref,
         att_ref, k_scr, v_scr) = refs
    else:
        qkv_ref, qg_ref, kg_ref, oq_ref, ok_ref, att_ref, kn_ref, k_scr, v_scr = refs
    qi = pl.program_id(1)
    tq = ATT_Q_TILE

    @pl.when(qi == 0)
    def _prepare_keys():
        k = _head_norm(qkv_ref[:, ATT_Q:ATT_Q + ATT_KV], kg_ref[...], ok_ref[...])
        v = qkv_ref[:, ATT_Q + ATT_KV:ATT_Q + 2 * ATT_KV]
        if rotary:
            k = _rope(k, cos_ref[:, :ATT_KV], sin_ref[:, :ATT_KV])
            k_scr[0:n_ctx, :] = ck_ref[0].astype(BF16)
            v_scr[0:n_ctx, :] = cv_ref[0].astype(BF16)
        else:
            kn_ref[...] = k
        k_scr[n_ctx:n_ctx + t_len, :] = k.astype(BF16)
        v_scr[n_ctx:n_ctx + t_len, :] = v.astype(BF16)

    row0 = pl.multiple_of(qi * tq, tq)
    q = _head_norm(qkv_ref[pl.ds(row0, tq), 0:ATT_Q], qg_ref[...], oq_ref[...])
    if rotary:
        q = _rope(q, cos_ref[pl.ds(row0, tq), :], sin_ref[pl.ds(row0, tq), :])
    q = (q * (HEAD_DIM ** -0.5 * LOG2_E)).astype(BF16)
    outs = []
    for j in range(ATT_KV_HEADS):
        kj = k_scr[:, j * HEAD_DIM:(j + 1) * HEAD_DIM]
        vj = v_scr[:, j * HEAD_DIM:(j + 1) * HEAD_DIM]
        qs = jnp.concatenate(
            [q[:, (j * GQA_GROUP + g) * HEAD_DIM:(j * GQA_GROUP + g + 1) * HEAD_DIM] for g in range(GQA_GROUP)],
            axis=0)
        s = _dot_nt(qs, kj)
        p = jnp.exp2(s - jnp.max(s, axis=-1, keepdims=True))
        o = _dot(p.astype(BF16), vj) / jnp.sum(p, axis=-1, keepdims=True)
        outs.extend(o[g * tq:(g + 1) * tq] for g in range(GQA_GROUP))
    att_ref[...] = jnp.concatenate(outs, axis=1)


def attention(qkv, seq0, n_seq, t_len, q_g, k_g, cache=None, rope=None):
    rotary = cache is not None
    n_ctx = cache[0].shape[1] if rotary else 0
    blk0 = seq0 // t_len
    n_q = t_len // ATT_Q_TILE
    width = qkv.shape[1]
    qg = jnp.tile(q_g, ATT_HEADS).reshape(1, ATT_Q)
    kg = jnp.tile(k_g, ATT_KV_HEADS).reshape(1, ATT_KV)
    const = lambda shape: pl.BlockSpec(shape, lambda b, qi: (0,) * len(shape))
    in_specs = [pl.BlockSpec((t_len, width), lambda b, qi: (blk0 + b, 0))]
    args = [qkv]
    if rotary:
        in_specs += [pl.BlockSpec((1, n_ctx, ATT_KV), lambda b, qi: (b, 0, 0))] * 2
        in_specs += [const((t_len, ATT_Q))] * 2
        args += [cache[0], cache[1], rope[0], rope[1]]
    in_specs += [const((1, ATT_Q)), const((1, ATT_KV)), const((ATT_Q, ATT_Q)), const((ATT_KV, ATT_KV))]
    args += [qg, kg, _block_ones(ATT_Q, HEAD_DIM), _block_ones(ATT_KV, HEAD_DIM)]
    out_specs = [pl.BlockSpec((ATT_Q_TILE, ATT_Q), lambda b, qi: (b * n_q + qi, 0))]
    out_shape = [jax.ShapeDtypeStruct((n_seq * t_len, ATT_Q), F32)]
    if not rotary:
        out_specs.append(pl.BlockSpec((t_len, ATT_KV), lambda b, qi: (b, 0)))
        out_shape.append(jax.ShapeDtypeStruct((n_seq * t_len, ATT_KV), F32))
    return pl.pallas_call(
        functools.partial(_attn_kernel, t_len=t_len, n_ctx=n_ctx, rotary=rotary),
        grid=(n_seq, n_q),
        in_specs=in_specs,
        out_specs=out_specs,
        out_shape=out_shape,
        scratch_shapes=[pltpu.VMEM((n_ctx + t_len, ATT_KV), BF16), pltpu.VMEM((n_ctx + t_len, ATT_KV), BF16)],
        compiler_params=_cparams(("arbitrary", "arbitrary")),
        name="attention_latent" if rotary else "attention_context",
    )(*args)


def rope_tables(t_len):
    t = jnp.arange(t_len)
    pos = jnp.stack([t // GRID_W, t % GRID_W], axis=-1).astype(F32)
    n_freq = HEAD_DIM // 4
    inv = ROPE_THETA ** (-jnp.arange(n_freq, dtype=F32) / n_freq)
    ang = (pos[:, :, None] * inv).reshape(t_len, 2 * n_freq)
    cos = jnp.repeat(jnp.cos(ang), 2, axis=1)
    sin = jnp.repeat(jnp.sin(ang), 2, axis=1) * jnp.tile(jnp.asarray([-1.0, 1.0], F32), HEAD_DIM // 2)
    return jnp.tile(cos, (1, ATT_HEADS)), jnp.tile(sin, (1, ATT_HEADS))


def _shifted_rows(x, prev_row, next_row):
    m = x.shape[0]
    row = lax.broadcasted_iota(I32, x.shape, 0)
    prv = jnp.where(row == 0, prev_row, pltpu.roll(x, 1, 0))
    nxt = jnp.where(row == m - 1, next_row, pltpu.roll(x, m - 1, 0))
    return prv, nxt


def _halo_proj(xprev_ref, xnext_ref, g_ref, sh_ref, sc_ref, w_cols):
    halo = jnp.concatenate([xprev_ref[...], xnext_ref[...]], axis=0)
    return _dot(_norm_mod(halo, g_ref[...], sh_ref[0], sc_ref[0]).astype(BF16), w_cols)


def _even_in_kernel(xm_ref, xprev_ref, xnext_ref, g_ref, sh_ref, sc_ref, w_ref,
                    mu_ref, kk_ref, ka_ref, rk_ref, w0_ref, wup_ref, a0_ref, aup_ref, gup_ref, ones_ref,
                    qkv_ref, r_ref, v_ref, al_ref, lw0_ref, be0_ref, kd0_ref, lw1_ref, be1_ref, kd1_ref,
                    bonus_ref, gate_ref, *, lay):
    i = pl.program_id(0)
    _, _, j, per = lay.seq_of_tile(i, TOKEN_TILE)
    n_qkv = ATT_Q + 2 * ATT_KV
    y = _dot(_norm_mod(xm_ref[...], g_ref[...], sh_ref[0], sc_ref[0]).astype(BF16), w_ref[...])
    qkv_ref[...] = y[:, :n_qkv]
    x = y[:, n_qkv:]
    yh = _halo_proj(xprev_ref, xnext_ref, g_ref, sh_ref, sc_ref, w_ref[:, n_qkv:])
    prev_row = jnp.where(j == 0, 0.0, yh[HALO - 1:HALO, :])
    next_row = jnp.where(j == per - 1, 0.0, yh[HALO:HALO + 1, :])
    prv, nxt = _shifted_rows(x, prev_row, next_row)
    u = x + mu_ref[...] * (0.5 * (prv + nxt) - x)

    dim = RWKV_DIM
    r, k, v = u[:, :dim], u[:, dim:2 * dim], u[:, 2 * dim:3 * dim]
    o = 3 * dim
    wd = u[:, o:o + 2 * DECAY_RANK]
    o += 2 * DECAY_RANK
    ad = u[:, o:o + 2 * ICLR_RANK]
    o += 2 * ICLR_RANK
    gd = u[:, o:o + GATE_RANK]

    ones = ones_ref[...]
    kk = k * kk_ref[...]
    norm = jnp.sqrt(_split_dot(kk * kk, ones))
    alpha = kk / jnp.maximum(norm, 1e-12)
    r_ref[...] = r
    v_ref[...] = v
    al_ref[...] = alpha

    tanh_wd = jnp.tanh(wd).astype(BF16)
    ad16 = ad.astype(BF16)
    kd_sum = None
    for d, (lw_ref, be_ref, kd_ref) in enumerate(((lw0_ref, be0_ref, kd0_ref), (lw1_ref, be1_ref, kd1_ref))):
        w_log = w0_ref[d:d + 1, :] + _dot(tanh_wd[:, d * DECAY_RANK:(d + 1) * DECAY_RANK], wup_ref[d])
        lw_ref[...] = -_sigmoid(w_log) * float(np.exp(-0.5))
        a = _sigmoid(a0_ref[d:d + 1, :] + _dot(ad16[:, d * ICLR_RANK:(d + 1) * ICLR_RANK], aup_ref[d]))
        kd = k * (1.0 + (a - 1.0) * ka_ref[...])
        be_ref[...] = alpha * a
        kd_ref[...] = kd
        kd_sum = kd if kd_sum is None else kd_sum + kd
    bonus_ref[...] = _split_dot(r * kd_sum * rk_ref[...], ones) * v
    gate_ref[...] = _dot(_sigmoid(gd).astype(BF16), gup_ref[...])


def _halo_specs(n, d, tile):
    hb = tile // HALO
    n_halo = n // HALO
    return [pl.BlockSpec((HALO, d), lambda i: (jnp.maximum(i * hb - 1, 0), 0)),
            pl.BlockSpec((HALO, d), lambda i: (jnp.minimum((i + 1) * hb, n_halo - 1), 0))]


def even_in(lay, x, g, mods, w_in, prm):
    mu, w0, w_up, a0, a_up, g_up, k_k, k_a, r_k = prm
    n, d = x.shape
    cols = RWKV_COLS
    n_qkv = ATT_Q + 2 * ATT_KV
    tile = TOKEN_TILE
    dim = RWKV_DIM
    const = lambda shape: pl.BlockSpec(shape, lambda i: (0,) * len(shape))
    row = lambda a: a.reshape(1, -1)
    tok = lambda w: pl.BlockSpec((tile, w), lambda i: (i, 0))
    return pl.pallas_call(
        functools.partial(_even_in_kernel, lay=lay),
        grid=(n // tile,),
        in_specs=[tok(d)] + _halo_specs(n, d, tile) + [
            const((1, d)), _mod_spec(lay, 0, tile), _mod_spec(lay, 1, tile), const((d, n_qkv + cols)),
            const((1, cols)), const((1, dim)), const((1, dim)), const((1, dim)),
            const((2, dim)), const((2, DECAY_RANK, dim)), const((2, dim)), const((2, ICLR_RANK, dim)),
            const((GATE_RANK, dim)), const((dim, dim)),
        ],
        out_specs=[tok(n_qkv)] + [tok(dim)] * 11,
        out_shape=[jax.ShapeDtypeStruct((n, n_qkv), F32)] + [jax.ShapeDtypeStruct((n, dim), F32)] * 11,
        compiler_params=_cparams(("arbitrary",)),
        name="even_in",
    )(x, x, x, g.reshape(1, d), mods, mods, w_in.astype(BF16),
      row(mu), row(k_k), row(k_a), row(r_k), w0, w_up.astype(BF16), a0, a_up.astype(BF16),
      g_up.astype(BF16), _block_ones(dim, RWKV_HD))


def _each(fn, *lists):
    return [fn(*args) for args in zip(*lists)]


def _unit_triangular_inverses(l_mats, eye, diag_blocks):
    mm = lambda a, b: _dot(a.astype(BF16), b.astype(BF16))
    ld = [jnp.where(diag_blocks, l, 0.0) for l in l_mats]
    lo = _each(lambda l, d: l - d, l_mats, ld)
    x = [eye - d for d in ld]
    p = ld
    for _ in range(int(np.log2(INV_BLOCK)) - 1):
        p = _each(mm, p, p)
        x = _each(lambda xi, pi: xi + mm(xi, pi), x, p)
    nb = _each(mm, x, lo)
    y = [eye - m for m in nb]
    p = nb
    for _ in range(int(np.log2(RWKV_CHUNK // INV_BLOCK)) - 1):
        p = _each(mm, p, p)
        y = _each(lambda yi, pi: yi + mm(yi, pi), y, p)
    return _each(mm, y, x)


def _chunk_factors(r, alpha, lw, beta, kd, incl01):
    c_incl = _split_dot_left(incl01, lw)
    c_tot = jnp.sum(lw, axis=0, keepdims=True)
    inv_p = jnp.exp(-c_incl)
    to_end = jnp.exp(c_tot - c_incl)
    r_bar = r * jnp.exp(c_incl)
    b16 = lambda m: m.astype(BF16)
    return dict(a_bar=b16(alpha * jnp.exp(c_incl - lw)), r_bar=r_bar, r_bar16=b16(r_bar),
                b_bar=b16(beta * inv_p), k_bar=b16(kd * inv_p), b_til=b16(beta * to_end),
                k_til=b16(kd * to_end), p_tot=jnp.exp(c_tot))


def _rwkv_chunk_kernel(rf_ref, vf_ref, af_ref, lwf_ref, bef_ref, kdf_ref,
                       rb_ref, vb_ref, ab_ref, lwb_ref, beb_ref, kdb_ref, s0_ref,
                       of_ref, ob_ref, sfin_ref, h_scr, *, lay):
    s = pl.program_id(0)
    _, _, c, per = lay.seq_of_tile(s, RWKV_STEP_CHUNKS * RWKV_CHUNK)

    @pl.when(c == 0)
    def _load_state():
        h_scr[...] = s0_ref[0]

    n = RWKV_CHUNK
    hd = RWKV_HD
    row = lax.broadcasted_iota(I32, (n, n), 0)
    col = lax.broadcasted_iota(I32, (n, n), 1)
    eye = (row == col).astype(F32)
    diag_blocks = (row // INV_BLOCK) == (col // INV_BLOCK)
    b16 = lambda m: m.astype(BF16)
    mm = lambda a, b: _dot(b16(a), b16(b))

    fac, strict_m, incl_m, v16, unit_key = [], [], [], [], []
    for d, refs in enumerate(((rf_ref, vf_ref, af_ref, lwf_ref, bef_ref, kdf_ref),
                              (rb_ref, vb_ref, ab_ref, lwb_ref, beb_ref, kdb_ref))):
        r_ref, v_ref, a_ref, lw_ref, be_ref, kd_ref = refs
        strict = (col > row) if d else (col < row)
        incl = (col >= row) if d else (col <= row)
        incl01 = jnp.where(incl, 1.0, 0.0).astype(BF16)
        for j in range(RWKV_STEP_CHUNKS):
            rows = pl.ds(j * n, n)
            f = _chunk_factors(r_ref[rows, :], a_ref[rows, :], lw_ref[rows, :], be_ref[rows, :], kd_ref[rows, :],
                               incl01)
            v = b16(v_ref[rows, :])
            for h in range(RWKV_HEADS):
                sl = slice(h * hd, (h + 1) * hd)
                fac.append({k: a[:, sl] for k, a in f.items()})
                strict_m.append(strict)
                incl_m.append(incl)
                v16.append(v[:, sl])
                unit_key.append((d, j, h))

    gram = [_dot_nt(jnp.concatenate([f["a_bar"], f["r_bar16"]], axis=0),
                    jnp.concatenate([f["b_bar"], f["k_bar"]], axis=0)) for f in fac]
    l_mat = _each(lambda g, m: jnp.where(m, g[:n, :n], 0.0), gram, strict_m)
    a_k = _each(lambda g, m: jnp.where(m, g[:n, n:], 0.0), gram, strict_m)
    r_b = _each(lambda g, m: jnp.where(m, g[n:, :n], 0.0), gram, incl_m)
    r_k = _each(lambda g, m: jnp.where(m, g[n:, n:], 0.0), gram, incl_m)
    akv = _each(mm, a_k, v16)
    kt_v = _each(lambda f, v: _dot_tn(f["k_til"], v), fac, v16)
    rk_v = _each(mm, r_k, v16)
    t_inv = _unit_triangular_inverses(l_mat, eye, diag_blocks)
    mw = _each(lambda t, f, w: b16(mm(t, jnp.concatenate([f["a_bar"], b16(w)], axis=1))), t_inv, fac, akv)
    bt_mw = _each(lambda f, m: _dot_tn(f["b_til"], m), fac, mw)
    rb_mw = _each(mm, r_b, mw)
    q_eff = _each(lambda f, rb: b16(f["r_bar"] - rb[:, :hd]), fac, rb_mw)
    o_intra = _each(lambda rk, rb: rk - rb[:, hd:], rk_v, rb_mw)
    d_mat = _each(lambda kv, bt: kv - bt[:, hd:], kt_v, bt_mw)
    decay_col = [jnp.sum(eye * f["p_tot"], axis=1, keepdims=True) for f in fac]
    unit = {key: u for u, key in enumerate(unit_key)}

    for d, o_ref in enumerate((of_ref, ob_ref)):
        h_cur = [h_scr[d * RWKV_HEADS + h] for h in range(RWKV_HEADS)]
        order = range(RWKV_STEP_CHUNKS - 1, -1, -1) if d else range(RWKV_STEP_CHUNKS)
        for j in order:
            outs = []
            for h in range(RWKV_HEADS):
                u = unit[(d, j, h)]
                h16 = b16(h_cur[h])
                outs.append(_dot(q_eff[u], h16) + o_intra[u])
                h_cur[h] = decay_col[u] * h_cur[h] - mm(bt_mw[u][:, :hd], h16) + d_mat[u]
            o_ref[pl.ds(j * n, n), :] = jnp.concatenate(outs, axis=1)
        for h in range(RWKV_HEADS):
            h_scr[d * RWKV_HEADS + h] = h_cur[h]

    @pl.when(c == per - 1)
    def _store_state():
        sfin_ref[0] = h_scr[...]


def rwkv_chunks(lay, r, v, alpha, lw0, be0, kd0, lw1, be1, kd1, s0):
    n, dim = r.shape
    rows = RWKV_STEP_CHUNKS * RWKV_CHUNK
    n_seq = lay.bp + lay.bs
    assert lay.tp % rows == 0 and lay.ts % rows == 0

    def fwd(s):
        return (s, 0)

    def bwd(s):
        _, _, c, per = lay.seq_of_tile(s, rows)
        return (s - c + (per - 1 - c), 0)

    def seq(s):
        is_p, q, _, _ = lay.seq_of_tile(s, rows)
        return (jnp.where(is_p, q, lay.bp + q), 0, 0, 0)

    state_block = (1, 2 * RWKV_HEADS, RWKV_HD, RWKV_HD)
    return pl.pallas_call(
        functools.partial(_rwkv_chunk_kernel, lay=lay),
        grid=(n // rows,),
        in_specs=[pl.BlockSpec((rows, dim), fwd)] * 6 + [pl.BlockSpec((rows, dim), bwd)] * 6
        + [pl.BlockSpec(state_block, seq)],
        out_specs=[pl.BlockSpec((rows, dim), fwd), pl.BlockSpec((rows, dim), bwd), pl.BlockSpec(state_block, seq)],
        out_shape=[jax.ShapeDtypeStruct((n, dim), F32), jax.ShapeDtypeStruct((n, dim), F32),
                   jax.ShapeDtypeStruct((n_seq,) + state_block[1:], F32)],
        scratch_shapes=[pltpu.VMEM(state_block[1:], F32)],
        compiler_params=_cparams(("arbitrary",)),
        name="rwkv_chunks",
    )(r, v, alpha, lw0, be0, kd0, r, v, alpha, lw1, be1, kd1, s0)


def _even_out_kernel(x_ref, att_ref, of_ref, ob_ref, bonus_ref, gate_ref, lng_ref, lnb_ref, ones_ref,
                     wa_ref, wr_ref, g1_ref, o_ref):
    ones = ones_ref[...]
    osum = of_ref[...] + ob_ref[...]
    mean = _split_dot(osum, ones) * (1.0 / RWKV_HD)
    cen = osum - mean
    var = _split_dot(cen * cen, ones) * (1.0 / RWKV_HD)
    on = cen * lax.rsqrt(var + GN_EPS) * lng_ref[...] + lnb_ref[...]
    rw = (on + bonus_ref[...]) * gate_ref[...]
    y = _dot(att_ref[...].astype(BF16), wa_ref[...]) + _dot(rw.astype(BF16), wr_ref[...])
    o_ref[...] = x_ref[...] + g1_ref[0] * y


def even_out(lay, x, att, o_f, o_b, bonus, gate, ln_g, ln_b, w_out, mods):
    n, d = x.shape
    tile = TOKEN_TILE
    dim = RWKV_DIM
    const = lambda shape: pl.BlockSpec(shape, lambda i: (0,) * len(shape))
    tok = lambda w: pl.BlockSpec((tile, w), lambda i: (i, 0))
    w16 = w_out.astype(BF16)
    return pl.pallas_call(
        _even_out_kernel,
        grid=(n // tile,),
        in_specs=[tok(d), tok(ATT_Q), tok(dim), tok(dim), tok(dim), tok(dim),
                  const((1, dim)), const((1, dim)), const((dim, dim)),
                  const((ATT_Q, d)), const((dim, d)), _mod_spec(lay, 2, tile)],
        out_specs=tok(d),
        out_shape=jax.ShapeDtypeStruct((n, d), F32),
        compiler_params=_cparams(("arbitrary",)),
        name="even_out",
    )(x, att, o_f, o_b, bonus, gate, ln_g.reshape(1, dim), ln_b.reshape(1, dim), _block_ones(dim, RWKV_HD),
      w16[:ATT_Q], w16[ATT_Q:], mods)


def _gelu_tanh(x):
    return 0.5 * x * (1.0 + jnp.tanh(float(np.sqrt(2.0 / np.pi)) * (x + 0.044715 * (x * x * x))))


def _softplus(x):
    return jnp.maximum(x, 0.0) + jnp.log(1.0 + jnp.exp(-jnp.abs(x)))


def _odd_in_kernel(xm_ref, xprev_ref, xnext_ref, g_ref, sh_ref, sc_ref, w_ref, cw_ref, cb_ref,
                   gate_ref, u_ref, *, lay):
    i = pl.program_id(0)
    _, _, j, per = lay.seq_of_tile(i, TOKEN_TILE)
    tile = TOKEN_TILE
    y = _dot(_norm_mod(xm_ref[...], g_ref[...], sh_ref[0], sc_ref[0]).astype(BF16), w_ref[...])
    gate_ref[...] = _gelu_tanh(y[:, :D_RNN])
    x = y[:, D_RNN:]
    yh = _halo_proj(xprev_ref, xnext_ref, g_ref, sh_ref, sc_ref, w_ref[:, D_RNN:])
    first = j == 0
    last = j == per - 1
    row = lax.broadcasted_iota(I32, x.shape, 0)
    p1 = jnp.where(first, 0.0, yh[HALO - 1:HALO, :])
    p2 = jnp.where(first, 0.0, yh[HALO - 2:HALO - 1, :])
    n1 = jnp.where(last, 0.0, yh[HALO:HALO + 1, :])
    xm1 = jnp.where(row == 0, p1, pltpu.roll(x, 1, 0))
    xm2 = jnp.where(row == 0, p2, jnp.where(row == 1, p1, pltpu.roll(x, 2, 0)))
    xp1 = jnp.where(row == tile - 1, n1, pltpu.roll(x, tile - 1, 0))
    u_ref[...] = (cb_ref[...] + xm2 * cw_ref[0:1, :] + xm1 * cw_ref[1:2, :] + x * cw_ref[2:3, :]
                  + xp1 * cw_ref[3:4, :])


def odd_in(lay, x, g, mods, w_in, conv_w, conv_b):
    n, d = x.shape
    tile = TOKEN_TILE
    dr = D_RNN
    const = lambda shape: pl.BlockSpec(shape, lambda i: (0,) * len(shape))
    tok = lambda w: pl.BlockSpec((tile, w), lambda i: (i, 0))
    return pl.pallas_call(
        functools.partial(_odd_in_kernel, lay=lay),
        grid=(n // tile,),
        in_specs=[tok(d)] + _halo_specs(n, d, tile) + [
            const((1, d)), _mod_spec(lay, 0, tile), _mod_spec(lay, 1, tile), const((d, 2 * dr)),
            const((CONV_W, dr)), const((1, dr))],
        out_specs=[tok(dr)] * 2,
        out_shape=[jax.ShapeDtypeStruct((n, dr), F32)] * 2,
        compiler_params=_cparams(("arbitrary",)),
        name="odd_in",
    )(x, x, x, g.reshape(1, d), mods, mods, w_in.astype(BF16), conv_w, conv_b.reshape(1, dr))


def _lru_coefficients(u, d, wbd_ref, ba_ref, bx_ref, lam_ref):
    u16 = u.astype(BF16)
    n_grp = D_RNN // LRU_GROUP
    z = [_dot(u16[:, g * LRU_GROUP:(g + 1) * LRU_GROUP], wbd_ref[d, g]) for g in range(n_grp)]
    pick = lambda m: jnp.concatenate([zg[:, m * LRU_GROUP:(m + 1) * LRU_GROUP] for zg in z], axis=1)
    ga = _sigmoid(pick(0) + ba_ref[d:d + 1, :])
    gx = _sigmoid(pick(1) + bx_ref[d:d + 1, :])
    log_a = -LRU_C * ga * _softplus(-lam_ref[d:d + 1, :])
    return jnp.exp(log_a), jnp.sqrt(1.0 - jnp.exp(2.0 * log_a)) * gx * u


def _lru_scan_kernel(uf_ref, ub_ref, wbd_ref, ba_ref, bx_ref, lam_ref, h0_ref, hf_ref, hb_ref,
                     af_ref, bf_ref, ab_ref, bb_ref, carry, *, lay):
    i = pl.program_id(0)
    _, _, j, _ = lay.seq_of_tile(i, TOKEN_TILE)
    tile = TOKEN_TILE

    @pl.when(j == 0)
    def _load_state():
        carry[...] = h0_ref[0]

    af_ref[...], bf_ref[...] = _lru_coefficients(uf_ref[...], 0, wbd_ref, ba_ref, bx_ref, lam_ref)
    ab_ref[...], bb_ref[...] = _lru_coefficients(ub_ref[...], 1, wbd_ref, ba_ref, bx_ref, lam_ref)

    def step(t, hs):
        hf, hb = hs
        tb = tile - 1 - t
        hf = af_ref[pl.ds(t, 1), :] * hf + bf_ref[pl.ds(t, 1), :]
        hb = ab_ref[pl.ds(tb, 1), :] * hb + bb_ref[pl.ds(tb, 1), :]
        hf_ref[pl.ds(t, 1), :] = hf
        hb_ref[pl.ds(tb, 1), :] = hb
        return hf, hb

    hf, hb = lax.fori_loop(0, tile, step, (carry[0:1, :], carry[1:2, :]), unroll=8)
    carry[0:1, :] = hf
    carry[1:2, :] = hb


def lru_scan(lay, u, wa, ba, wx, bx, lam, h0):
    n, dr = u.shape
    tile = TOKEN_TILE
    per = LRU_GROUP // LRU_BS
    n_grp = dr // LRU_GROUP

    def block_diag(w):
        w = w.reshape(n_grp, per, LRU_BS, LRU_BS)
        eye = jnp.eye(per, dtype=w.dtype)
        return jnp.einsum("gpcd,pq->gpcqd", w, eye).reshape(n_grp, LRU_GROUP, LRU_GROUP)

    wbd = jnp.stack([jnp.concatenate([block_diag(wa[d]), block_diag(wx[d])], axis=2) for d in range(2)]).astype(BF16)
    const = lambda shape: pl.BlockSpec(shape, lambda i: (0,) * len(shape))

    def fwd(i):
        return (i, 0)

    def bwd(i):
        _, _, j, per = lay.seq_of_tile(i, tile)
        return (i - j + (per - 1 - j), 0)

    def seq(i):
        is_p, q, _, _ = lay.seq_of_tile(i, tile)
        return (jnp.where(is_p, q, lay.bp + q), 0, 0)

    return pl.pallas_call(
        functools.partial(_lru_scan_kernel, lay=lay),
        grid=(n // tile,),
        in_specs=[pl.BlockSpec((tile, dr), fwd), pl.BlockSpec((tile, dr), bwd),
                  const((2, n_grp, LRU_GROUP, 2 * LRU_GROUP)), const((2, dr)), const((2, dr)), const((2, dr)),
                  pl.BlockSpec((1, 2, dr), seq)],
        out_specs=[pl.BlockSpec((tile, dr), fwd), pl.BlockSpec((tile, dr), bwd)],
        out_shape=[jax.ShapeDtypeStruct((n, dr), F32)] * 2,
        scratch_shapes=[pltpu.VMEM((tile, dr), F32)] * 4 + [pltpu.VMEM((2, dr), F32)],
        compiler_params=_cparams(("arbitrary",)),
        name="lru_scan",
    )(u, u, wbd, ba, bx, lam, h0)


def _odd_out_kernel(x_ref, gate_ref, hf_ref, hb_ref, w_ref, g1_ref, o_ref):
    y = _dot((gate_ref[...] * (hf_ref[...] + hb_ref[...])).astype(BF16), w_ref[...])
    o_ref[...] = x_ref[...] + g1_ref[0] * y


def odd_out(lay, x, gate, hf, hb, w_out, mods):
    n, d = x.shape
    tile = TOKEN_TILE
    tok = lambda w: pl.BlockSpec((tile, w), lambda i: (i, 0))
    return pl.pallas_call(
        _odd_out_kernel,
        grid=(n // tile,),
        in_specs=[tok(d), tok(D_RNN), tok(D_RNN), tok(D_RNN), pl.BlockSpec((D_RNN, d), lambda i: (0, 0)),
                  _mod_spec(lay, 2, tile)],
        out_specs=tok(d),
        out_shape=jax.ShapeDtypeStruct((n, d), F32),
        compiler_params=_cparams(("arbitrary",)),
        name="odd_out",
    )(x, gate, hf, hb, w_out.astype(BF16), mods)


def _pack_bf16_pairs(x):
    n = x.shape[1] // 2
    hi = pltpu.bitcast(x[:, :n].astype(BF16).astype(F32), jnp.uint32)
    lo = pltpu.bitcast(x[:, n:].astype(BF16).astype(F32), jnp.uint32)
    return hi | (lo >> 16)


def _unpack_bf16_pairs(u):
    a = pltpu.bitcast(u & jnp.uint32(0xFFFF0000), F32)
    b = pltpu.bitcast(u << 16, F32)
    return jnp.concatenate([a, b], axis=1).astype(BF16)


def _router_kernel(x_ref, g_ref, sh_ref, sc_ref, rw_ref, rb_ref, before_ref,
                   h_ref, idx_ref, gate_ref, rank_ref, count_ref, cnt_scr):
    @pl.when(pl.program_id(0) == 0)
    def _reset():
        cnt_scr[...] = jnp.zeros_like(cnt_scr)

    h = _norm_mod(x_ref[...], g_ref[...], sh_ref[0], sc_ref[0])
    h_ref[...] = _pack_bf16_pairs(h)
    logits = lax.dot_general(rw_ref[...], h, (((1,), (1,)), ((), ())), preferred_element_type=F32,
                             precision=HIGHEST) + rb_ref[...]
    e_id = lax.broadcasted_iota(I32, logits.shape, 0)
    vals, ids, hots = [], [], []
    for _ in range(TOP_K):
        m = jnp.max(logits, axis=0, keepdims=True)
        pick = jnp.min(jnp.where(logits == m, e_id, N_EXPERTS), axis=0, keepdims=True)
        hot = e_id == pick
        vals.append(m)
        ids.append(pick)
        hots.append(jnp.where(hot, 1.0, 0.0))
        logits = jnp.where(hot, -jnp.inf, logits)
    top = jnp.concatenate(vals, axis=0)
    p = jnp.exp(top - top[0:1, :])
    gate_ref[...] = p / jnp.sum(p, axis=0, keepdims=True)
    idx_ref[...] = jnp.concatenate(ids, axis=0)
    hot_all = hots[0] + hots[1] + hots[2] + hots[3]
    seen = cnt_scr[:, 0:1] + _dot(hot_all.astype(BF16), before_ref[...])
    rank_ref[...] = jnp.concatenate([jnp.sum(hk * seen, axis=0, keepdims=True) for hk in hots],
                                    axis=0).astype(I32)
    cnt_scr[...] = cnt_scr[...] + jnp.sum(hot_all, axis=1, keepdims=True)
    count_ref[...] = cnt_scr[...]


def router(lay, x, g, mods, rw, rb):
    n, d = x.shape
    tile = TOKEN_TILE
    before = jnp.asarray(np.triu(np.ones((tile, tile), np.float32), 1), dtype=BF16)
    per_tok = pl.BlockSpec((TOP_K, tile), lambda i: (0, i))
    return pl.pallas_call(
        _router_kernel,
        grid=(n // tile,),
        in_specs=[pl.BlockSpec((tile, d), lambda i: (i, 0)), pl.BlockSpec((1, d), lambda i: (0, 0)),
                  _mod_spec(lay, 3, tile), _mod_spec(lay, 4, tile),
                  pl.BlockSpec((N_EXPERTS, d), lambda i: (0, 0)), pl.BlockSpec((N_EXPERTS, 1), lambda i: (0, 0)),
                  pl.BlockSpec((tile, tile), lambda i: (0, 0))],
        out_specs=[pl.BlockSpec((tile, d // 2), lambda i: (i, 0)), per_tok, per_tok, per_tok,
                   pl.BlockSpec((N_EXPERTS, 128), lambda i: (0, 0))],
        out_shape=[jax.ShapeDtypeStruct((n, d // 2), jnp.uint32), jax.ShapeDtypeStruct((TOP_K, n), I32),
                   jax.ShapeDtypeStruct((TOP_K, n), F32), jax.ShapeDtypeStruct((TOP_K, n), I32),
                   jax.ShapeDtypeStruct((N_EXPERTS, 128), F32)],
        scratch_shapes=[pltpu.VMEM((N_EXPERTS, 128), F32)],
        compiler_params=_cparams(("arbitrary",)),
        name="router",
    )(x, g.reshape(1, d), mods, mods, rw.T, rb.reshape(N_EXPERTS, 1), before)


def _row_copy(src_hbm, dst_vmem, sem, src_row, dst_row):
    return pltpu.make_async_copy(src_hbm.at[pl.ds(src_row, 1)], dst_vmem.at[pl.ds(dst_row, 1)], sem)


def _dispatch_kernel(dest_ref, fill_lo_ref, fill_hi_ref, src_ref, o_ref, row_tok, *, n_tok):
    @pl.when(pl.program_id(0) == 0)
    def _invert():
        def clear(i, _):
            row_tok[i] = 0
            return 0

        def clear_range(e, _):
            lax.fori_loop(fill_lo_ref[e], fill_hi_ref[e], clear, 0)
            return 0

        lax.fori_loop(0, N_EXPERTS, clear_range, 0)
        for k in range(TOP_K):
            def put(t, _, k=k):
                row_tok[dest_ref[k * n_tok + t]] = t
                return 0

            lax.fori_loop(0, n_tok, put, 0, unroll=8)

    base = pl.program_id(0) * MOE_TILE

    def copy_row(r, _):
        o_ref[pl.ds(r, 1), :] = src_ref[pl.ds(row_tok[base + r], 1), :]
        return 0

    lax.fori_loop(0, MOE_TILE, copy_row, 0, unroll=8)


def dispatch_rows(src, dest, fill_lo, fill_hi, n_out):
    n, d = src.shape
    return pl.pallas_call(
        functools.partial(_dispatch_kernel, n_tok=n),
        grid_spec=pltpu.PrefetchScalarGridSpec(
            num_scalar_prefetch=3,
            grid=(n_out // MOE_TILE,),
            in_specs=[pl.BlockSpec((n, d), lambda i, *_: (0, 0), pipeline_mode=pl.Buffered(1))],
            out_specs=pl.BlockSpec((MOE_TILE, d), lambda i, *_: (i, 0)),
            scratch_shapes=[pltpu.SMEM((n_out,), I32)],
        ),
        out_shape=jax.ShapeDtypeStruct((n_out, d), src.dtype),
        compiler_params=_cparams(("arbitrary",)),
        name="moe_dispatch",
    )(dest, fill_lo, fill_hi, src)


def _expert_weight_copies(w1_hbm, w2_hbm, w1_buf, w2_buf, sems, row, slot):
    return (pltpu.make_async_copy(w1_hbm.at[row], w1_buf.at[slot], sems.at[slot, 0]),
            pltpu.make_async_copy(w2_hbm.at[row], w2_buf.at[slot], sems.at[slot, 1]))


def _expert_kernel(blk_e_ref, first_ref, slot_ref, next_e_ref, n_used_ref, x_ref, w1_hbm, b1_ref, w2_hbm, b2_ref,
                   o_ref, w1_buf, w2_buf, w1_scr, w2_scr, sems, *, layer):
    i = pl.program_id(0)
    used = i < n_used_ref[0]
    copies = functools.partial(_expert_weight_copies, w1_hbm, w2_hbm, w1_buf, w2_buf, sems)

    @pl.when(jnp.logical_and(used, first_ref[i] == 1))
    def _switch_expert():
        slot = slot_ref[i]

        @pl.when(i == 0)
        def _fetch_first():
            for cp in copies(layer * N_EXPERTS + blk_e_ref[0], 0):
                cp.start()

        nxt = next_e_ref[i]

        @pl.when(nxt >= 0)
        def _prefetch_next():
            for cp in copies(layer * N_EXPERTS + nxt, 1 - slot):
                cp.start(priority=1)

        for cp in copies(0, slot):
            cp.wait()
        w1_scr[...] = w1_buf[slot].astype(BF16)
        w2_scr[...] = w2_buf[slot].astype(BF16)

    @pl.when(used)
    def _compute():
        hb = _dot(_unpack_bf16_pairs(x_ref[...]), w1_scr[...]) + b1_ref[0]
        glu = jnp.minimum(hb[:, :D_EXPERT], SWIGLU_LIMIT)
        lin = jnp.clip(hb[:, D_EXPERT:], -SWIGLU_LIMIT, SWIGLU_LIMIT)
        act = glu * _sigmoid(SWIGLU_ALPHA * glu) * (lin + 1.0)
        o_ref[...] = _dot(act.astype(BF16), w2_scr[...]) + b2_ref[0]

    @pl.when(jnp.logical_not(used))
    def _clear():
        o_ref[...] = jnp.zeros_like(o_ref)


def experts(x_sorted, blk_expert, count, n_used, layer, w1, b1, w2, b2):
    n_rows = x_sorted.shape[0]
    n_blk = n_rows // MOE_TILE
    d, de2 = w1.shape[1:]
    de = w2.shape[1]
    first = jnp.concatenate([jnp.ones((1,), I32), (blk_expert[1:] != blk_expert[:-1]).astype(I32)])
    slot = (jnp.cumsum(first) - 1) % 2
    e_id = jnp.arange(N_EXPERTS, dtype=I32)
    later_present = jnp.logical_and(e_id[None, :] > e_id[:, None], count[None, :] > 0)
    next_present = jnp.min(jnp.where(later_present, e_id[None, :], N_EXPERTS), axis=1)
    next_e = jnp.where(next_present < N_EXPERTS, next_present, -1)[blk_expert]
    bmap = lambda i, be, *_: (layer * N_EXPERTS + be[i], 0, 0)
    return pl.pallas_call(
        functools.partial(_expert_kernel, layer=layer),
        grid_spec=pltpu.PrefetchScalarGridSpec(
            num_scalar_prefetch=5,
            grid=(n_blk,),
            in_specs=[pl.BlockSpec((MOE_TILE, d // 2), lambda i, *_: (i, 0)),
                      pl.BlockSpec(memory_space=pl.ANY), pl.BlockSpec((1, 1, de2), bmap),
                      pl.BlockSpec(memory_space=pl.ANY), pl.BlockSpec((1, 1, d), bmap)],
            out_specs=pl.BlockSpec((MOE_TILE, d), lambda i, *_: (i, 0)),
            scratch_shapes=[pltpu.VMEM((2, d, de2), F32), pltpu.VMEM((2, de, d), F32),
                            pltpu.VMEM((d, de2), BF16), pltpu.VMEM((de, d), BF16),
                            pltpu.SemaphoreType.DMA((2, 2))],
        ),
        out_shape=jax.ShapeDtypeStruct((n_rows, d), F32),
        compiler_params=_cparams(("arbitrary",)),
        name="moe_experts",
    )(blk_expert, first, slot.astype(I32), next_e.astype(I32), n_used, x_sorted, w1, b1, w2, b2)


def _combine_kernel(dest_ref, y_hbm, x_ref, gate_ref, g2_ref, o_ref, buf, sems, *, n_tok):
    tile = TOKEN_TILE
    i = pl.program_id(0)
    slot = i % 2

    def issue(step, dst_slot):
        def body(r, _):
            for k in range(TOP_K):
                _row_copy(y_hbm, buf.at[dst_slot, k], sems.at[dst_slot],
                          dest_ref[k * n_tok + step * tile + r], r).start(priority=k % 2)
            return 0

        lax.fori_loop(0, tile, body, 0, unroll=2)

    @pl.when(i == 0)
    def _first_tile():
        issue(0, 0)

    @pl.when(i + 1 < pl.num_programs(0))
    def _next_tile():
        issue(i + 1, 1 - slot)

    def wait(r, _):
        for k in range(TOP_K):
            _row_copy(y_hbm, buf.at[slot, k], sems.at[slot], 0, r).wait()
        return 0

    lax.fori_loop(0, tile, wait, 0, unroll=8)
    gate = gate_ref[...]
    acc = buf[slot, 0] * gate[:, 0:1]
    for k in range(1, TOP_K):
        acc = acc + buf[slot, k] * gate[:, k:k + 1]
    o_ref[...] = x_ref[...] + g2_ref[0] * acc


def combine(lay, x, y_sorted, dest, gate, mods):
    n, d = x.shape
    tile = TOKEN_TILE
    return pl.pallas_call(
        functools.partial(_combine_kernel, n_tok=n),
        grid_spec=pltpu.PrefetchScalarGridSpec(
            num_scalar_prefetch=1,
            grid=(n // tile,),
            in_specs=[pl.BlockSpec(memory_space=pl.ANY),
                      pl.BlockSpec((tile, d), lambda i, dest: (i, 0)),
                      pl.BlockSpec((tile, TOP_K), lambda i, dest: (i, 0)),
                      _mod_spec(lay, 5, tile)],
            out_specs=pl.BlockSpec((tile, d), lambda i, dest: (i, 0)),
            scratch_shapes=[pltpu.VMEM((2, TOP_K, tile, d), F32), pltpu.SemaphoreType.DMA((2,))],
        ),
        out_shape=jax.ShapeDtypeStruct((n, d), F32),
        compiler_params=_cparams(("arbitrary",)),
        name="moe_combine",
    )(dest, y_sorted, x, gate, mods)


def moe_layer(lay, x, g, mods, rw, rb, layer, w1, b1, w2, b2):
    n, d = x.shape
    h, idx_t, gate_t, rank_t, count = router(lay, x, g, mods, rw, rb)
    expert = idx_t.reshape(-1)
    rank = rank_t.reshape(-1)
    n_asg = expert.shape[0]
    count = count[:, 0].astype(I32)
    padded = (count + MOE_TILE - 1) // MOE_TILE * MOE_TILE
    pend = jnp.cumsum(padded)
    dest = ((pend - padded)[expert] + rank).astype(I32)
    n_blk = n_asg // MOE_TILE + N_EXPERTS
    blk_start = jnp.arange(n_blk, dtype=I32) * MOE_TILE
    blk_expert = jnp.minimum(jnp.sum((pend[None, :] <= blk_start[:, None]).astype(I32), axis=1), N_EXPERTS - 1)
    n_used = (pend[-1] // MOE_TILE).astype(I32).reshape(1)
    fill_lo = (pend - padded + count).astype(I32)
    fill_hi = pend.at[-1].set(n_blk * MOE_TILE).astype(I32)
    x_sorted = dispatch_rows(h, dest, fill_lo, fill_hi, n_blk * MOE_TILE)
    y_sorted = experts(x_sorted, blk_expert, count, n_used, layer, w1, b1, w2, b2)
    return combine(lay, x, y_sorted, dest, gate_t.T, mods)


def kernel(x_prompt, x_sample, cache_attn_k, cache_attn_v, state_rwkv, state_lru, c, c_ctx,
           mod_w, mod_b, norm1_g, norm2_g, ev_w_in, ev_w_out, q_norm_g, k_norm_g,
           rwkv_mu, rwkv_w0, rwkv_w_up, rwkv_a0, rwkv_a_up, rwkv_g_up, rwkv_k_k, rwkv_k_a,
           rwkv_r_k, rwkv_ln_g, rwkv_ln_b, od_w_in, od_w_out, conv_w, conv_b,
           lru_wa, lru_ba, lru_wx, lru_bx, lru_lambda,
           router_w, router_b, exp_w1, exp_b1, exp_w2, exp_b2):
    bp, tp, d = x_prompt.shape
    bs, ts, _ = x_sample.shape
    depth = mod_w.shape[0]
    lay = Layout(bp, tp, bs, ts)
    assert bs < MOD_ROWS and tp % TOKEN_TILE == 0 and ts % TOKEN_TILE == 0 and d == D_MODEL
    n_p = lay.n_p

    x = jnp.concatenate([x_prompt.reshape(n_p, d), x_sample.reshape(bs * ts, d)], axis=0)
    cvec = jnp.zeros((MOD_ROWS, d), F32).at[:bs].set(c).at[bs].set(c_ctx)
    mods_all = modulation(cvec, mod_w, mod_b)
    rope = rope_tables(ts)
    n_le = depth * N_EXPERTS
    expert_prm = (exp_w1.reshape(n_le, d, -1), exp_b1.reshape(n_le, 1, -1),
                  exp_w2.reshape(n_le, -1, d), exp_b2.reshape(n_le, 1, d))

    new_k, new_v, new_rw, new_lru = [], [], [], []
    for l in range(depth):
        j = l // 2
        mods = mods_all[l]
        if l % 2 == 0:
            prm = (rwkv_mu[j], rwkv_w0[j], rwkv_w_up[j], rwkv_a0[j], rwkv_a_up[j], rwkv_g_up[j],
                   rwkv_k_k[j], rwkv_k_a[j], rwkv_r_k[j].reshape(-1))
            qkv, r, v, alpha, lw0, be0, kd0, lw1, be1, kd1, bonus, gate = even_in(
                lay, x, norm1_g[l], mods, ev_w_in[j], prm)
            att_p, k_norm = attention(qkv, 0, bp, tp, q_norm_g[j], k_norm_g[j])
            cache = (cache_attn_k[:, j].reshape(bs, -1, ATT_KV), cache_attn_v[:, j].reshape(bs, -1, ATT_KV))
            (att_s,) = attention(qkv, n_p, bs, ts, q_norm_g[j], k_norm_g[j], cache=cache, rope=rope)
            att = jnp.concatenate([att_p, att_s], axis=0)
            s_lat = jnp.swapaxes(state_rwkv[:, j], -1, -2).reshape(bs, 2 * RWKV_HEADS, RWKV_HD, RWKV_HD)
            s0 = jnp.concatenate([jnp.zeros((bp,) + s_lat.shape[1:], F32), s_lat], axis=0)
            o_f, o_b, s_fin = rwkv_chunks(lay, r, v, alpha, lw0, be0, kd0, lw1, be1, kd1, s0)
            x = even_out(lay, x, att, o_f, o_b, bonus, gate, rwkv_ln_g[j], rwkv_ln_b[j], ev_w_out[j], mods)
            new_k.append(k_norm.reshape(bp, tp, ATT_KV_HEADS, HEAD_DIM))
            new_v.append(qkv[:n_p, ATT_Q + ATT_KV:].reshape(bp, tp, ATT_KV_HEADS, HEAD_DIM))
            new_rw.append(jnp.swapaxes(s_fin[:bp].reshape(bp, 2, RWKV_HEADS, RWKV_HD, RWKV_HD), -1, -2))
        else:
            gate, u = odd_in(lay, x, norm1_g[l], mods, od_w_in[j], conv_w[j], conv_b[j])
            h0 = jnp.concatenate([jnp.zeros((bp, 2, D_RNN), F32), state_lru[:, j]], axis=0)
            hf, hb = lru_scan(lay, u, lru_wa[j], lru_ba[j], lru_wx[j], lru_bx[j], lru_lambda[j], h0)
            x = odd_out(lay, x, gate, hf, hb, od_w_out[j], mods)
            hf_p = hf[:n_p].reshape(bp, tp, D_RNN)
            hb_p = hb[:n_p].reshape(bp, tp, D_RNN)
            new_lru.append(jnp.stack([hf_p[:, -1], hb_p[:, 0]], axis=1))
        x = moe_layer(lay, x, norm2_g[l], mods, router_w[l], router_b[l], l, *expert_prm)

    y_prompt = x[:n_p].reshape(bp, tp, d)
    y_sample = x[n_p:].reshape(bs, ts, d)
    return (y_prompt, y_sample, jnp.stack(new_k, axis=1), jnp.stack(new_v, axis=1), jnp.stack(new_rw, axis=1),
            jnp.stack(new_lru, axis=1))
```

```python
import functools
from typing import NamedTuple

import numpy as np
import jax
import jax.numpy as jnp
from jax import lax
from jax.experimental import pallas as pl
from jax.experimental.pallas import tpu as pltpu

F32 = jnp.float32
BF16 = jnp.bfloat16
I32 = jnp.int32
HIGHEST = lax.Precision.HIGHEST

D_MODEL = 1024
EPS = 1e-6
GRID_W = 64
ATT_HEADS = 8
ATT_KV_HEADS = 2
HEAD_DIM = 64
GQA_GROUP = ATT_HEADS // ATT_KV_HEADS
ATT_Q = ATT_HEADS * HEAD_DIM
ATT_KV = ATT_KV_HEADS * HEAD_DIM
ROPE_THETA = 10000.0
LOG2_E = 1.4426950408889634
RWKV_HEADS = 8
RWKV_HD = 64
RWKV_DIM = RWKV_HEADS * RWKV_HD
DECAY_RANK = 64
ICLR_RANK = 64
GATE_RANK = 128
RWKV_COLS = 3 * RWKV_DIM + 2 * DECAY_RANK + 2 * ICLR_RANK + GATE_RANK
GN_EPS = 64e-5
D_RNN = D_MODEL
LRU_BS = 64
LRU_GROUP = 256
CONV_W = 4
LRU_C = 8.0
N_EXPERTS = 32
TOP_K = 4
D_EXPERT = D_MODEL
SWIGLU_LIMIT = 7.0
SWIGLU_ALPHA = 1.702

TOKEN_TILE = 256
RWKV_CHUNK = 64
RWKV_STEP_CHUNKS = 4
INV_BLOCK = 16
ATT_Q_TILE = 256
MOE_TILE = 256
HALO = 8
MOD_ROWS = 16
VMEM_LIMIT = 56 * 1024 * 1024


class Layout(NamedTuple):
    bp: int
    tp: int
    bs: int
    ts: int

    @property
    def n_p(self):
        return self.bp * self.tp

    @property
    def n(self):
        return self.bp * self.tp + self.bs * self.ts

    def tiles(self, tile):
        return self.n // tile

    def seq_of_tile(self, i, tile):
        npt = self.n_p // tile
        is_p = i < npt
        ii = jnp.where(is_p, i, i - npt)
        per = jnp.where(is_p, self.tp // tile, self.ts // tile)
        return is_p, ii // per, ii % per, per

    def mod_row(self, i, tile):
        is_p, seq, _, _ = self.seq_of_tile(i, tile)
        return jnp.where(is_p, self.bs, seq)


def _cparams(sem):
    return pltpu.CompilerParams(dimension_semantics=sem, vmem_limit_bytes=VMEM_LIMIT)


def _dot(a, b):
    return jnp.dot(a, b, preferred_element_type=F32)


def _dot_nt(a, b):
    return lax.dot_general(a, b, (((1,), (1,)), ((), ())), preferred_element_type=F32)


def _dot_tn(a, b):
    return lax.dot_general(a, b, (((0,), (0,)), ((), ())), preferred_element_type=F32)


def _split_dot(x, m01):
    hi = x.astype(BF16)
    lo = (x - hi.astype(F32)).astype(BF16)
    return _dot(hi, m01) + _dot(lo, m01)


def _split_dot_left(m01, x):
    hi = x.astype(BF16)
    lo = (x - hi.astype(F32)).astype(BF16)
    return _dot(m01, hi) + _dot(m01, lo)


def _sigmoid(x):
    return 1.0 / (1.0 + jnp.exp(-x))


def _block_ones(n, blk):
    idx = np.arange(n) // blk
    return jnp.asarray((idx[:, None] == idx[None, :]).astype(np.float32), dtype=BF16)


def _mod_kernel(c_ref, w_ref, b_ref, o_ref):
    c = c_ref[...]
    s = c * _sigmoid(c)
    o_ref[0] = jnp.dot(s, w_ref[0], preferred_element_type=F32, precision=HIGHEST) + b_ref[0]


def modulation(cvec, mod_w, mod_b):
    depth, d, six_d = mod_w.shape
    nchunk = six_d // d
    out = pl.pallas_call(
        _mod_kernel,
        grid=(depth, nchunk),
        in_specs=[
            pl.BlockSpec((MOD_ROWS, d), lambda l, k: (0, 0)),
            pl.BlockSpec((1, d, d), lambda l, k: (l, 0, k)),
            pl.BlockSpec((1, 1, d), lambda l, k: (l, 0, k)),
        ],
        out_specs=pl.BlockSpec((1, MOD_ROWS, d), lambda l, k: (l, 0, k)),
        out_shape=jax.ShapeDtypeStruct((depth, MOD_ROWS, six_d), F32),
        compiler_params=_cparams(("arbitrary", "arbitrary")),
        name="modulation",
    )(cvec, mod_w, mod_b.reshape(depth, 1, six_d))
    return out.reshape(depth, MOD_ROWS * nchunk, 1, d)


def _mod_spec(lay, k, tile):
    return pl.BlockSpec((1, 1, D_MODEL), lambda i, *_: (lay.mod_row(i, tile) * 6 + k, 0, 0))


def _norm_mod(x, g, shift, scale):
    ms = jnp.mean(x * x, axis=-1, keepdims=True)
    h = x * lax.rsqrt(ms + EPS) * g
    return h * (1.0 + scale) + shift


def _head_norm(x, g, ones):
    ms = _split_dot(x * x, ones) * (1.0 / HEAD_DIM)
    return x * lax.rsqrt(ms + EPS) * g


def _rope(x, cos, sin_signed):
    n = x.shape[1]
    nxt = pltpu.roll(x, n - 1, 1)
    prv = pltpu.roll(x, 1, 1)
    lane = lax.broadcasted_iota(I32, x.shape, 1)
    swapped = jnp.where(lane % 2 == 0, nxt, prv)
    return x * cos + swapped * sin_signed


def _attn_kernel(*refs, t_len, n_ctx, rotary):
    if rotary:
        (qkv_ref, ck_ref, cv_ref, cos_ref, sin_ref, qg_ref, kg_ref, oq_ref, ok_ref,
         att_ref, k_scr, v_scr) = refs
    else:
        qkv_ref, qg_ref, kg_ref, oq_ref, ok_ref, att_ref, kn_ref, k_scr, v_scr = refs
    qi = pl.program_id(1)
    tq = ATT_Q_TILE

    @pl.when(qi == 0)
    def _prepare_keys():
        k = _head_norm(qkv_ref[:, ATT_Q:ATT_Q + ATT_KV], kg_ref[...], ok_ref[...])
        v = qkv_ref[:, ATT_Q + ATT_KV:ATT_Q + 2 * ATT_KV]
        if rotary:
            k = _rope(k, cos_ref[:, :ATT_KV], sin_ref[:, :ATT_KV])
            k_scr[0:n_ctx, :] = ck_ref[0].astype(BF16)
            v_scr[0:n_ctx, :] = cv_ref[0].astype(BF16)
        else:
            kn_ref[...] = k
        k_scr[n_ctx:n_ctx + t_len, :] = k.astype(BF16)
        v_scr[n_ctx:n_ctx + t_len, :] = v.astype(BF16)

    row0 = pl.multiple_of(qi * tq, tq)
    q = _head_norm(qkv_ref[pl.ds(row0, tq), 0:ATT_Q], qg_ref[...], oq_ref[...])
    if rotary:
        q = _rope(q, cos_ref[pl.ds(row0, tq), :], sin_ref[pl.ds(row0, tq), :])
    q = (q * (HEAD_DIM ** -0.5 * LOG2_E)).astype(BF16)
    outs = []
    for j in range(ATT_KV_HEADS):
        kj = k_scr[:, j * HEAD_DIM:(j + 1) * HEAD_DIM]
        vj = v_scr[:, j * HEAD_DIM:(j + 1) * HEAD_DIM]
        qs = jnp.concatenate(
            [q[:, (j * GQA_GROUP + g) * HEAD_DIM:(j * GQA_GROUP + g + 1) * HEAD_DIM] for g in range(GQA_GROUP)],
            axis=0)
        s = _dot_nt(qs, kj)
        p = jnp.exp2(s - jnp.max(s, axis=-1, keepdims=True))
        o = _dot(p.astype(BF16), vj) / jnp.sum(p, axis=-1, keepdims=True)
        outs.extend(o[g * tq:(g + 1) * tq] for g in range(GQA_GROUP))
    att_ref[...] = jnp.concatenate(outs, axis=1).astype(BF16)


def attention(qkv, seq0, n_seq, t_len, q_g, k_g, cache=None, rope=None):
    rotary = cache is not None
    n_ctx = cache[0].shape[1] if rotary else 0
    blk0 = seq0 // t_len
    n_q = t_len // ATT_Q_TILE
    width = qkv.shape[1]
    qg = jnp.tile(q_g, ATT_HEADS).reshape(1, ATT_Q)
    kg = jnp.tile(k_g, ATT_KV_HEADS).reshape(1, ATT_KV)
    const = lambda shape: pl.BlockSpec(shape, lambda b, qi: (0,) * len(shape))
    in_specs = [pl.BlockSpec((t_len, width), lambda b, qi: (blk0 + b, 0))]
    args = [qkv]
    if rotary:
        in_specs += [pl.BlockSpec((1, n_ctx, ATT_KV), lambda b, qi: (b, 0, 0))] * 2
        in_specs += [const((t_len, ATT_Q))] * 2
        args += [cache[0], cache[1], rope[0], rope[1]]
    in_specs += [const((1, ATT_Q)), const((1, ATT_KV)), const((ATT_Q, ATT_Q)), const((ATT_KV, ATT_KV))]
    args += [qg, kg, _block_ones(ATT_Q, HEAD_DIM), _block_ones(ATT_KV, HEAD_DIM)]
    out_specs = [pl.BlockSpec((ATT_Q_TILE, ATT_Q), lambda b, qi: (b * n_q + qi, 0))]
    out_shape = [jax.ShapeDtypeStruct((n_seq * t_len, ATT_Q), BF16)]
    if not rotary:
        out_specs.append(pl.BlockSpec((t_len, ATT_KV), lambda b, qi: (b, 0)))
        out_shape.append(jax.ShapeDtypeStruct((n_seq * t_len, ATT_KV), F32))
    return pl.pallas_call(
        functools.partial(_attn_kernel, t_len=t_len, n_ctx=n_ctx, rotary=rotary),
        grid=(n_seq, n_q),
        in_specs=in_specs,
        out_specs=out_specs,
        out_shape=out_shape,
        scratch_shapes=[pltpu.VMEM((n_ctx + t_len, ATT_KV), BF16), pltpu.VMEM((n_ctx + t_len, ATT_KV), BF16)],
        compiler_params=_cparams(("arbitrary", "arbitrary")),
        name="attention_latent" if rotary else "attention_context",
    )(*args)


def rope_tables(t_len):
    t = jnp.arange(t_len)
    pos = jnp.stack([t // GRID_W, t % GRID_W], axis=-1).astype(F32)
    n_freq = HEAD_DIM // 4
    inv = ROPE_THETA ** (-jnp.arange(n_freq, dtype=F32) / n_freq)
    ang = (pos[:, :, None] * inv).reshape(t_len, 2 * n_freq)
    cos = jnp.repeat(jnp.cos(ang), 2, axis=1)
    sin = jnp.repeat(jnp.sin(ang), 2, axis=1) * jnp.tile(jnp.asarray([-1.0, 1.0], F32), HEAD_DIM // 2)
    return jnp.tile(cos, (1, ATT_HEADS)), jnp.tile(sin, (1, ATT_HEADS))


def _shifted_rows(x, prev_row, next_row):
    m = x.shape[0]
    row = lax.broadcasted_iota(I32, x.shape, 0)
    prv = jnp.where(row == 0, prev_row, pltpu.roll(x, 1, 0))
    nxt = jnp.where(row == m - 1, next_row, pltpu.roll(x, m - 1, 0))
    return prv, nxt


def _halo_proj(xprev_ref, xnext_ref, g_ref, sh_ref, sc_ref, w_cols):
    halo = jnp.concatenate([xprev_ref[...], xnext_ref[...]], axis=0)
    return _dot(_norm_mod(halo, g_ref[...], sh_ref[0], sc_ref[0]).astype(BF16), w_cols)


def _even_in_kernel(xm_ref, xprev_ref, xnext_ref, g_ref, sh_ref, sc_ref, w_ref,
                    mu_ref, kk_ref, ka_ref, rk_ref, w0_ref, wup_ref, a0_ref, aup_ref, gup_ref, ones_ref,
                    qkv_ref, r_ref, v_ref, al_ref, lw0_ref, be0_ref, kd0_ref, lw1_ref, be1_ref, kd1_ref,
                    bonus_ref, gate_ref, *, lay):
    i = pl.program_id(0)
    _, _, j, per = lay.seq_of_tile(i, TOKEN_TILE)
    n_qkv = ATT_Q + 2 * ATT_KV
    y = _dot(_norm_mod(xm_ref[...], g_ref[...], sh_ref[0], sc_ref[0]).astype(BF16), w_ref[...])
    qkv_ref[...] = y[:, :n_qkv]
    x = y[:, n_qkv:]
    yh = _halo_proj(xprev_ref, xnext_ref, g_ref, sh_ref, sc_ref, w_ref[:, n_qkv:])
    prev_row = jnp.where(j == 0, 0.0, yh[HALO - 1:HALO, :])
    next_row = jnp.where(j == per - 1, 0.0, yh[HALO:HALO + 1, :])
    prv, nxt = _shifted_rows(x, prev_row, next_row)
    u = x + mu_ref[...] * (0.5 * (prv + nxt) - x)

    dim = RWKV_DIM
    r, k, v = u[:, :dim], u[:, dim:2 * dim], u[:, 2 * dim:3 * dim]
    o = 3 * dim
    wd = u[:, o:o + 2 * DECAY_RANK]
    o += 2 * DECAY_RANK
    ad = u[:, o:o + 2 * ICLR_RANK]
    o += 2 * ICLR_RANK
    gd = u[:, o:o + GATE_RANK]

    ones = ones_ref[...]
    kk = k * kk_ref[...]
    norm = jnp.sqrt(_split_dot(kk * kk, ones))
    alpha = kk / jnp.maximum(norm, 1e-12)
    r_ref[...] = r
    v_ref[...] = v
    al_ref[...] = alpha

    tanh_wd = jnp.tanh(wd).astype(BF16)
    ad16 = ad.astype(BF16)
    kd_sum = None
    for d, (lw_ref, be_ref, kd_ref) in enumerate(((lw0_ref, be0_ref, kd0_ref), (lw1_ref, be1_ref, kd1_ref))):
        w_log = w0_ref[d:d + 1, :] + _dot(tanh_wd[:, d * DECAY_RANK:(d + 1) * DECAY_RANK], wup_ref[d])
        lw_ref[...] = -_sigmoid(w_log) * float(np.exp(-0.5))
        a = _sigmoid(a0_ref[d:d + 1, :] + _dot(ad16[:, d * ICLR_RANK:(d + 1) * ICLR_RANK], aup_ref[d]))
        kd = k * (1.0 + (a - 1.0) * ka_ref[...])
        be_ref[...] = alpha * a
        kd_ref[...] = kd
        kd_sum = kd if kd_sum is None else kd_sum + kd
    bonus_ref[...] = _split_dot(r * kd_sum * rk_ref[...], ones) * v
    gate_ref[...] = _dot(_sigmoid(gd).astype(BF16), gup_ref[...])


def _halo_specs(n, d, tile):
    hb = tile // HALO
    n_halo = n // HALO
    return [pl.BlockSpec((HALO, d), lambda i: (jnp.maximum(i * hb - 1, 0), 0)),
            pl.BlockSpec((HALO, d), lambda i: (jnp.minimum((i + 1) * hb, n_halo - 1), 0))]


def even_in(lay, x, g, mods, w_in, prm):
    mu, w0, w_up, a0, a_up, g_up, k_k, k_a, r_k = prm
    n, d = x.shape
    cols = RWKV_COLS
    n_qkv = ATT_Q + 2 * ATT_KV
    tile = TOKEN_TILE
    dim = RWKV_DIM
    const = lambda shape: pl.BlockSpec(shape, lambda i: (0,) * len(shape))
    row = lambda a: a.reshape(1, -1)
    tok = lambda w: pl.BlockSpec((tile, w), lambda i: (i, 0))
    return pl.pallas_call(
        functools.partial(_even_in_kernel, lay=lay),
        grid=(n // tile,),
        in_specs=[tok(d)] + _halo_specs(n, d, tile) + [
            const((1, d)), _mod_spec(lay, 0, tile), _mod_spec(lay, 1, tile), const((d, n_qkv + cols)),
            const((1, cols)), const((1, dim)), const((1, dim)), const((1, dim)),
            const((2, dim)), const((2, DECAY_RANK, dim)), const((2, dim)), const((2, ICLR_RANK, dim)),
            const((GATE_RANK, dim)), const((dim, dim)),
        ],
        out_specs=[tok(n_qkv)] + [tok(dim)] * 11,
        out_shape=[jax.ShapeDtypeStruct((n, n_qkv), F32)] + [jax.ShapeDtypeStruct((n, dim), F32)] * 11,
        compiler_params=_cparams(("arbitrary",)),
        name="even_in",
    )(x, x, x, g.reshape(1, d), mods, mods, w_in.astype(BF16),
      row(mu), row(k_k), row(k_a), row(r_k), w0, w_up.astype(BF16), a0, a_up.astype(BF16),
      g_up.astype(BF16), _block_ones(dim, RWKV_HD))


def _each(fn, *lists):
    return [fn(*args) for args in zip(*lists)]


def _unit_triangular_inverses(l_mats, eye, diag_blocks):
    mm = lambda a, b: _dot(a.astype(BF16), b.astype(BF16))
    ld = [jnp.where(diag_blocks, l, 0.0) for l in l_mats]
    lo = _each(lambda l, d: l - d, l_mats, ld)
    x = [eye - d for d in ld]
    p = ld
    for _ in range(int(np.log2(INV_BLOCK)) - 1):
        p = _each(mm, p, p)
        x = _each(lambda xi, pi: xi + mm(xi, pi), x, p)
    nb = _each(mm, x, lo)
    y = [eye - m for m in nb]
    p = nb
    for _ in range(int(np.log2(RWKV_CHUNK // INV_BLOCK)) - 1):
        p = _each(mm, p, p)
        y = _each(lambda yi, pi: yi + mm(yi, pi), y, p)
    return _each(mm, y, x)


def _chunk_factors(r, alpha, lw, beta, kd, incl01):
    c_incl = _split_dot_left(incl01, lw)
    c_tot = jnp.sum(lw, axis=0, keepdims=True)
    inv_p = jnp.exp(-c_incl)
    to_end = jnp.exp(c_tot - c_incl)
    r_bar = r * jnp.exp(c_incl)
    b16 = lambda m: m.astype(BF16)
    return dict(a_bar=b16(alpha * jnp.exp(c_incl - lw)), r_bar=r_bar, r_bar16=b16(r_bar),
                b_bar=b16(beta * inv_p), k_bar=b16(kd * inv_p), b_til=b16(beta * to_end),
                k_til=b16(kd * to_end), p_tot=jnp.exp(c_tot))


def _rwkv_chunk_kernel(rf_ref, vf_ref, af_ref, lwf_ref, bef_ref, kdf_ref,
                       rb_ref, vb_ref, ab_ref, lwb_ref, beb_ref, kdb_ref, s0_ref,
                       of_ref, ob_ref, sfin_ref, h_scr, *, lay):
    s = pl.program_id(0)
    _, _, c, per = lay.seq_of_tile(s, RWKV_STEP_CHUNKS * RWKV_CHUNK)

    @pl.when(c == 0)
    def _load_state():
        h_scr[...] = s0_ref[0]

    n = RWKV_CHUNK
    hd = RWKV_HD
    row = lax.broadcasted_iota(I32, (n, n), 0)
    col = lax.broadcasted_iota(I32, (n, n), 1)
    eye = (row == col).astype(F32)
    diag_blocks = (row // INV_BLOCK) == (col // INV_BLOCK)
    b16 = lambda m: m.astype(BF16)
    mm = lambda a, b: _dot(b16(a), b16(b))

    fac, strict_m, incl_m, v16, unit_key = [], [], [], [], []
    for d, refs in enumerate(((rf_ref, vf_ref, af_ref, lwf_ref, bef_ref, kdf_ref),
                              (rb_ref, vb_ref, ab_ref, lwb_ref, beb_ref, kdb_ref))):
        r_ref, v_ref, a_ref, lw_ref, be_ref, kd_ref = refs
        strict = (col > row) if d else (col < row)
        incl = (col >= row) if d else (col <= row)
        incl01 = jnp.where(incl, 1.0, 0.0).astype(BF16)
        for j in range(RWKV_STEP_CHUNKS):
            rows = pl.ds(j * n, n)
            f = _chunk_factors(r_ref[rows, :], a_ref[rows, :], lw_ref[rows, :], be_ref[rows, :], kd_ref[rows, :],
                               incl01)
            v = b16(v_ref[rows, :])
            for h in range(RWKV_HEADS):
                sl = slice(h * hd, (h + 1) * hd)
                fac.append({k: a[:, sl] for k, a in f.items()})
                strict_m.append(strict)
                incl_m.append(incl)
                v16.append(v[:, sl])
                unit_key.append((d, j, h))

    gram = [_dot_nt(jnp.concatenate([f["a_bar"], f["r_bar16"]], axis=0),
                    jnp.concatenate([f["b_bar"], f["k_bar"]], axis=0)) for f in fac]
    l_mat = _each(lambda g, m: jnp.where(m, g[:n, :n], 0.0), gram, strict_m)
    a_k = _each(lambda g, m: jnp.where(m, g[:n, n:], 0.0), gram, strict_m)
    r_b = _each(lambda g, m: jnp.where(m, g[n:, :n], 0.0), gram, incl_m)
    r_k = _each(lambda g, m: jnp.where(m, g[n:, n:], 0.0), gram, incl_m)
    akv = _each(mm, a_k, v16)
    kt_v = _each(lambda f, v: _dot_tn(f["k_til"], v), fac, v16)
    rk_v = _each(mm, r_k, v16)
    t_inv = _unit_triangular_inverses(l_mat, eye, diag_blocks)
    mw = _each(lambda t, f, w: b16(mm(t, jnp.concatenate([f["a_bar"], b16(w)], axis=1))), t_inv, fac, akv)
    bt_mw = _each(lambda f, m: _dot_tn(f["b_til"], m), fac, mw)
    rb_mw = _each(mm, r_b, mw)
    q_eff = _each(lambda f, rb: b16(f["r_bar"] - rb[:, :hd]), fac, rb_mw)
    o_intra = _each(lambda rk, rb: rk - rb[:, hd:], rk_v, rb_mw)
    d_mat = _each(lambda kv, bt: kv - bt[:, hd:], kt_v, bt_mw)
    decay_col = [jnp.sum(eye * f["p_tot"], axis=1, keepdims=True) for f in fac]
    unit = {key: u for u, key in enumerate(unit_key)}

    for d, o_ref in enumerate((of_ref, ob_ref)):
        h_cur = [h_scr[d * RWKV_HEADS + h] for h in range(RWKV_HEADS)]
        order = range(RWKV_STEP_CHUNKS - 1, -1, -1) if d else range(RWKV_STEP_CHUNKS)
        for j in order:
            outs = []
            for h in range(RWKV_HEADS):
                u = unit[(d, j, h)]
                h16 = b16(h_cur[h])
                outs.append(_dot(q_eff[u], h16) + o_intra[u])
                h_cur[h] = decay_col[u] * h_cur[h] - mm(bt_mw[u][:, :hd], h16) + d_mat[u]
            o_ref[pl.ds(j * n, n), :] = jnp.concatenate(outs, axis=1)
        for h in range(RWKV_HEADS):
            h_scr[d * RWKV_HEADS + h] = h_cur[h]

    @pl.when(c == per - 1)
    def _store_state():
        sfin_ref[0] = h_scr[...]


def rwkv_chunks(lay, r, v, alpha, lw0, be0, kd0, lw1, be1, kd1, s0):
    n, dim = r.shape
    rows = RWKV_STEP_CHUNKS * RWKV_CHUNK
    n_seq = lay.bp + lay.bs
    assert lay.tp % rows == 0 and lay.ts % rows == 0

    def fwd(s):
        return (s, 0)

    def bwd(s):
        _, _, c, per = lay.seq_of_tile(s, rows)
        return (s - c + (per - 1 - c), 0)

    def seq(s):
        is_p, q, _, _ = lay.seq_of_tile(s, rows)
        return (jnp.where(is_p, q, lay.bp + q), 0, 0, 0)

    state_block = (1, 2 * RWKV_HEADS, RWKV_HD, RWKV_HD)
    return pl.pallas_call(
        functools.partial(_rwkv_chunk_kernel, lay=lay),
        grid=(n // rows,),
        in_specs=[pl.BlockSpec((rows, dim), fwd)] * 6 + [pl.BlockSpec((rows, dim), bwd)] * 6
        + [pl.BlockSpec(state_block, seq)],
        out_specs=[pl.BlockSpec((rows, dim), fwd), pl.BlockSpec((rows, dim), bwd), pl.BlockSpec(state_block, seq)],
        out_shape=[jax.ShapeDtypeStruct((n, dim), F32), jax.ShapeDtypeStruct((n, dim), F32),
                   jax.ShapeDtypeStruct((n_seq,) + state_block[1:], F32)],
        scratch_shapes=[pltpu.VMEM(state_block[1:], F32)],
        compiler_params=_cparams(("arbitrary",)),
        name="rwkv_chunks",
    )(r, v, alpha, lw0, be0, kd0, r, v, alpha, lw1, be1, kd1, s0)


def _even_out_kernel(x_ref, attp_ref, atts_ref, of_ref, ob_ref, bonus_ref, gate_ref, lng_ref, lnb_ref, ones_ref,
                     wa_ref, wr_ref, g1_ref, o_ref, *, lay):
    is_context = pl.program_id(0) < lay.n_p // TOKEN_TILE
    att = jnp.where(is_context, attp_ref[...], atts_ref[...])
    ones = ones_ref[...]
    osum = of_ref[...] + ob_ref[...]
    mean = _split_dot(osum, ones) * (1.0 / RWKV_HD)
    cen = osum - mean
    var = _split_dot(cen * cen, ones) * (1.0 / RWKV_HD)
    on = cen * lax.rsqrt(var + GN_EPS) * lng_ref[...] + lnb_ref[...]
    rw = (on + bonus_ref[...]) * gate_ref[...]
    y = _dot(att, wa_ref[...]) + _dot(rw.astype(BF16), wr_ref[...])
    o_ref[...] = x_ref[...] + g1_ref[0] * y


def even_out(lay, x, att_p, att_s, o_f, o_b, bonus, gate, ln_g, ln_b, w_out, mods):
    n, d = x.shape
    tile = TOKEN_TILE
    npt = lay.n_p // tile
    dim = RWKV_DIM
    const = lambda shape: pl.BlockSpec(shape, lambda i: (0,) * len(shape))
    tok = lambda w: pl.BlockSpec((tile, w), lambda i: (i, 0))
    w16 = w_out.astype(BF16)
    return pl.pallas_call(
        functools.partial(_even_out_kernel, lay=lay),
        grid=(n // tile,),
        in_specs=[tok(d),
                  pl.BlockSpec((tile, ATT_Q), lambda i: (jnp.minimum(i, npt - 1), 0)),
                  pl.BlockSpec((tile, ATT_Q), lambda i: (jnp.maximum(i - npt, 0), 0)),
                  tok(dim), tok(dim), tok(dim), tok(dim),
                  const((1, dim)), const((1, dim)), const((dim, dim)),
                  const((ATT_Q, d)), const((dim, d)), _mod_spec(lay, 2, tile)],
        out_specs=tok(d),
        out_shape=jax.ShapeDtypeStruct((n, d), F32),
        compiler_params=_cparams(("arbitrary",)),
        name="even_out",
    )(x, att_p, att_s, o_f, o_b, bonus, gate, ln_g.reshape(1, dim), ln_b.reshape(1, dim), _block_ones(dim, RWKV_HD),
      w16[:ATT_Q], w16[ATT_Q:], mods)


def _gelu_tanh(x):
    return 0.5 * x * (1.0 + jnp.tanh(float(np.sqrt(2.0 / np.pi)) * (x + 0.044715 * (x * x * x))))


def _softplus(x):
    return jnp.maximum(x, 0.0) + jnp.log(1.0 + jnp.exp(-jnp.abs(x)))


def _odd_in_kernel(xm_ref, xprev_ref, xnext_ref, g_ref, sh_ref, sc_ref, w_ref, cw_ref, cb_ref,
                   gate_ref, u_ref, *, lay):
    i = pl.program_id(0)
    _, _, j, per = lay.seq_of_tile(i, TOKEN_TILE)
    tile = TOKEN_TILE
    y = _dot(_norm_mod(xm_ref[...], g_ref[...], sh_ref[0], sc_ref[0]).astype(BF16), w_ref[...])
    gate_ref[...] = _gelu_tanh(y[:, :D_RNN])
    x = y[:, D_RNN:]
    yh = _halo_proj(xprev_ref, xnext_ref, g_ref, sh_ref, sc_ref, w_ref[:, D_RNN:])
    first = j == 0
    last = j == per - 1
    row = lax.broadcasted_iota(I32, x.shape, 0)
    p1 = jnp.where(first, 0.0, yh[HALO - 1:HALO, :])
    p2 = jnp.where(first, 0.0, yh[HALO - 2:HALO - 1, :])
    n1 = jnp.where(last, 0.0, yh[HALO:HALO + 1, :])
    xm1 = jnp.where(row == 0, p1, pltpu.roll(x, 1, 0))
    xm2 = jnp.where(row == 0, p2, jnp.where(row == 1, p1, pltpu.roll(x, 2, 0)))
    xp1 = jnp.where(row == tile - 1, n1, pltpu.roll(x, tile - 1, 0))
    u_ref[...] = (cb_ref[...] + xm2 * cw_ref[0:1, :] + xm1 * cw_ref[1:2, :] + x * cw_ref[2:3, :]
                  + xp1 * cw_ref[3:4, :])


def odd_in(lay, x, g, mods, w_in, conv_w, conv_b):
    n, d = x.shape
    tile = TOKEN_TILE
    dr = D_RNN
    const = lambda shape: pl.BlockSpec(shape, lambda i: (0,) * len(shape))
    tok = lambda w: pl.BlockSpec((tile, w), lambda i: (i, 0))
    return pl.pallas_call(
        functools.partial(_odd_in_kernel, lay=lay),
        grid=(n // tile,),
        in_specs=[tok(d)] + _halo_specs(n, d, tile) + [
            const((1, d)), _mod_spec(lay, 0, tile), _mod_spec(lay, 1, tile), const((d, 2 * dr)),
            const((CONV_W, dr)), const((1, dr))],
        out_specs=[tok(dr)] * 2,
        out_shape=[jax.ShapeDtypeStruct((n, dr), F32)] * 2,
        compiler_params=_cparams(("arbitrary",)),
        name="odd_in",
    )(x, x, x, g.reshape(1, d), mods, mods, w_in.astype(BF16), conv_w, conv_b.reshape(1, dr))


def _lru_coefficients(u, d, wbd_ref, ba_ref, bx_ref, lam_ref):
    u16 = u.astype(BF16)
    n_grp = D_RNN // LRU_GROUP
    z = [_dot(u16[:, g * LRU_GROUP:(g + 1) * LRU_GROUP], wbd_ref[d, g]) for g in range(n_grp)]
    pick = lambda m: jnp.concatenate([zg[:, m * LRU_GROUP:(m + 1) * LRU_GROUP] for zg in z], axis=1)
    ga = _sigmoid(pick(0) + ba_ref[d:d + 1, :])
    gx = _sigmoid(pick(1) + bx_ref[d:d + 1, :])
    log_a = -LRU_C * ga * _softplus(-lam_ref[d:d + 1, :])
    return jnp.exp(log_a), jnp.sqrt(1.0 - jnp.exp(2.0 * log_a)) * gx * u


def _lru_scan_kernel(uf_ref, ub_ref, wbd_ref, ba_ref, bx_ref, lam_ref, h0_ref, hf_ref, hb_ref,
                     af_ref, bf_ref, ab_ref, bb_ref, carry, *, lay):
    i = pl.program_id(0)
    _, _, j, _ = lay.seq_of_tile(i, TOKEN_TILE)
    tile = TOKEN_TILE

    @pl.when(j == 0)
    def _load_state():
        carry[...] = h0_ref[0]

    af_ref[...], bf_ref[...] = _lru_coefficients(uf_ref[...], 0, wbd_ref, ba_ref, bx_ref, lam_ref)
    ab_ref[...], bb_ref[...] = _lru_coefficients(ub_ref[...], 1, wbd_ref, ba_ref, bx_ref, lam_ref)

    def step(t, hs):
        hf, hb = hs
        tb = tile - 1 - t
        hf = af_ref[pl.ds(t, 1), :] * hf + bf_ref[pl.ds(t, 1), :]
        hb = ab_ref[pl.ds(tb, 1), :] * hb + bb_ref[pl.ds(tb, 1), :]
        hf_ref[pl.ds(t, 1), :] = hf
        hb_ref[pl.ds(tb, 1), :] = hb
        return hf, hb

    hf, hb = lax.fori_loop(0, tile, step, (carry[0:1, :], carry[1:2, :]), unroll=8)
    carry[0:1, :] = hf
    carry[1:2, :] = hb


def lru_scan(lay, u, wa, ba, wx, bx, lam, h0):
    n, dr = u.shape
    tile = TOKEN_TILE
    per = LRU_GROUP // LRU_BS
    n_grp = dr // LRU_GROUP

    def block_diag(w):
        w = w.reshape(n_grp, per, LRU_BS, LRU_BS)
        eye = jnp.eye(per, dtype=w.dtype)
        return jnp.einsum("gpcd,pq->gpcqd", w, eye).reshape(n_grp, LRU_GROUP, LRU_GROUP)

    wbd = jnp.stack([jnp.concatenate([block_diag(wa[d]), block_diag(wx[d])], axis=2) for d in range(2)]).astype(BF16)
    const = lambda shape: pl.BlockSpec(shape, lambda i: (0,) * len(shape))

    def fwd(i):
        return (i, 0)

    def bwd(i):
        _, _, j, per = lay.seq_of_tile(i, tile)
        return (i - j + (per - 1 - j), 0)

    def seq(i):
        is_p, q, _, _ = lay.seq_of_tile(i, tile)
        return (jnp.where(is_p, q, lay.bp + q), 0, 0)

    return pl.pallas_call(
        functools.partial(_lru_scan_kernel, lay=lay),
        grid=(n // tile,),
        in_specs=[pl.BlockSpec((tile, dr), fwd), pl.BlockSpec((tile, dr), bwd),
                  const((2, n_grp, LRU_GROUP, 2 * LRU_GROUP)), const((2, dr)), const((2, dr)), const((2, dr)),
                  pl.BlockSpec((1, 2, dr), seq)],
        out_specs=[pl.BlockSpec((tile, dr), fwd), pl.BlockSpec((tile, dr), bwd)],
        out_shape=[jax.ShapeDtypeStruct((n, dr), F32)] * 2,
        scratch_shapes=[pltpu.VMEM((tile, dr), F32)] * 4 + [pltpu.VMEM((2, dr), F32)],
        compiler_params=_cparams(("arbitrary",)),
        name="lru_scan",
    )(u, u, wbd, ba, bx, lam, h0)


def _odd_out_kernel(x_ref, gate_ref, hf_ref, hb_ref, w_ref, g1_ref, o_ref):
    y = _dot((gate_ref[...] * (hf_ref[...] + hb_ref[...])).astype(BF16), w_ref[...])
    o_ref[...] = x_ref[...] + g1_ref[0] * y


def odd_out(lay, x, gate, hf, hb, w_out, mods):
    n, d = x.shape
    tile = TOKEN_TILE
    tok = lambda w: pl.BlockSpec((tile, w), lambda i: (i, 0))
    return pl.pallas_call(
        _odd_out_kernel,
        grid=(n // tile,),
        in_specs=[tok(d), tok(D_RNN), tok(D_RNN), tok(D_RNN), pl.BlockSpec((D_RNN, d), lambda i: (0, 0)),
                  _mod_spec(lay, 2, tile)],
        out_specs=tok(d),
        out_shape=jax.ShapeDtypeStruct((n, d), F32),
        compiler_params=_cparams(("arbitrary",)),
        name="odd_out",
    )(x, gate, hf, hb, w_out.astype(BF16), mods)


def _pack_bf16_pairs(x):
    n = x.shape[1] // 2
    hi = pltpu.bitcast(x[:, :n].astype(BF16).astype(F32), jnp.uint32)
    lo = pltpu.bitcast(x[:, n:].astype(BF16).astype(F32), jnp.uint32)
    return hi | (lo >> 16)


def _unpack_bf16_pairs(u):
    a = pltpu.bitcast(u & jnp.uint32(0xFFFF0000), F32)
    b = pltpu.bitcast(u << 16, F32)
    return jnp.concatenate([a, b], axis=1).astype(BF16)


def _router_kernel(x_ref, g_ref, sh_ref, sc_ref, rw_ref, rb_ref, before_ref,
                   h_ref, idx_ref, gate_ref, rank_ref, count_ref, cnt_scr):
    @pl.when(pl.program_id(0) == 0)
    def _reset():
        cnt_scr[...] = jnp.zeros_like(cnt_scr)

    h = _norm_mod(x_ref[...], g_ref[...], sh_ref[0], sc_ref[0])
    h_ref[...] = _pack_bf16_pairs(h)
    logits = lax.dot_general(rw_ref[...], h, (((1,), (1,)), ((), ())), preferred_element_type=F32,
                             precision=HIGHEST) + rb_ref[...]
    e_id = lax.broadcasted_iota(I32, logits.shape, 0)
    vals, ids, hots = [], [], []
    for _ in range(TOP_K):
        m = jnp.max(logits, axis=0, keepdims=True)
        pick = jnp.min(jnp.where(logits == m, e_id, N_EXPERTS), axis=0, keepdims=True)
        hot = e_id == pick
        vals.append(m)
        ids.append(pick)
        hots.append(jnp.where(hot, 1.0, 0.0))
        logits = jnp.where(hot, -jnp.inf, logits)
    top = jnp.concatenate(vals, axis=0)
    p = jnp.exp(top - top[0:1, :])
    gate_ref[...] = p / jnp.sum(p, axis=0, keepdims=True)
    idx_ref[...] = jnp.concatenate(ids, axis=0)
    hot_all = hots[0] + hots[1] + hots[2] + hots[3]
    seen = cnt_scr[:, 0:1] + _dot(hot_all.astype(BF16), before_ref[...])
    rank_ref[...] = jnp.concatenate([jnp.sum(hk * seen, axis=0, keepdims=True) for hk in hots],
                                    axis=0).astype(I32)
    cnt_scr[...] = cnt_scr[...] + jnp.sum(hot_all, axis=1, keepdims=True)
    count_ref[...] = cnt_scr[...]


def router(lay, x, g, mods, rw, rb):
    n, d = x.shape
    tile = TOKEN_TILE
    before = jnp.asarray(np.triu(np.ones((tile, tile), np.float32), 1), dtype=BF16)
    per_tok = pl.BlockSpec((TOP_K, tile), lambda i: (0, i))
    return pl.pallas_call(
        _router_kernel,
        grid=(n // tile,),
        in_specs=[pl.BlockSpec((tile, d), lambda i: (i, 0)), pl.BlockSpec((1, d), lambda i: (0, 0)),
                  _mod_spec(lay, 3, tile), _mod_spec(lay, 4, tile),
                  pl.BlockSpec((N_EXPERTS, d), lambda i: (0, 0)), pl.BlockSpec((N_EXPERTS, 1), lambda i: (0, 0)),
                  pl.BlockSpec((tile, tile), lambda i: (0, 0))],
        out_specs=[pl.BlockSpec((tile, d // 2), lambda i: (i, 0)), per_tok, per_tok, per_tok,
                   pl.BlockSpec((N_EXPERTS, 128), lambda i: (0, 0))],
        out_shape=[jax.ShapeDtypeStruct((n, d // 2), jnp.uint32), jax.ShapeDtypeStruct((TOP_K, n), I32),
                   jax.ShapeDtypeStruct((TOP_K, n), F32), jax.ShapeDtypeStruct((TOP_K, n), I32),
                   jax.ShapeDtypeStruct((N_EXPERTS, 128), F32)],
        scratch_shapes=[pltpu.VMEM((N_EXPERTS, 128), F32)],
        compiler_params=_cparams(("arbitrary",)),
        name="router",
    )(x, g.reshape(1, d), mods, mods, rw.T, rb.reshape(N_EXPERTS, 1), before)


def _row_copy(src_hbm, dst_vmem, sem, src_row, dst_row):
    return pltpu.make_async_copy(src_hbm.at[pl.ds(src_row, 1)], dst_vmem.at[pl.ds(dst_row, 1)], sem)


def _dispatch_kernel(dest_ref, fill_lo_ref, fill_hi_ref, src_ref, o_ref, row_tok, *, n_tok):
    @pl.when(pl.program_id(0) == 0)
    def _invert():
        def clear(i, _):
            row_tok[i] = 0
            return 0

        def clear_range(e, _):
            lax.fori_loop(fill_lo_ref[e], fill_hi_ref[e], clear, 0)
            return 0

        lax.fori_loop(0, N_EXPERTS, clear_range, 0)
        for k in range(TOP_K):
            def put(t, _, k=k):
                row_tok[dest_ref[k * n_tok + t]] = t
                return 0

            lax.fori_loop(0, n_tok, put, 0, unroll=8)

    base = pl.program_id(0) * MOE_TILE

    def copy_row(r, _):
        o_ref[pl.ds(r, 1), :] = src_ref[pl.ds(row_tok[base + r], 1), :]
        return 0

    lax.fori_loop(0, MOE_TILE, copy_row, 0, unroll=8)


def dispatch_rows(src, dest, fill_lo, fill_hi, n_out):
    n, d = src.shape
    return pl.pallas_call(
        functools.partial(_dispatch_kernel, n_tok=n),
        grid_spec=pltpu.PrefetchScalarGridSpec(
            num_scalar_prefetch=3,
            grid=(n_out // MOE_TILE,),
            in_specs=[pl.BlockSpec((n, d), lambda i, *_: (0, 0), pipeline_mode=pl.Buffered(1))],
            out_specs=pl.BlockSpec((MOE_TILE, d), lambda i, *_: (i, 0)),
            scratch_shapes=[pltpu.SMEM((n_out,), I32)],
        ),
        out_shape=jax.ShapeDtypeStruct((n_out, d), src.dtype),
        compiler_params=_cparams(("arbitrary",)),
        name="moe_dispatch",
    )(dest, fill_lo, fill_hi, src)


def _expert_weight_copies(w1_hbm, w2_hbm, w1_buf, w2_buf, sems, row, slot):
    return (pltpu.make_async_copy(w1_hbm.at[row], w1_buf.at[slot], sems.at[slot, 0]),
            pltpu.make_async_copy(w2_hbm.at[row], w2_buf.at[slot], sems.at[slot, 1]))


def _expert_kernel(blk_e_ref, first_ref, slot_ref, next_e_ref, n_used_ref, x_ref, w1_hbm, b1_ref, w2_hbm, b2_ref,
                   o_ref, w1_buf, w2_buf, w1_scr, w2_scr, sems, *, layer):
    i = pl.program_id(0)
    used = i < n_used_ref[0]
    copies = functools.partial(_expert_weight_copies, w1_hbm, w2_hbm, w1_buf, w2_buf, sems)

    @pl.when(jnp.logical_and(used, first_ref[i] == 1))
    def _switch_expert():
        slot = slot_ref[i]

        @pl.when(i == 0)
        def _fetch_first():
            for cp in copies(layer * N_EXPERTS + blk_e_ref[0], 0):
                cp.start()

        nxt = next_e_ref[i]

        @pl.when(nxt >= 0)
        def _prefetch_next():
            for cp in copies(layer * N_EXPERTS + nxt, 1 - slot):
                cp.start(priority=1)

        for cp in copies(0, slot):
            cp.wait()
        w1_scr[...] = w1_buf[slot].astype(BF16)
        w2_scr[...] = w2_buf[slot].astype(BF16)

    @pl.when(used)
    def _compute():
        hb = _dot(_unpack_bf16_pairs(x_ref[...]), w1_scr[...]) + b1_ref[0]
        glu = jnp.minimum(hb[:, :D_EXPERT], SWIGLU_LIMIT)
        lin = jnp.clip(hb[:, D_EXPERT:], -SWIGLU_LIMIT, SWIGLU_LIMIT)
        act = glu * _sigmoid(SWIGLU_ALPHA * glu) * (lin + 1.0)
        o_ref[...] = _dot(act.astype(BF16), w2_scr[...]) + b2_ref[0]

    @pl.when(jnp.logical_not(used))
    def _clear():
        o_ref[...] = jnp.zeros_like(o_ref)


def experts(x_sorted, blk_expert, count, n_used, layer, w1, b1, w2, b2):
    n_rows = x_sorted.shape[0]
    n_blk = n_rows // MOE_TILE
    d, de2 = w1.shape[1:]
    de = w2.shape[1]
    first = jnp.concatenate([jnp.ones((1,), I32), (blk_expert[1:] != blk_expert[:-1]).astype(I32)])
    slot = (jnp.cumsum(first) - 1) % 2
    e_id = jnp.arange(N_EXPERTS, dtype=I32)
    later_present = jnp.logical_and(e_id[None, :] > e_id[:, None], count[None, :] > 0)
    next_present = jnp.min(jnp.where(later_present, e_id[None, :], N_EXPERTS), axis=1)
    next_e = jnp.where(next_present < N_EXPERTS, next_present, -1)[blk_expert]
    bmap = lambda i, be, *_: (layer * N_EXPERTS + be[i], 0, 0)
    return pl.pallas_call(
        functools.partial(_expert_kernel, layer=layer),
        grid_spec=pltpu.PrefetchScalarGridSpec(
            num_scalar_prefetch=5,
            grid=(n_blk,),
            in_specs=[pl.BlockSpec((MOE_TILE, d // 2), lambda i, *_: (i, 0)),
                      pl.BlockSpec(memory_space=pl.ANY), pl.BlockSpec((1, 1, de2), bmap),
                      pl.BlockSpec(memory_space=pl.ANY), pl.BlockSpec((1, 1, d), bmap)],
            out_specs=pl.BlockSpec((MOE_TILE, d), lambda i, *_: (i, 0)),
            scratch_shapes=[pltpu.VMEM((2, d, de2), F32), pltpu.VMEM((2, de, d), F32),
                            pltpu.VMEM((d, de2), BF16), pltpu.VMEM((de, d), BF16),
                            pltpu.SemaphoreType.DMA((2, 2))],
        ),
        out_shape=jax.ShapeDtypeStruct((n_rows, d), F32),
        compiler_params=_cparams(("arbitrary",)),
        name="moe_experts",
    )(blk_expert, first, slot.astype(I32), next_e.astype(I32), n_used, x_sorted, w1, b1, w2, b2)


def _combine_kernel(dest_ref, y_hbm, x_ref, gate_ref, g2_ref, o_ref, buf, sems, *, n_tok):
    tile = TOKEN_TILE
    i = pl.program_id(0)
    slot = i % 2

    def issue(step, dst_slot):
        def body(r, _):
            for k in range(TOP_K):
                _row_copy(y_hbm, buf.at[dst_slot, k], sems.at[dst_slot],
                          dest_ref[k * n_tok + step * tile + r], r).start(priority=k % 2)
            return 0

        lax.fori_loop(0, tile, body, 0, unroll=2)

    @pl.when(i == 0)
    def _first_tile():
        issue(0, 0)

    @pl.when(i + 1 < pl.num_programs(0))
    def _next_tile():
        issue(i + 1, 1 - slot)

    def wait(r, _):
        for k in range(TOP_K):
            _row_copy(y_hbm, buf.at[slot, k], sems.at[slot], 0, r).wait()
        return 0

    lax.fori_loop(0, tile, wait, 0, unroll=8)
    gate = gate_ref[...]
    acc = buf[slot, 0] * gate[:, 0:1]
    for k in range(1, TOP_K):
        acc = acc + buf[slot, k] * gate[:, k:k + 1]
    o_ref[...] = x_ref[...] + g2_ref[0] * acc


def combine(lay, x, y_sorted, dest, gate, mods):
    n, d = x.shape
    tile = TOKEN_TILE
    return pl.pallas_call(
        functools.partial(_combine_kernel, n_tok=n),
        grid_spec=pltpu.PrefetchScalarGridSpec(
            num_scalar_prefetch=1,
            grid=(n // tile,),
            in_specs=[pl.BlockSpec(memory_space=pl.ANY),
                      pl.BlockSpec((tile, d), lambda i, dest: (i, 0)),
                      pl.BlockSpec((tile, TOP_K), lambda i, dest: (i, 0)),
                      _mod_spec(lay, 5, tile)],
            out_specs=pl.BlockSpec((tile, d), lambda i, dest: (i, 0)),
            scratch_shapes=[pltpu.VMEM((2, TOP_K, tile, d), F32), pltpu.SemaphoreType.DMA((2,))],
        ),
        out_shape=jax.ShapeDtypeStruct((n, d), F32),
        compiler_params=_cparams(("arbitrary",)),
        name="moe_combine",
    )(dest, y_sorted, x, gate, mods)


def moe_layer(lay, x, g, mods, rw, rb, layer, w1, b1, w2, b2):
    n, d = x.shape
    h, idx_t, gate_t, rank_t, count = router(lay, x, g, mods, rw, rb)
    expert = idx_t.reshape(-1)
    rank = rank_t.reshape(-1)
    n_asg = expert.shape[0]
    count = count[:, 0].astype(I32)
    padded = (count + MOE_TILE - 1) // MOE_TILE * MOE_TILE
    pend = jnp.cumsum(padded)
    dest = ((pend - padded)[expert] + rank).astype(I32)
    n_blk = n_asg // MOE_TILE + N_EXPERTS
    blk_start = jnp.arange(n_blk, dtype=I32) * MOE_TILE
    blk_expert = jnp.minimum(jnp.sum((pend[None, :] <= blk_start[:, None]).astype(I32), axis=1), N_EXPERTS - 1)
    n_used = (pend[-1] // MOE_TILE).astype(I32).reshape(1)
    fill_lo = (pend - padded + count).astype(I32)
    fill_hi = pend.at[-1].set(n_blk * MOE_TILE).astype(I32)
    x_sorted = dispatch_rows(h, dest, fill_lo, fill_hi, n_blk * MOE_TILE)
    y_sorted = experts(x_sorted, blk_expert, count, n_used, layer, w1, b1, w2, b2)
    return combine(lay, x, y_sorted, dest, gate_t.T, mods)


def kernel(x_prompt, x_sample, cache_attn_k, cache_attn_v, state_rwkv, state_lru, c, c_ctx,
           mod_w, mod_b, norm1_g, norm2_g, ev_w_in, ev_w_out, q_norm_g, k_norm_g,
           rwkv_mu, rwkv_w0, rwkv_w_up, rwkv_a0, rwkv_a_up, rwkv_g_up, rwkv_k_k, rwkv_k_a,
           rwkv_r_k, rwkv_ln_g, rwkv_ln_b, od_w_in, od_w_out, conv_w, conv_b,
           lru_wa, lru_ba, lru_wx, lru_bx, lru_lambda,
           router_w, router_b, exp_w1, exp_b1, exp_w2, exp_b2):
    bp, tp, d = x_prompt.shape
    bs, ts, _ = x_sample.shape
    depth = mod_w.shape[0]
    lay = Layout(bp, tp, bs, ts)
    assert bs < MOD_ROWS and tp % TOKEN_TILE == 0 and ts % TOKEN_TILE == 0 and d == D_MODEL
    n_p = lay.n_p

    x = jnp.concatenate([x_prompt.reshape(n_p, d), x_sample.reshape(bs * ts, d)], axis=0)
    cvec = jnp.zeros((MOD_ROWS, d), F32).at[:bs].set(c).at[bs].set(c_ctx)
    mods_all = modulation(cvec, mod_w, mod_b)
    rope = rope_tables(ts)
    n_le = depth * N_EXPERTS
    expert_prm = (exp_w1.reshape(n_le, d, -1), exp_b1.reshape(n_le, 1, -1),
                  exp_w2.reshape(n_le, -1, d), exp_b2.reshape(n_le, 1, d))

    new_k, new_v, new_rw, new_lru = [], [], [], []
    for l in range(depth):
        j = l // 2
        mods = mods_all[l]
        if l % 2 == 0:
            prm = (rwkv_mu[j], rwkv_w0[j], rwkv_w_up[j], rwkv_a0[j], rwkv_a_up[j], rwkv_g_up[j],
                   rwkv_k_k[j], rwkv_k_a[j], rwkv_r_k[j].reshape(-1))
            qkv, r, v, alpha, lw0, be0, kd0, lw1, be1, kd1, bonus, gate = even_in(
                lay, x, norm1_g[l], mods, ev_w_in[j], prm)
            att_p, k_norm = attention(qkv, 0, bp, tp, q_norm_g[j], k_norm_g[j])
            cache = (cache_attn_k[:, j].reshape(bs, -1, ATT_KV), cache_attn_v[:, j].reshape(bs, -1, ATT_KV))
            (att_s,) = attention(qkv, n_p, bs, ts, q_norm_g[j], k_norm_g[j], cache=cache, rope=rope)
            s_lat = jnp.swapaxes(state_rwkv[:, j], -1, -2).reshape(bs, 2 * RWKV_HEADS, RWKV_HD, RWKV_HD)
            s0 = jnp.concatenate([jnp.zeros((bp,) + s_lat.shape[1:], F32), s_lat], axis=0)
            o_f, o_b, s_fin = rwkv_chunks(lay, r, v, alpha, lw0, be0, kd0, lw1, be1, kd1, s0)
            x = even_out(lay, x, att_p, att_s, o_f, o_b, bonus, gate, rwkv_ln_g[j], rwkv_ln_b[j], ev_w_out[j], mods)
            new_k.append(k_norm.reshape(bp, tp, ATT_KV_HEADS, HEAD_DIM))
            new_v.append(qkv[:n_p, ATT_Q + ATT_KV:].reshape(bp, tp, ATT_KV_HEADS, HEAD_DIM))
            new_rw.append(jnp.swapaxes(s_fin[:bp].reshape(bp, 2, RWKV_HEADS, RWKV_HD, RWKV_HD), -1, -2))
        else:
            gate, u = odd_in(lay, x, norm1_g[l], mods, od_w_in[j], conv_w[j], conv_b[j])
            h0 = jnp.concatenate([jnp.zeros((bp, 2, D_RNN), F32), state_lru[:, j]], axis=0)
            hf, hb = lru_scan(lay, u, lru_wa[j], lru_ba[j], lru_wx[j], lru_bx[j], lru_lambda[j], h0)
            x = odd_out(lay, x, gate, hf, hb, od_w_out[j], mods)
            hf_p = hf[:n_p].reshape(bp, tp, D_RNN)
            hb_p = hb[:n_p].reshape(bp, tp, D_RNN)
            new_lru.append(jnp.stack([hf_p[:, -1], hb_p[:, 0]], axis=1))
        x = moe_layer(lay, x, norm2_g[l], mods, router_w[l], router_b[l], l, *expert_prm)

    y_prompt = x[:n_p].reshape(bp, tp, d)
    y_sample = x[n_p:].reshape(bs, ts, d)
    return (y_prompt, y_sample, jnp.stack(new_k, axis=1), jnp.stack(new_v, axis=1), jnp.stack(new_rw, axis=1),
            jnp.stack(new_lru, axis=1))
```
